```python
import math
import jax
import jax.numpy as jnp
from jax import lax
import numpy as np

D_MODEL = 2048
BATCH = 1
SEQ = 16384
DEPTH = 2

GRID_W = 64
CTX_LEN = 256
HEAD_DIM = 64
ROPE_BASE = 10000.0
ROPE_PAIRS_PER_AXIS = HEAD_DIM // 4
NORM_EPS = 1e-6
LN_EPS = 1e-5
Q_BLOCK = 128
ATTN_SCALE = HEAD_DIM ** -0.5
NEG_INF = -1e30
N_MOD = 6
MIX_WIDTH = D_MODEL

A_WIDTH = MIX_WIDTH // 2
A_IN = 2 * A_WIDTH
CONV_A_WIDTH = 31
B_HEADS = (MIX_WIDTH // 2) // (2 * HEAD_DIM)
B_WIDTH = B_HEADS * 2 * HEAD_DIM

C_WIDTH = MIX_WIDTH // 2
C_IN = 3 * C_WIDTH
CONV_C_WIDTH = 3
D_HEADS = (MIX_WIDTH // 2) // HEAD_DIM
D_KV_HEADS = 4
D_GROUP = D_HEADS // D_KV_HEADS
D_Q_WIDTH = D_HEADS * HEAD_DIM
D_KV_WIDTH = D_KV_HEADS * HEAD_DIM
WINDOW = 128

D_FF = 4 * D_MODEL

kernel_name = 'hybrid_diffusion_convmod_diffattn_shortconv_swa'


def rmsnorm(x, g):
    xf = x.astype(jnp.float32)
    y = xf * lax.rsqrt(jnp.mean(xf * xf, axis=-1, keepdims=True) + NORM_EPS)
    return (y * g.astype(jnp.float32)).astype(x.dtype)


def layernorm(x, g, b):
    xf = x.astype(jnp.float32)
    mu = jnp.mean(xf, axis=-1, keepdims=True)
    var = jnp.mean(jnp.square(xf - mu), axis=-1, keepdims=True)
    y = (xf - mu) * lax.rsqrt(var + LN_EPS) * g.astype(jnp.float32) + b.astype(jnp.float32)
    return y.astype(x.dtype)


def adaln(cvec, w, b):
    m = jax.nn.silu(cvec) @ w + b
    return [t[:, None, :] for t in jnp.split(m, N_MOD, axis=-1)]


def sandwich_in(s, g_pre, shift, scale):
    return rmsnorm(s, g_pre) * (1.0 + scale) + shift


def sandwich_out(s, y, g_post, gate):
    return s + gate * rmsnorm(y, g_post)


def sq_relu_mlp(h, w1, w2):
    return jnp.square(jax.nn.relu(h @ w1)) @ w2


def axial_rope(n):
    rows = n // GRID_W
    row = jnp.repeat(jnp.arange(rows, dtype=jnp.float32), GRID_W)
    col = jnp.tile(jnp.arange(GRID_W, dtype=jnp.float32), rows)
    inv = jnp.power(ROPE_BASE, -jnp.arange(ROPE_PAIRS_PER_AXIS, dtype=jnp.float32) / ROPE_PAIRS_PER_AXIS)
    ang = jnp.concatenate([row[:, None] * inv, col[:, None] * inv], axis=-1)
    return jnp.cos(ang), jnp.sin(ang)


def apply_rope(x, cos, sin):
    half = HEAD_DIM // 2
    shape = (1, x.shape[1]) + (1,) * (x.ndim - 3) + (half,)
    cos = cos.reshape(shape).astype(x.dtype)
    sin = sin.reshape(shape).astype(x.dtype)
    x1 = x[..., :half]
    x2 = x[..., half:]
    return jnp.concatenate([x1 * cos - x2 * sin, x2 * cos + x1 * sin], axis=-1)


def depthwise_conv(u, w):
    k = w.shape[0]
    return lax.conv_general_dilated(
        u, w[:, None, :].astype(u.dtype), window_strides=(1,), padding=[(k // 2, k // 2)],
        dimension_numbers=('NWC', 'WIO', 'NWC'), feature_group_count=u.shape[-1])


def conformer_conv(u2, conv_w, conv_b, ln_g, ln_b):
    a, g = jnp.split(u2, 2, axis=-1)
    u = a * jax.nn.sigmoid(g)
    u = depthwise_conv(u, conv_w) + conv_b
    u = layernorm(u, ln_g, ln_b)
    return jax.nn.silu(u)


def short_gated_conv(part, w):
    b_g, c_g, x_in = jnp.split(part, 3, axis=-1)
    return b_g * depthwise_conv(c_g * x_in, w)


def lambda_init_at(depth):
    return 0.8 - 0.6 * math.exp(-0.3 * depth)


def diff_lambda(lq1, lk1, lq2, lk2, lam_init):
    f = jnp.float32
    return (jnp.exp(jnp.sum(lq1.astype(f) * lk1.astype(f)))
            - jnp.exp(jnp.sum(lq2.astype(f) * lk2.astype(f))) + lam_init)


def diff_attend(q, k, v, lam):
    s = jnp.einsum('bqhmd,bkhmd->bhmqk', q, k).astype(jnp.float32) * ATTN_SCALE
    p = jax.nn.softmax(s, axis=-1)
    w = p[:, :, 0] - lam * p[:, :, 1]
    return jnp.einsum('bhqk,bkhe->bqhe', w.astype(v.dtype), v)


def diff_head_norm(o, g, lam_init):
    bsz, n = o.shape[:2]
    return (rmsnorm(o, g) * (1.0 - lam_init)).reshape(bsz, n, B_WIDTH)


def sink_attend(q, k_c, v_c, sink, kw=None, vw=None, valid=None):
    s_c = jnp.einsum('bqkgd,bckd->bkgqc', q, k_c).astype(jnp.float32) * ATTN_SCALE
    parts = [s_c]
    if kw is not None:
        s_w = jnp.einsum('bqkgd,bjkd->bkgqj', q, kw).astype(jnp.float32) * ATTN_SCALE
        parts.append(jnp.where(valid, s_w, NEG_INF))
    sk = jnp.broadcast_to(sink.astype(jnp.float32).reshape(1, D_KV_HEADS, D_GROUP, 1, 1), s_c.shape[:-1] + (1,))
    p = jax.nn.softmax(jnp.concatenate(parts + [sk], axis=-1), axis=-1)
    n_c = k_c.shape[1]
    o = jnp.einsum('bkgqc,bckd->bqkgd', p[..., :n_c].astype(v_c.dtype), v_c)
    if kw is not None:
        o = o + jnp.einsum('bkgqj,bjkd->bqkgd', p[..., n_c:-1].astype(vw.dtype), vw)
    return o


def window_gqa(q, k, v, k_c, v_c, sink):
    bsz, n = q.shape[:2]
    nb = n // Q_BLOCK
    pad = ((0, 0), (Q_BLOCK, Q_BLOCK), (0, 0), (0, 0))
    kp = jnp.pad(k, pad)
    vp = jnp.pad(v, pad)
    qb = q.reshape(bsz, nb, Q_BLOCK, D_KV_HEADS, D_GROUP, HEAD_DIM).swapaxes(0, 1)
    qi = jnp.arange(Q_BLOCK)
    kj = jnp.arange(3 * Q_BLOCK)
    band = jnp.abs(kj[None, :] - Q_BLOCK - qi[:, None]) <= WINDOW

    def block(args):
        blk, qblk = args
        start = blk * Q_BLOCK
        kw = lax.dynamic_slice_in_dim(kp, start, 3 * Q_BLOCK, axis=1)
        vw = lax.dynamic_slice_in_dim(vp, start, 3 * Q_BLOCK, axis=1)
        pos = start - Q_BLOCK + kj
        valid = band & ((pos >= 0) & (pos < n))[None, :]
        return sink_attend(qblk, k_c, v_c, sink, kw, vw, valid)

    o = lax.map(block, (jnp.arange(nb), qb))
    return o.swapaxes(0, 1).reshape(bsz, n, D_Q_WIDTH)


def mixer_ab(h, hc, p, depth, cos, sin, need_ctx):
    bsz, n, _ = h.shape
    nc = hc.shape[1]
    w_in = p['w_in']
    lam_init = lambda_init_at(depth)
    lam = diff_lambda(p['lambda_q1'], p['lambda_k1'], p['lambda_q2'], p['lambda_k2'], lam_init)
    proj = h @ w_in
    a_lat = conformer_conv(proj[..., :A_IN], p['conv_w'], p['conv_b'], p['ln_g'], p['ln_b'])
    q, k, v = jnp.split(proj[..., A_IN:], 3, axis=-1)
    q = apply_rope(q.reshape(bsz, n, B_HEADS, 2, HEAD_DIM), cos, sin)
    k = apply_rope(k.reshape(bsz, n, B_HEADS, 2, HEAD_DIM), cos, sin)
    v = v.reshape(bsz, n, B_HEADS, 2 * HEAD_DIM)
    k_c, v_c = jnp.split(hc @ w_in[:, A_IN + B_WIDTH:], 2, axis=-1)
    k_c = k_c.reshape(bsz, nc, B_HEADS, 2, HEAD_DIM)
    v_c = v_c.reshape(bsz, nc, B_HEADS, 2 * HEAD_DIM)
    k_all = jnp.concatenate([k_c, k], axis=1)
    v_all = jnp.concatenate([v_c, v], axis=1)
    nb = n // Q_BLOCK
    qb = q.reshape(bsz, nb, Q_BLOCK, B_HEADS, 2, HEAD_DIM).swapaxes(0, 1)
    o = lax.map(lambda qblk: diff_attend(qblk, k_all, v_all, lam), qb)
    o = o.swapaxes(0, 1).reshape(bsz, n, B_HEADS, 2 * HEAD_DIM)
    b_lat = diff_head_norm(o, p['subln_g'], lam_init)
    y = jnp.concatenate([a_lat, b_lat], axis=-1) @ p['w_out']
    yc = None
    if need_ctx:
        pc = hc @ w_in[:, :A_IN + B_WIDTH]
        a_ctx = conformer_conv(pc[..., :A_IN], p['conv_w'], p['conv_b'], p['ln_g'], p['ln_b'])
        q_c = pc[..., A_IN:].reshape(bsz, nc, B_HEADS, 2, HEAD_DIM)
        b_ctx = diff_head_norm(diff_attend(q_c, k_c, v_c, lam), p['subln_g'], lam_init)
        yc = jnp.concatenate([a_ctx, b_ctx], axis=-1) @ p['w_out']
    return y, yc


def mixer_cd(h, hc, p, cos, sin, need_ctx):
    bsz, n, _ = h.shape
    nc = hc.shape[1]
    w_in = p['w_in']
    proj = h @ w_in
    c_lat = short_gated_conv(proj[..., :C_IN], p['sconv_w'])
    q = proj[..., C_IN:C_IN + D_Q_WIDTH].reshape(bsz, n, D_KV_HEADS, D_GROUP, HEAD_DIM)
    k = proj[..., C_IN + D_Q_WIDTH:C_IN + D_Q_WIDTH + D_KV_WIDTH].reshape(bsz, n, D_KV_HEADS, HEAD_DIM)
    v = proj[..., C_IN + D_Q_WIDTH + D_KV_WIDTH:].reshape(bsz, n, D_KV_HEADS, HEAD_DIM)
    q = apply_rope(q, cos, sin)
    k = apply_rope(k, cos, sin)
    k_c, v_c = jnp.split(hc @ w_in[:, C_IN + D_Q_WIDTH:], 2, axis=-1)
    k_c = k_c.reshape(bsz, nc, D_KV_HEADS, HEAD_DIM)
    v_c = v_c.reshape(bsz, nc, D_KV_HEADS, HEAD_DIM)
    d_lat = window_gqa(q, k, v, k_c, v_c, p['sink'])
    y = jnp.concatenate([c_lat, d_lat], axis=-1) @ p['w_out']
    yc = None
    if need_ctx:
        pc = hc @ w_in[:, :C_IN + D_Q_WIDTH]
        c_ctx_out = short_gated_conv(pc[..., :C_IN], p['sconv_w'])
        q_c = pc[..., C_IN:].reshape(bsz, nc, D_KV_HEADS, D_GROUP, HEAD_DIM)
        d_ctx = sink_attend(q_c, k_c, v_c, p['sink']).reshape(bsz, nc, D_Q_WIDTH)
        yc = jnp.concatenate([c_ctx_out, d_ctx], axis=-1) @ p['w_out']
    return y, yc


def setup_inputs(seed: int = 0) -> dict:
    key = jax.random.key(seed)
    ks = iter(jax.random.split(key, 64))
    d = D_MODEL

    def nrm(shape, scale):
        return jax.random.normal(next(ks), shape, jnp.float32) * scale

    def gain(width):
        return 1.0 + nrm((width,), 0.05)

    inp = {}
    inp['x'] = nrm((BATCH, SEQ, d), 1.0)
    inp['c'] = nrm((BATCH, d), 1.0)
    inp['ctx'] = nrm((BATCH, CTX_LEN, d), 1.0)
    inp['c_ctx'] = nrm((d,), 1.0)
    inp['l0_mod_w'] = nrm((d, N_MOD * d), 0.5 * d ** -0.5)
    inp['l0_mod_b'] = nrm((N_MOD * d,), 0.01)
    inp['l0_norm_mix_pre'] = gain(d)
    inp['l0_norm_mix_post'] = gain(d)
    inp['l0_norm_mlp_pre'] = gain(d)
    inp['l0_norm_mlp_post'] = gain(d)
    inp['l0_w_in'] = nrm((d, A_IN + 3 * B_WIDTH), d ** -0.5)
    inp['l0_conv_w'] = nrm((CONV_A_WIDTH, A_WIDTH), CONV_A_WIDTH ** -0.5)
    inp['l0_conv_b'] = nrm((A_WIDTH,), 0.01)
    inp['l0_ln_g'] = gain(A_WIDTH)
    inp['l0_ln_b'] = nrm((A_WIDTH,), 0.01)
    inp['l0_lambda_q1'] = nrm((HEAD_DIM,), 0.1)
    inp['l0_lambda_k1'] = nrm((HEAD_DIM,), 0.1)
    inp['l0_lambda_q2'] = nrm((HEAD_DIM,), 0.1)
    inp['l0_lambda_k2'] = nrm((HEAD_DIM,), 0.1)
    inp['l0_subln_g'] = gain(2 * HEAD_DIM)
    inp['l0_w_out'] = nrm((MIX_WIDTH, d), MIX_WIDTH ** -0.5)
    inp['l0_mlp_w1'] = nrm((d, D_FF), d ** -0.5)
    inp['l0_mlp_w2'] = nrm((D_FF, d), D_FF ** -0.5)
    inp['l1_mod_w'] = nrm((d, N_MOD * d), 0.5 * d ** -0.5)
    inp['l1_mod_b'] = nrm((N_MOD * d,), 0.01)
    inp['l1_norm_mix_pre'] = gain(d)
    inp['l1_norm_mix_post'] = gain(d)
    inp['l1_norm_mlp_pre'] = gain(d)
    inp['l1_norm_mlp_post'] = gain(d)
    inp['l1_w_in'] = nrm((d, C_IN + D_Q_WIDTH + 2 * D_KV_WIDTH), d ** -0.5)
    inp['l1_sconv_w'] = nrm((CONV_C_WIDTH, C_WIDTH), CONV_C_WIDTH ** -0.5)
    inp['l1_sink'] = nrm((D_HEADS,), 0.5)
    inp['l1_w_out'] = nrm((MIX_WIDTH, d), MIX_WIDTH ** -0.5)
    inp['l1_mlp_w1'] = nrm((d, D_FF), d ** -0.5)
    inp['l1_mlp_w2'] = nrm((D_FF, d), D_FF ** -0.5)
    return inp


def reference(x, c, ctx, c_ctx,
              l0_mod_w, l0_mod_b, l0_norm_mix_pre, l0_norm_mix_post, l0_norm_mlp_pre, l0_norm_mlp_post,
              l0_w_in, l0_conv_w, l0_conv_b, l0_ln_g, l0_ln_b,
              l0_lambda_q1, l0_lambda_k1, l0_lambda_q2, l0_lambda_k2, l0_subln_g,
              l0_w_out, l0_mlp_w1, l0_mlp_w2,
              l1_mod_w, l1_mod_b, l1_norm_mix_pre, l1_norm_mix_post, l1_norm_mlp_pre, l1_norm_mlp_post,
              l1_w_in, l1_sconv_w, l1_sink, l1_w_out, l1_mlp_w1, l1_mlp_w2):
    n = x.shape[1]
    cos, sin = axial_rope(n)
    layers = [
        dict(mod_w=l0_mod_w, mod_b=l0_mod_b, norm_mix_pre=l0_norm_mix_pre, norm_mix_post=l0_norm_mix_post,
             norm_mlp_pre=l0_norm_mlp_pre, norm_mlp_post=l0_norm_mlp_post, w_in=l0_w_in,
             conv_w=l0_conv_w, conv_b=l0_conv_b, ln_g=l0_ln_g, ln_b=l0_ln_b,
             lambda_q1=l0_lambda_q1, lambda_k1=l0_lambda_k1, lambda_q2=l0_lambda_q2, lambda_k2=l0_lambda_k2,
             subln_g=l0_subln_g, w_out=l0_w_out, mlp_w1=l0_mlp_w1, mlp_w2=l0_mlp_w2),
        dict(mod_w=l1_mod_w, mod_b=l1_mod_b, norm_mix_pre=l1_norm_mix_pre, norm_mix_post=l1_norm_mix_post,
             norm_mlp_pre=l1_norm_mlp_pre, norm_mlp_post=l1_norm_mlp_post, w_in=l1_w_in,
             sconv_w=l1_sconv_w, sink=l1_sink, w_out=l1_w_out, mlp_w1=l1_mlp_w1, mlp_w2=l1_mlp_w2),
    ]
    for i in range(DEPTH):
        p = layers[i]
        need_ctx = i < DEPTH - 1
        sh_m, sc_m, gt_m, sh_f, sc_f, gt_f = adaln(c, p['mod_w'], p['mod_b'])
        csh_m, csc_m, cgt_m, csh_f, csc_f, cgt_f = adaln(c_ctx[None, :], p['mod_w'], p['mod_b'])
        h = sandwich_in(x, p['norm_mix_pre'], sh_m, sc_m)
        hc = sandwich_in(ctx, p['norm_mix_pre'], csh_m, csc_m)
        if i % 2 == 0:
            y, yc = mixer_ab(h, hc, p, i, cos, sin, need_ctx)
        else:
            y, yc = mixer_cd(h, hc, p, cos, sin, need_ctx)
        x = sandwich_out(x, y, p['norm_mix_post'], gt_m)
        hf = sandwich_in(x, p['norm_mlp_pre'], sh_f, sc_f)
        x = sandwich_out(x, sq_relu_mlp(hf, p['mlp_w1'], p['mlp_w2']), p['norm_mlp_post'], gt_f)
        if need_ctx:
            ctx = sandwich_out(ctx, yc, p['norm_mix_post'], cgt_m)
            hcf = sandwich_in(ctx, p['norm_mlp_pre'], csh_f, csc_f)
            ctx = sandwich_out(ctx, sq_relu_mlp(hcf, p['mlp_w1'], p['mlp_w2']), p['norm_mlp_post'], cgt_f)
    return x
```

```python
import functools
import math

import jax
import jax.numpy as jnp
from jax import lax
from jax.experimental import pallas as pl
from jax.experimental.pallas import tpu as pltpu

F32 = jnp.float32
BF16 = jnp.bfloat16

D_MODEL = 2048
HEAD_DIM = 64
HALF = HEAD_DIM // 2
GRID_W = 64
ROPE_BASE = 10000.0
NORM_EPS = 1e-6
LN_EPS = 1e-5
ATTN_SCALE = HEAD_DIM ** -0.5
NEG_INF = -1e30
N_MOD = 6
WINDOW = 128
LANES = 128
HALO = 16

A_WIDTH = 1024
A_IN = 2 * A_WIDTH
CONV_A_WIDTH = 31
B_HEADS = 8
B_WIDTH = 1024
C_WIDTH = 1024
C_IN = 3 * C_WIDTH
D_HEADS = 16
D_KV_HEADS = 4
D_GROUP = 4
D_Q_WIDTH = 1024
D_KV_WIDTH = 256
D_FF = 4 * D_MODEL

VMEM_LIMIT = 56 * 1024 * 1024


def _params(sem):
    return pltpu.CompilerParams(dimension_semantics=sem, vmem_limit_bytes=VMEM_LIMIT)


def _rms(x, eps=NORM_EPS):
    return x * lax.rsqrt(jnp.mean(x * x, axis=-1, keepdims=True) + eps)


def _dot_nt(a, b):
    return lax.dot_general(a, b, (((1,), (1,)), ((), ())), preferred_element_type=F32)


def _mod_kernel(c_ref, w_ref, b_ref, o_ref):
    c = c_ref[...]
    s = c * jax.nn.sigmoid(c)
    o_ref[...] = jnp.dot(s, w_ref[...], preferred_element_type=F32) + b_ref[...]


def _modulation(cvecs, w, b):
    tn = 1024
    n_out = w.shape[1]
    return pl.pallas_call(
        _mod_kernel,
        grid=(n_out // tn,),
        in_specs=[pl.BlockSpec((8, D_MODEL), lambda j: (0, 0)),
                  pl.BlockSpec((D_MODEL, tn), lambda j: (0, j)),
                  pl.BlockSpec((1, tn), lambda j: (0, j))],
        out_specs=pl.BlockSpec((8, tn), lambda j: (0, j)),
        out_shape=jax.ShapeDtypeStruct((8, n_out), F32),
        compiler_params=_params(("arbitrary",)),
        name="modulation",
    )(cvecs, w, b.reshape(1, n_out))


MODE_PLAIN, MODE_ROPE, MODE_ROPE_SCALE, MODE_SCALE = range(4)


def _rope_chunk(x, cosf, sinf):
    lane = lax.broadcasted_iota(jnp.int32, x.shape, 1)
    swapped = jnp.where((lane & (HEAD_DIM - 1)) < HALF,
                        pltpu.roll(x, LANES - HALF, 1), pltpu.roll(x, HALF, 1))
    return x * cosf + swapped * sinf


def _proj_kernel(*refs, tn, modes, use_rope):
    if use_rope:
        x_ref, g_ref, sc_ref, sh_ref, cos_ref, sin_ref, w_ref, o_ref, h_scr = refs
    else:
        x_ref, g_ref, sc_ref, sh_ref, w_ref, o_ref, h_scr = refs
    j = pl.program_id(1)

    @pl.when(j == 0)
    def _():
        y = _rms(x_ref[...]) * g_ref[...]
        h_scr[...] = (y * (1.0 + sc_ref[...]) + sh_ref[...]).astype(BF16)

    acc = jnp.dot(h_scr[...], w_ref[...], preferred_element_type=F32)
    nchunk = tn // LANES
    patterns = {}
    for jj in range(len(modes) // nchunk):
        patterns.setdefault(tuple(modes[jj * nchunk:(jj + 1) * nchunk]), []).append(jj)

    def epilogue(pat):
        if all(m == MODE_PLAIN for m in pat):
            o_ref[...] = acc.astype(BF16)
            return
        for c, m in enumerate(pat):
            chunk = acc[:, c * LANES:(c + 1) * LANES]
            if m in (MODE_ROPE, MODE_ROPE_SCALE):
                chunk = _rope_chunk(chunk, cos_ref[...], sin_ref[...])
            if m in (MODE_ROPE_SCALE, MODE_SCALE):
                chunk = chunk * ATTN_SCALE
            o_ref[:, c * LANES:(c + 1) * LANES] = chunk.astype(BF16)

    if len(patterns) == 1:
        epilogue(next(iter(patterns)))
    else:
        for pat, js in patterns.items():
            cond = functools.reduce(jnp.logical_or, [j == jj for jj in js])
            pl.when(cond)(functools.partial(epilogue, pat))


def _project(x, g, scale, shift, w, *, tm, tn, rope=None, rope_cols=(), scaled_rope_cols=(), scaled_cols=()):
    n, d = x.shape
    p = w.shape[1]
    modes = [MODE_PLAIN] * (p // LANES)
    for cols, mode in ((rope_cols, MODE_ROPE), (scaled_rope_cols, MODE_ROPE_SCALE), (scaled_cols, MODE_SCALE)):
        for lo, hi in cols:
            for c in range(lo // LANES, hi // LANES):
                modes[c] = mode
    use_rope = rope is not None
    vec = pl.BlockSpec((1, d), lambda i, j: (0, 0))
    in_specs = [pl.BlockSpec((tm, d), lambda i, j: (i, 0)), vec, vec, vec]
    args = [x, g, scale, shift]
    if use_rope:
        tab = pl.BlockSpec((tm, LANES), lambda i, j: (i, 0))
        in_specs += [tab, tab]
        args += list(rope)
    in_specs.append(pl.BlockSpec((d, tn), lambda i, j: (0, j)))
    args.append(w)
    return pl.pallas_call(
        functools.partial(_proj_kernel, tn=tn, modes=tuple(modes), use_rope=use_rope),
        grid=(n // tm, p // tn),
        in_specs=in_specs,
        out_specs=pl.BlockSpec((tm, tn), lambda i, j: (i, j)),
        out_shape=jax.ShapeDtypeStruct((n, p), BF16),
        scratch_shapes=[pltpu.VMEM((tm, d), BF16)],
        compiler_params=_params(("parallel", "arbitrary")),
        name="norm_project",
    )(*args)


CONV_A_ROWS = 32


def _conv_a_kernel(main_ref, left_ref, right_ref, w_ref, cb_ref, lg_ref, lb_ref, o_ref, ext_scr, sh_scr, *, t):
    i = pl.program_id(0)
    last = pl.num_programs(0) - 1

    def glu(ref):
        v = ref[...].astype(F32)
        return v[:, :A_WIDTH] * jax.nn.sigmoid(v[:, A_WIDTH:])

    ext_scr[0:HALO, :] = jnp.where(i > 0, glu(left_ref), 0.0)
    ext_scr[HALO:HALO + t, :] = glu(main_ref)
    ext_scr[HALO + t:, :] = jnp.where(i < last, glu(right_ref), 0.0)
    ext = ext_scr[...]
    rows = t + 2 * HALO
    srows = t + 3 * 8
    sh_scr[0] = ext[:srows]
    for b in range(1, 8):
        sh_scr[b] = pltpu.roll(ext, rows - b, 0)[:srows]

    def chunk(c, carry):
        r0 = pl.multiple_of(c * CONV_A_ROWS, CONV_A_ROWS)
        acc = jnp.zeros((CONV_A_ROWS, A_WIDTH), F32)
        for k in range(CONV_A_WIDTH):
            a, b = divmod(k + 1, 8)
            acc = acc + sh_scr[b, pl.ds(r0 + 8 * a, CONV_A_ROWS), :] * w_ref[k:k + 1, :]
        v = acc + cb_ref[...]
        mu = jnp.mean(v, axis=-1, keepdims=True)
        vc = v - mu
        var = jnp.mean(vc * vc, axis=-1, keepdims=True)
        y = vc * lax.rsqrt(var + LN_EPS) * lg_ref[...] + lb_ref[...]
        o_ref[pl.ds(r0, CONV_A_ROWS), :] = (y * jax.nn.sigmoid(y)).astype(BF16)
        return carry

    lax.fori_loop(0, t // CONV_A_ROWS, chunk, 0)


def _conformer_conv(proj, conv_w, conv_b, ln_g, ln_b, *, t):
    n = proj.shape[0]
    per = t // HALO
    nh = n // HALO
    vec = pl.BlockSpec((1, A_WIDTH), lambda i: (0, 0))
    return pl.pallas_call(
        functools.partial(_conv_a_kernel, t=t),
        grid=(n // t,),
        in_specs=[pl.BlockSpec((t, A_IN), lambda i: (i, 0)),
                  pl.BlockSpec((HALO, A_IN), lambda i: (jnp.maximum(i * per - 1, 0), 0)),
                  pl.BlockSpec((HALO, A_IN), lambda i: (jnp.minimum((i + 1) * per, nh - 1), 0)),
                  pl.BlockSpec((CONV_A_WIDTH, A_WIDTH), lambda i: (0, 0)),
                  vec, vec, vec],
        out_specs=pl.BlockSpec((t, A_WIDTH), lambda i: (i, 0)),
        out_shape=jax.ShapeDtypeStruct((n, A_WIDTH), BF16),
        scratch_shapes=[pltpu.VMEM((t + 2 * HALO, A_WIDTH), F32),
                        pltpu.VMEM((8, t + 24, A_WIDTH), F32)],
        compiler_params=_params(("parallel",)),
        name="conformer_conv",
    )(proj, proj, proj, conv_w, conv_b.reshape(1, -1), ln_g.reshape(1, -1), ln_b.reshape(1, -1))


def _tile_lanes(x, reps):
    return x if reps == 1 else jnp.concatenate([x] * reps, axis=1)


def _diff_attn_kernel(lam_ref, sub_ref, q_ref, k_ref, v_ref, o_ref, qm_scr, m_scr, l_scr, acc_scr, *, lam_init):
    j = pl.program_id(2)
    tk = k_ref.shape[0]

    @pl.when(j == 0)
    def _():
        q = q_ref[...]
        lane = lax.broadcasted_iota(jnp.int32, q.shape, 1)
        zero = jnp.zeros_like(q)
        qm_scr[0] = jnp.where(lane < HEAD_DIM, q, zero)
        qm_scr[1] = jnp.where(lane >= HEAD_DIM, q, zero)
        m_scr[...] = jnp.full(m_scr.shape, -jnp.inf, F32)
        l_scr[...] = jnp.zeros(l_scr.shape, F32)
        acc_scr[...] = jnp.zeros(acc_scr.shape, F32)

    k = k_ref[...]
    v = v_ref[...]
    for m in range(2):
        s = _dot_nt(qm_scr[m], k)
        m_prev = m_scr[m]
        m_new = jnp.maximum(m_prev, jnp.max(s, axis=1, keepdims=True))
        alpha = jnp.exp(m_prev - m_new)
        p = jnp.exp(s - _tile_lanes(m_new, tk // LANES))
        l_scr[m] = alpha * l_scr[m] + jnp.sum(p, axis=1, keepdims=True)
        acc_scr[m] = alpha * acc_scr[m] + jnp.dot(p.astype(BF16), v, preferred_element_type=F32)
        m_scr[m] = m_new

    @pl.when(j == pl.num_programs(2) - 1)
    def _():
        lp = lam_ref[...]
        lam = (jnp.exp(jnp.sum(lp[0:1] * lp[1:2], axis=1, keepdims=True))
               - jnp.exp(jnp.sum(lp[2:3] * lp[3:4], axis=1, keepdims=True)) + lam_init)
        o = acc_scr[0] / l_scr[0] - lam * (acc_scr[1] / l_scr[1])
        o = _rms(o) * sub_ref[...] * (1.0 - lam_init)
        o_ref[...] = o.astype(BF16)


def _diff_attention(lam_rows, subln_g, q_src, q_col0, k_all, v_all, *, tq, tk, lam_init):
    n = q_src.shape[0]
    nk = k_all.shape[0]
    qb = q_col0 // LANES
    return pl.pallas_call(
        functools.partial(_diff_attn_kernel, lam_init=lam_init),
        grid=(B_HEADS, n // tq, nk // tk),
        in_specs=[pl.BlockSpec((8, LANES), lambda h, i, j: (0, 0)),
                  pl.BlockSpec((1, LANES), lambda h, i, j: (0, 0)),
                  pl.BlockSpec((tq, LANES), lambda h, i, j: (i, qb + h)),
                  pl.BlockSpec((tk, LANES), lambda h, i, j: (j, h)),
                  pl.BlockSpec((tk, LANES), lambda h, i, j: (j, h))],
        out_specs=pl.BlockSpec((tq, LANES), lambda h, i, j: (i, h)),
        out_shape=jax.ShapeDtypeStruct((n, B_WIDTH), BF16),
        scratch_shapes=[pltpu.VMEM((2, tq, LANES), BF16),
                        pltpu.VMEM((2, tq, LANES), F32),
                        pltpu.VMEM((2, tq, LANES), F32),
                        pltpu.VMEM((2, tq, LANES), F32)],
        compiler_params=_params(("parallel", "parallel", "arbitrary")),
        name="diff_attention",
    )(lam_rows, subln_g.reshape(1, LANES), q_src, k_all, v_all)


def _conv_c_kernel(main_ref, left_ref, right_ref, w_ref, o_ref, ext_scr, *, t):
    i = pl.program_id(0)
    last = pl.num_programs(0) - 1

    def gated(ref):
        v = ref[...].astype(F32)
        return v[:, C_WIDTH:2 * C_WIDTH] * v[:, 2 * C_WIDTH:]

    ext_scr[0:HALO, :] = jnp.where(i > 0, gated(left_ref), 0.0)
    ext_scr[HALO:HALO + t, :] = gated(main_ref)
    ext_scr[HALO + t:, :] = jnp.where(i < last, gated(right_ref), 0.0)
    ext = ext_scr[...]
    rows = t + 2 * HALO
    prev = pltpu.roll(ext, 1, 0)[HALO:HALO + t]
    nxt = pltpu.roll(ext, rows - 1, 0)[HALO:HALO + t]
    conv = prev * w_ref[0:1, :] + ext[HALO:HALO + t] * w_ref[1:2, :] + nxt * w_ref[2:3, :]
    o_ref[...] = (main_ref[:, :C_WIDTH].astype(F32) * conv).astype(BF16)


def _short_conv(proj, w, *, t):
    n = proj.shape[0]
    per = t // HALO
    nh = n // HALO
    return pl.pallas_call(
        functools.partial(_conv_c_kernel, t=t),
        grid=(n // t,),
        in_specs=[pl.BlockSpec((t, C_IN), lambda i: (i, 0)),
                  pl.BlockSpec((HALO, C_IN), lambda i: (jnp.maximum(i * per - 1, 0), 0)),
                  pl.BlockSpec((HALO, C_IN), lambda i: (jnp.minimum((i + 1) * per, nh - 1), 0)),
                  pl.BlockSpec((3, C_WIDTH), lambda i: (0, 0))],
        out_specs=pl.BlockSpec((t, C_WIDTH), lambda i: (i, 0)),
        out_shape=jax.ShapeDtypeStruct((n, C_WIDTH), BF16),
        scratch_shapes=[pltpu.VMEM((t + 2 * HALO, C_WIDTH), F32)],
        compiler_params=_params(("parallel",)),
        name="short_conv",
    )(proj, proj, proj, w)


def _win_attn_kernel(sink_ref, q_ref, kp_ref, km_ref, kn_ref, vp_ref, vm_ref, vn_ref, kc_ref, vc_ref, o_ref,
                     *, tq, n):
    i = pl.program_id(0)
    kv = pl.program_id(1)
    kw = jnp.concatenate([kp_ref[...], km_ref[...], kn_ref[...]], axis=0)
    vw = jnp.concatenate([vp_ref[...], vm_ref[...], vn_ref[...]], axis=0)
    kc = kc_ref[...]
    vc = vc_ref[...]
    wk = tq + 2 * WINDOW
    row = lax.broadcasted_iota(jnp.int32, (tq, wk), 0)
    col = lax.broadcasted_iota(jnp.int32, (tq, wk), 1)
    pos = i * tq - WINDOW + col
    mask = (col >= row) & (col <= row + 2 * WINDOW) & (pos >= 0) & (pos < n)
    lane = lax.broadcasted_iota(jnp.int32, (tq, LANES), 1)
    for a in range(2 * LANES // LANES):
        qa = q_ref[:, a * LANES:(a + 1) * LANES]
        halves = []
        for hh in range(2):
            sel = (lane < HEAD_DIM) if hh == 0 else (lane >= HEAD_DIM)
            qm = jnp.where(sel, qa, jnp.zeros_like(qa))
            s_c = _dot_nt(qm, kc)
            s_w = jnp.where(mask, _dot_nt(qm, kw), NEG_INF)
            sk = sink_ref[kv * D_GROUP + 2 * a + hh]
            mx = jnp.maximum(jnp.maximum(jnp.max(s_c, axis=1, keepdims=True),
                                         jnp.max(s_w, axis=1, keepdims=True)), sk)
            p_c = jnp.exp(s_c - mx)
            p_w = jnp.exp(s_w - mx)
            den = (jnp.sum(p_c, axis=1, keepdims=True) + jnp.sum(p_w, axis=1, keepdims=True)
                   + jnp.exp(sk - mx))
            num = (jnp.dot(p_c.astype(BF16), vc, preferred_element_type=F32)
                   + jnp.dot(p_w.astype(BF16), vw, preferred_element_type=F32))
            halves.append(num / den)
        o_ref[:, a * LANES:(a + 1) * LANES] = jnp.where(lane < HEAD_DIM, halves[0], halves[1]).astype(BF16)


def _window_attention(sink, proj, q_col0, k_dup, v_dup, kc_dup, vc_dup, *, tq):
    n = proj.shape[0]
    nc = kc_dup.shape[0]
    qb = q_col0 // (2 * LANES)
    per = tq // WINDOW
    nw = n // WINDOW
    prev = lambda i, kv: (jnp.maximum(i * per - 1, 0), kv)
    main = lambda i, kv: (i, kv)
    nxt = lambda i, kv: (jnp.minimum((i + 1) * per, nw - 1), kv)
    halo = lambda f: pl.BlockSpec((WINDOW, LANES), f)
    return pl.pallas_call(
        functools.partial(_win_attn_kernel, tq=tq, n=n),
        grid=(n // tq, D_KV_HEADS),
        in_specs=[pl.BlockSpec(memory_space=pltpu.SMEM),
                  pl.BlockSpec((tq, 2 * LANES), lambda i, kv: (i, qb + kv)),
                  halo(prev), pl.BlockSpec((tq, LANES), main), halo(nxt),
                  halo(prev), pl.BlockSpec((tq, LANES), main), halo(nxt),
                  pl.BlockSpec((nc, LANES), lambda i, kv: (0, kv)),
                  pl.BlockSpec((nc, LANES), lambda i, kv: (0, kv))],
        out_specs=pl.BlockSpec((tq, 2 * LANES), lambda i, kv: (i, kv)),
        out_shape=jax.ShapeDtypeStruct((n, D_Q_WIDTH), BF16),
        compiler_params=_params(("parallel", "parallel")),
        name="window_attention",
    )(sink, proj, k_dup, k_dup, k_dup, v_dup, v_dup, v_dup, kc_dup, vc_dup)


def _dup_heads(x):
    n = x.shape[0]
    x = x.reshape(n, D_KV_HEADS, 1, HEAD_DIM)
    return jnp.broadcast_to(x, (n, D_KV_HEADS, 2, HEAD_DIM)).reshape(n, D_KV_HEADS * LANES)


def _out_kernel(a_ref, b_ref, x_ref, w_ref, gpost_ref, gate_ref, gpre_ref, sc_ref, sh_ref, x1_ref, hf_ref):
    half = a_ref.shape[1]
    y = (jnp.dot(a_ref[...], w_ref[:half, :], preferred_element_type=F32)
         + jnp.dot(b_ref[...], w_ref[half:, :], preferred_element_type=F32))
    x1 = x_ref[...] + gate_ref[...] * (_rms(y) * gpost_ref[...])
    x1_ref[...] = x1
    hf_ref[...] = ((_rms(x1) * gpre_ref[...]) * (1.0 + sc_ref[...]) + sh_ref[...]).astype(BF16)


def _out_project(a, b, x, w, g_post, gate, g_pre, scale, shift, *, tm):
    n, d = x.shape
    half = a.shape[1]
    vec = pl.BlockSpec((1, d), lambda i: (0, 0))
    return pl.pallas_call(
        _out_kernel,
        grid=(n // tm,),
        in_specs=[pl.BlockSpec((tm, half), lambda i: (i, 0)),
                  pl.BlockSpec((tm, half), lambda i: (i, 0)),
                  pl.BlockSpec((tm, d), lambda i: (i, 0)),
                  pl.BlockSpec((2 * half, d), lambda i: (0, 0)),
                  vec, vec, vec, vec, vec],
        out_specs=[pl.BlockSpec((tm, d), lambda i: (i, 0)), pl.BlockSpec((tm, d), lambda i: (i, 0))],
        out_shape=[jax.ShapeDtypeStruct((n, d), F32), jax.ShapeDtypeStruct((n, d), BF16)],
        compiler_params=_params(("parallel",)),
        name="out_project",
    )(a, b, x, w, g_post, gate, g_pre, scale, shift)


def _mlp_kernel(hf_ref, w1_ref, w2_ref, x1_ref, gpost_ref, gate_ref, o_ref, acc_scr):
    f = pl.program_id(1)

    @pl.when(f == 0)
    def _():
        acc_scr[...] = jnp.zeros(acc_scr.shape, F32)

    h = jnp.dot(hf_ref[...], w1_ref[...], preferred_element_type=F32)
    h = jnp.square(jnp.maximum(h, 0.0)).astype(BF16)
    acc_scr[...] += jnp.dot(h, w2_ref[...], preferred_element_type=F32)

    @pl.when(f == pl.num_programs(1) - 1)
    def _():
        o_ref[...] = x1_ref[...] + gate_ref[...] * (_rms(acc_scr[...]) * gpost_ref[...])


def _mlp(hf, w1, w2, x1, g_post, gate, *, tm, tf):
    n, d = x1.shape
    ff = w1.shape[1]
    vec = pl.BlockSpec((1, d), lambda i, f: (0, 0))
    return pl.pallas_call(
        _mlp_kernel,
        grid=(n // tm, ff // tf),
        in_specs=[pl.BlockSpec((tm, d), lambda i, f: (i, 0)),
                  pl.BlockSpec((d, tf), lambda i, f: (0, f)),
                  pl.BlockSpec((tf, d), lambda i, f: (f, 0)),
                  pl.BlockSpec((tm, d), lambda i, f: (i, 0)),
                  vec, vec],
        out_specs=pl.BlockSpec((tm, d), lambda i, f: (i, 0)),
        out_shape=jax.ShapeDtypeStruct((n, d), F32),
        scratch_shapes=[pltpu.VMEM((tm, d), F32)],
        compiler_params=_params(("parallel", "arbitrary")),
        name="mlp",
    )(hf, w1, w2, x1, g_post, gate)


def _rope_tables(n):
    rows = n // GRID_W
    row = jnp.repeat(jnp.arange(rows, dtype=F32), GRID_W)
    col = jnp.tile(jnp.arange(GRID_W, dtype=F32), rows)
    pairs = HEAD_DIM // 4
    inv = jnp.power(ROPE_BASE, -jnp.arange(pairs, dtype=F32) / pairs)
    ang = jnp.concatenate([row[:, None] * inv, col[:, None] * inv], axis=-1)
    cos, sin = jnp.cos(ang), jnp.sin(ang)
    return jnp.tile(cos, (1, 4)), jnp.tile(jnp.concatenate([-sin, sin], axis=-1), (1, 2))


def _row(v):
    return v.reshape(1, -1)


def _split_mod(m):
    return [_row(t) for t in jnp.split(m, N_MOD)]


def kernel(x, c, ctx, c_ctx, l0_mod_w, l0_mod_b, l0_norm_mix_pre, l0_norm_mix_post, l0_norm_mlp_pre, l0_norm_mlp_post, l0_w_in, l0_conv_w, l0_conv_b, l0_ln_g, l0_ln_b, l0_lambda_q1, l0_lambda_k1, l0_lambda_q2, l0_lambda_k2, l0_subln_g, l0_w_out, l0_mlp_w1, l0_mlp_w2, l1_mod_w, l1_mod_b, l1_norm_mix_pre, l1_norm_mix_post, l1_norm_mlp_pre, l1_norm_mlp_post, l1_w_in, l1_sconv_w, l1_sink, l1_w_out, l1_mlp_w1, l1_mlp_w2):
    n = x.shape[1]
    nc = ctx.shape[1]
    xs = x[0]
    cs = ctx[0]
    rope = _rope_tables(n)
    cvecs = jnp.zeros((8, D_MODEL), F32).at[0].set(c[0]).at[1].set(c_ctx)

    mod = _modulation(cvecs, l0_mod_w, l0_mod_b)
    sh_m, sc_m, gt_m, sh_f, sc_f, gt_f = _split_mod(mod[0])
    csh_m, csc_m, cgt_m, csh_f, csc_f, cgt_f = _split_mod(mod[1])
    w_in = l0_w_in.astype(BF16)
    w_out = l0_w_out.astype(BF16)
    w1 = l0_mlp_w1.astype(BF16)
    w2 = l0_mlp_w2.astype(BF16)
    g_pre, g_post = _row(l0_norm_mix_pre), _row(l0_norm_mix_post)
    gf_pre, gf_post = _row(l0_norm_mlp_pre), _row(l0_norm_mlp_post)
    q0, k0, v0 = A_IN, A_IN + B_WIDTH, A_IN + 2 * B_WIDTH

    proj = _project(xs, g_pre, sc_m, sh_m, w_in, tm=1024, tn=512, rope=rope,
                    rope_cols=[(k0, v0)], scaled_rope_cols=[(q0, k0)])
    pc = _project(cs, g_pre, csc_m, csh_m, w_in, tm=nc, tn=512, scaled_cols=[(q0, k0)])
    lam_init = 0.8 - 0.6 * math.exp(-0.3 * 0)
    lam_rows = jnp.zeros((8, LANES), F32)
    for r, lv in enumerate((l0_lambda_q1, l0_lambda_k1, l0_lambda_q2, l0_lambda_k2)):
        lam_rows = lam_rows.at[r, :HEAD_DIM].set(lv)

    a_lat = _conformer_conv(proj, l0_conv_w, l0_conv_b, l0_ln_g, l0_ln_b, t=256)
    a_ctx = _conformer_conv(pc, l0_conv_w, l0_conv_b, l0_ln_g, l0_ln_b, t=nc)
    k_c, v_c = pc[:, k0:v0], pc[:, v0:]
    k_all = jnp.concatenate([k_c, proj[:, k0:v0]], axis=0)
    v_all = jnp.concatenate([v_c, proj[:, v0:]], axis=0)
    b_lat = _diff_attention(lam_rows, l0_subln_g, proj, q0, k_all, v_all, tq=1024, tk=1280, lam_init=lam_init)
    b_ctx = _diff_attention(lam_rows, l0_subln_g, pc, q0, k_c, v_c, tq=nc, tk=nc, lam_init=lam_init)

    x1, hf = _out_project(a_lat, b_lat, xs, w_out, g_post, gt_m, gf_pre, sc_f, sh_f, tm=256)
    xs = _mlp(hf, w1, w2, x1, gf_post, gt_f, tm=512, tf=512)
    c1, hcf = _out_project(a_ctx, b_ctx, cs, w_out, g_post, cgt_m, gf_pre, csc_f, csh_f, tm=nc)
    cs = _mlp(hcf, w1, w2, c1, gf_post, cgt_f, tm=nc, tf=512)

    mod = _modulation(cvecs, l1_mod_w, l1_mod_b)
    sh_m, sc_m, gt_m, sh_f, sc_f, gt_f = _split_mod(mod[0])
    csh_m, csc_m = _split_mod(mod[1])[:2]
    w_in = l1_w_in.astype(BF16)
    w_out = l1_w_out.astype(BF16)
    w1 = l1_mlp_w1.astype(BF16)
    w2 = l1_mlp_w2.astype(BF16)
    g_pre, g_post = _row(l1_norm_mix_pre), _row(l1_norm_mix_post)
    gf_pre, gf_post = _row(l1_norm_mlp_pre), _row(l1_norm_mlp_post)
    q0, k0, v0 = C_IN, C_IN + D_Q_WIDTH, C_IN + D_Q_WIDTH + D_KV_WIDTH

    proj = _project(xs, g_pre, sc_m, sh_m, w_in, tm=1024, tn=512, rope=rope,
                    rope_cols=[(k0, v0)], scaled_rope_cols=[(q0, k0)])
    pc = _project(cs, g_pre, csc_m, csh_m, w_in[:, k0:], tm=nc, tn=512)
    c_lat = _short_conv(proj, l1_sconv_w, t=512)
    d_lat = _window_attention(l1_sink, proj, q0, _dup_heads(proj[:, k0:v0]), _dup_heads(proj[:, v0:]),
                              _dup_heads(pc[:, :D_KV_WIDTH]), _dup_heads(pc[:, D_KV_WIDTH:]), tq=512)
    x1, hf = _out_project(c_lat, d_lat, xs, w_out, g_post, gt_m, gf_pre, sc_f, sh_f, tm=256)
    xs = _mlp(hf, w1, w2, x1, gf_post, gt_f, tm=512, tf=512)
    return xs[None]
```

```python
import functools
import math

import jax
import jax.numpy as jnp
from jax import lax
from jax.experimental import pallas as pl
from jax.experimental.pallas import tpu as pltpu

F32 = jnp.float32
BF16 = jnp.bfloat16

D_MODEL = 2048
HEAD_DIM = 64
HALF = HEAD_DIM // 2
GRID_W = 64
ROPE_BASE = 10000.0
NORM_EPS = 1e-6
LN_EPS = 1e-5
ATTN_SCALE = HEAD_DIM ** -0.5
LOG2E = math.log2(math.e)
NEG_INF = -1e30
N_MOD = 6
WINDOW = 128
LANES = 128
HALO = 16

A_WIDTH = 1024
A_IN = 2 * A_WIDTH
CONV_A_WIDTH = 31
B_HEADS = 8
B_WIDTH = 1024
C_WIDTH = 1024
C_IN = 3 * C_WIDTH
D_HEADS = 16
D_KV_HEADS = 4
D_GROUP = 4
D_Q_WIDTH = 1024
D_KV_WIDTH = 256
D_FF = 4 * D_MODEL

VMEM_LIMIT = 56 * 1024 * 1024


def _params(sem):
    return pltpu.CompilerParams(dimension_semantics=sem, vmem_limit_bytes=VMEM_LIMIT)


def _rms(x, eps=NORM_EPS):
    return x * lax.rsqrt(jnp.mean(x * x, axis=-1, keepdims=True) + eps)


def _dot_nt(a, b):
    return lax.dot_general(a, b, (((1,), (1,)), ((), ())), preferred_element_type=F32)


def _mod_kernel(c_ref, w_ref, b_ref, o_ref):
    c = c_ref[...]
    s = c * jax.nn.sigmoid(c)
    o_ref[...] = jnp.dot(s, w_ref[...], preferred_element_type=F32) + b_ref[...]


def _modulation(cvecs, w, b):
    tn = 1024
    n_out = w.shape[1]
    return pl.pallas_call(
        _mod_kernel,
        grid=(n_out // tn,),
        in_specs=[pl.BlockSpec((8, D_MODEL), lambda j: (0, 0)),
                  pl.BlockSpec((D_MODEL, tn), lambda j: (0, j)),
                  pl.BlockSpec((1, tn), lambda j: (0, j))],
        out_specs=pl.BlockSpec((8, tn), lambda j: (0, j)),
        out_shape=jax.ShapeDtypeStruct((8, n_out), F32),
        compiler_params=_params(("arbitrary",)),
        name="modulation",
    )(cvecs, w, b.reshape(1, n_out))


MODE_PLAIN, MODE_ROPE, MODE_ROPE_SCALE, MODE_SCALE = range(4)


def _rope_chunk(x, cosf, sinf):
    lane = lax.broadcasted_iota(jnp.int32, x.shape, 1)
    swapped = jnp.where((lane & (HEAD_DIM - 1)) < HALF,
                        pltpu.roll(x, LANES - HALF, 1), pltpu.roll(x, HALF, 1))
    return x * cosf + swapped * sinf


def _proj_kernel(*refs, tn, modes, use_rope, q_scale):
    if use_rope:
        x_ref, g_ref, sc_ref, sh_ref, cos_ref, sin_ref, w_ref, o_ref, h_scr = refs
    else:
        x_ref, g_ref, sc_ref, sh_ref, w_ref, o_ref, h_scr = refs
    j = pl.program_id(1)

    @pl.when(j == 0)
    def _():
        y = _rms(x_ref[...]) * g_ref[...]
        h_scr[...] = (y * (1.0 + sc_ref[...]) + sh_ref[...]).astype(BF16)

    acc = jnp.dot(h_scr[...], w_ref[...], preferred_element_type=F32)
    nchunk = tn // LANES
    patterns = {}
    for jj in range(len(modes) // nchunk):
        patterns.setdefault(tuple(modes[jj * nchunk:(jj + 1) * nchunk]), []).append(jj)

    def epilogue(pat):
        if all(m == MODE_PLAIN for m in pat):
            o_ref[...] = acc.astype(BF16)
            return
        for c, m in enumerate(pat):
            chunk = acc[:, c * LANES:(c + 1) * LANES]
            if m in (MODE_ROPE, MODE_ROPE_SCALE):
                chunk = _rope_chunk(chunk, cos_ref[...], sin_ref[...])
            if m in (MODE_ROPE_SCALE, MODE_SCALE):
                chunk = chunk * q_scale
            o_ref[:, c * LANES:(c + 1) * LANES] = chunk.astype(BF16)

    if len(patterns) == 1:
        epilogue(next(iter(patterns)))
    else:
        for pat, js in patterns.items():
            cond = functools.reduce(jnp.logical_or, [j == jj for jj in js])
            pl.when(cond)(functools.partial(epilogue, pat))


def _project(x, g, scale, shift, w, *, tm, tn, rope=None, rope_cols=(), scaled_rope_cols=(), scaled_cols=(),
             q_scale=ATTN_SCALE):
    n, d = x.shape
    p = w.shape[1]
    modes = [MODE_PLAIN] * (p // LANES)
    for cols, mode in ((rope_cols, MODE_ROPE), (scaled_rope_cols, MODE_ROPE_SCALE), (scaled_cols, MODE_SCALE)):
        for lo, hi in cols:
            for c in range(lo // LANES, hi // LANES):
                modes[c] = mode
    use_rope = rope is not None
    vec = pl.BlockSpec((1, d), lambda i, j: (0, 0))
    in_specs = [pl.BlockSpec((tm, d), lambda i, j: (i, 0)), vec, vec, vec]
    args = [x, g, scale, shift]
    if use_rope:
        tab = pl.BlockSpec((tm, LANES), lambda i, j: (i, 0))
        in_specs += [tab, tab]
        args += list(rope)
    in_specs.append(pl.BlockSpec((d, tn), lambda i, j: (0, j)))
    args.append(w)
    return pl.pallas_call(
        functools.partial(_proj_kernel, tn=tn, modes=tuple(modes), use_rope=use_rope, q_scale=q_scale),
        grid=(n // tm, p // tn),
        in_specs=in_specs,
        out_specs=pl.BlockSpec((tm, tn), lambda i, j: (i, j)),
        out_shape=jax.ShapeDtypeStruct((n, p), BF16),
        scratch_shapes=[pltpu.VMEM((tm, d), BF16)],
        compiler_params=_params(("parallel", "arbitrary")),
        name="norm_project",
    )(*args)


CONV_A_ROWS = 32


def _conv_a_kernel(main_ref, left_ref, right_ref, w_ref, cb_ref, lg_ref, lb_ref, o_ref, ext_scr, sh_scr, *, t):
    i = pl.program_id(0)
    last = pl.num_programs(0) - 1

    def glu(ref):
        v = ref[...].astype(F32)
        return v[:, :A_WIDTH] * jax.nn.sigmoid(v[:, A_WIDTH:])

    ext_scr[0:HALO, :] = jnp.where(i > 0, glu(left_ref), 0.0)
    ext_scr[HALO:HALO + t, :] = glu(main_ref)
    ext_scr[HALO + t:, :] = jnp.where(i < last, glu(right_ref), 0.0)
    ext = ext_scr[...]
    rows = t + 2 * HALO
    srows = t + 3 * 8
    sh_scr[0] = ext[:srows]
    for b in range(1, 8):
        sh_scr[b] = pltpu.roll(ext, rows - b, 0)[:srows]

    def chunk(c, carry):
        r0 = pl.multiple_of(c * CONV_A_ROWS, CONV_A_ROWS)
        acc = jnp.zeros((CONV_A_ROWS, A_WIDTH), F32)
        for k in range(CONV_A_WIDTH):
            a, b = divmod(k + 1, 8)
            acc = acc + sh_scr[b, pl.ds(r0 + 8 * a, CONV_A_ROWS), :] * w_ref[k:k + 1, :]
        v = acc + cb_ref[...]
        mu = jnp.mean(v, axis=-1, keepdims=True)
        vc = v - mu
        var = jnp.mean(vc * vc, axis=-1, keepdims=True)
        y = vc * lax.rsqrt(var + LN_EPS) * lg_ref[...] + lb_ref[...]
        o_ref[pl.ds(r0, CONV_A_ROWS), :] = (y * jax.nn.sigmoid(y)).astype(BF16)
        return carry

    lax.fori_loop(0, t // CONV_A_ROWS, chunk, 0)


def _conformer_conv(proj, conv_w, conv_b, ln_g, ln_b, *, t):
    n = proj.shape[0]
    per = t // HALO
    nh = n // HALO
    vec = pl.BlockSpec((1, A_WIDTH), lambda i: (0, 0))
    return pl.pallas_call(
        functools.partial(_conv_a_kernel, t=t),
        grid=(n // t,),
        in_specs=[pl.BlockSpec((t, A_IN), lambda i: (i, 0)),
                  pl.BlockSpec((HALO, A_IN), lambda i: (jnp.maximum(i * per - 1, 0), 0)),
                  pl.BlockSpec((HALO, A_IN), lambda i: (jnp.minimum((i + 1) * per, nh - 1), 0)),
                  pl.BlockSpec((CONV_A_WIDTH, A_WIDTH), lambda i: (0, 0)),
                  vec, vec, vec],
        out_specs=pl.BlockSpec((t, A_WIDTH), lambda i: (i, 0)),
        out_shape=jax.ShapeDtypeStruct((n, A_WIDTH), BF16),
        scratch_shapes=[pltpu.VMEM((t + 2 * HALO, A_WIDTH), F32),
                        pltpu.VMEM((8, t + 24, A_WIDTH), F32)],
        compiler_params=_params(("parallel",)),
        name="conformer_conv",
    )(proj, proj, proj, conv_w, conv_b.reshape(1, -1), ln_g.reshape(1, -1), ln_b.reshape(1, -1))


def _tile_lanes(x, reps):
    return x if reps == 1 else jnp.concatenate([x] * reps, axis=1)


def _diff_attn_kernel(lam_ref, sub_ref, q_ref, k_ref, v_ref, o_ref, qm_scr, m_scr, acc_scr, *, lam_init, tk):
    q = q_ref[...]
    lane = lax.broadcasted_iota(jnp.int32, q.shape, 1)
    zero = jnp.zeros_like(q)
    qm_scr[0] = jnp.where(lane < HEAD_DIM, q, zero)
    qm_scr[1] = jnp.where(lane >= HEAD_DIM, q, zero)
    m_scr[...] = jnp.full(m_scr.shape, -jnp.inf, F32)
    acc_scr[...] = jnp.zeros(acc_scr.shape, F32)
    ones = jnp.ones((tk, LANES), BF16)

    def chunk(c, carry):
        r0 = pl.multiple_of(c * tk, tk)
        k = k_ref[pl.ds(r0, tk), :]
        v_ext = jnp.concatenate([v_ref[pl.ds(r0, tk), :], ones], axis=1)
        scores = [_dot_nt(qm_scr[m], k) for m in range(2)]
        for m in range(2):
            s = scores[m]
            m_prev = m_scr[m]
            m_new = jnp.maximum(m_prev, jnp.max(s, axis=1, keepdims=True))
            alpha = jnp.exp2(m_prev - m_new)
            p = jnp.exp2(s - _tile_lanes(m_new, tk // LANES))
            acc_scr[m] = (_tile_lanes(alpha, 2) * acc_scr[m]
                          + jnp.dot(p.astype(BF16), v_ext, preferred_element_type=F32))
            m_scr[m] = m_new
        return carry

    lax.fori_loop(0, k_ref.shape[0] // tk, chunk, 0)

    lp = lam_ref[...]
    lam = (jnp.exp(jnp.sum(lp[0:1] * lp[1:2], axis=1, keepdims=True))
           - jnp.exp(jnp.sum(lp[2:3] * lp[3:4], axis=1, keepdims=True)) + lam_init)
    o = (acc_scr[0, :, :LANES] / acc_scr[0, :, LANES:]
         - lam * (acc_scr[1, :, :LANES] / acc_scr[1, :, LANES:]))
    o = _rms(o) * sub_ref[...] * (1.0 - lam_init)
    o_ref[...] = o.astype(BF16)


def _diff_attention(lam_rows, subln_g, q_src, q_col0, k_all, v_all, *, tq, tk, lam_init):
    n = q_src.shape[0]
    nk = k_all.shape[0]
    qb = q_col0 // LANES
    return pl.pallas_call(
        functools.partial(_diff_attn_kernel, lam_init=lam_init, tk=tk),
        grid=(B_HEADS, n // tq),
        in_specs=[pl.BlockSpec((8, LANES), lambda h, i: (0, 0)),
                  pl.BlockSpec((1, LANES), lambda h, i: (0, 0)),
                  pl.BlockSpec((tq, LANES), lambda h, i: (i, qb + h)),
                  pl.BlockSpec((nk, LANES), lambda h, i: (0, h)),
                  pl.BlockSpec((nk, LANES), lambda h, i: (0, h))],
        out_specs=pl.BlockSpec((tq, LANES), lambda h, i: (i, h)),
        out_shape=jax.ShapeDtypeStruct((n, B_WIDTH), BF16),
        scratch_shapes=[pltpu.VMEM((2, tq, LANES), BF16),
                        pltpu.VMEM((2, tq, LANES), F32),
                        pltpu.VMEM((2, tq, 2 * LANES), F32)],
        compiler_params=_params(("parallel", "arbitrary")),
        name="diff_attention",
    )(lam_rows, subln_g.reshape(1, LANES), q_src, k_all, v_all)


def _conv_c_kernel(main_ref, left_ref, right_ref, w_ref, o_ref, ext_scr, *, t):
    i = pl.program_id(0)
    last = pl.num_programs(0) - 1

    def gated(ref):
        v = ref[...].astype(F32)
        return v[:, C_WIDTH:2 * C_WIDTH] * v[:, 2 * C_WIDTH:]

    ext_scr[0:HALO, :] = jnp.where(i > 0, gated(left_ref), 0.0)
    ext_scr[HALO:HALO + t, :] = gated(main_ref)
    ext_scr[HALO + t:, :] = jnp.where(i < last, gated(right_ref), 0.0)
    ext = ext_scr[...]
    rows = t + 2 * HALO
    prev = pltpu.roll(ext, 1, 0)[HALO:HALO + t]
    nxt = pltpu.roll(ext, rows - 1, 0)[HALO:HALO + t]
    conv = prev * w_ref[0:1, :] + ext[HALO:HALO + t] * w_ref[1:2, :] + nxt * w_ref[2:3, :]
    o_ref[...] = (main_ref[:, :C_WIDTH].astype(F32) * conv).astype(BF16)


def _short_conv(proj, w, *, t):
    n = proj.shape[0]
    per = t // HALO
    nh = n // HALO
    return pl.pallas_call(
        functools.partial(_conv_c_kernel, t=t),
        grid=(n // t,),
        in_specs=[pl.BlockSpec((t, C_IN), lambda i: (i, 0)),
                  pl.BlockSpec((HALO, C_IN), lambda i: (jnp.maximum(i * per - 1, 0), 0)),
                  pl.BlockSpec((HALO, C_IN), lambda i: (jnp.minimum((i + 1) * per, nh - 1), 0)),
                  pl.BlockSpec((3, C_WIDTH), lambda i: (0, 0))],
        out_specs=pl.BlockSpec((t, C_WIDTH), lambda i: (i, 0)),
        out_shape=jax.ShapeDtypeStruct((n, C_WIDTH), BF16),
        scratch_shapes=[pltpu.VMEM((t + 2 * HALO, C_WIDTH), F32)],
        compiler_params=_params(("parallel",)),
        name="short_conv",
    )(proj, proj, proj, w)


def _win_attn_kernel(sink_ref, q_ref, kp_ref, km_ref, kn_ref, vp_ref, vm_ref, vn_ref, kc_ref, vc_ref, o_ref,
                     *, tq, n):
    i = pl.program_id(0)
    kv = pl.program_id(1)
    kw = jnp.concatenate([kp_ref[...], km_ref[...], kn_ref[...]], axis=0)
    vw = jnp.concatenate([vp_ref[...], vm_ref[...], vn_ref[...]], axis=0)
    kc = kc_ref[...]
    vc = vc_ref[...]
    wk = tq + 2 * WINDOW
    row = lax.broadcasted_iota(jnp.int32, (tq, wk), 0)
    col = lax.broadcasted_iota(jnp.int32, (tq, wk), 1)
    pos = i * tq - WINDOW + col
    mask = (col >= row) & (col <= row + 2 * WINDOW) & (pos >= 0) & (pos < n)
    lane = lax.broadcasted_iota(jnp.int32, (tq, LANES), 1)
    qms = []
    for g in range(D_GROUP):
        qa = q_ref[:, (g // 2) * LANES:(g // 2 + 1) * LANES]
        sel = (lane < HEAD_DIM) if g % 2 == 0 else (lane >= HEAD_DIM)
        qms.append(jnp.where(sel, qa, jnp.zeros_like(qa)))
    s_cs = [_dot_nt(qm, kc) for qm in qms]
    s_ws = [_dot_nt(qm, kw) for qm in qms]
    probs = []
    for g in range(D_GROUP):
        s_c = s_cs[g]
        s_w = jnp.where(mask, s_ws[g], NEG_INF)
        sk = sink_ref[kv * D_GROUP + g]
        mx = jnp.maximum(jnp.maximum(jnp.max(s_c, axis=1, keepdims=True),
                                     jnp.max(s_w, axis=1, keepdims=True)), sk)
        p_c = jnp.exp(s_c - mx)
        p_w = jnp.exp(s_w - mx)
        den = (jnp.sum(p_c, axis=1, keepdims=True) + jnp.sum(p_w, axis=1, keepdims=True)
               + jnp.exp(sk - mx))
        probs.append((p_c.astype(BF16), p_w.astype(BF16), den))
    outs = [(jnp.dot(p_c, vc, preferred_element_type=F32) + jnp.dot(p_w, vw, preferred_element_type=F32)) / den
            for p_c, p_w, den in probs]
    for a in range(D_GROUP // 2):
        o_ref[:, a * LANES:(a + 1) * LANES] = jnp.where(lane < HEAD_DIM, outs[2 * a], outs[2 * a + 1]).astype(BF16)


def _window_attention(sink, proj, q_col0, k_dup, v_dup, kc_dup, vc_dup, *, tq):
    n = proj.shape[0]
    nc = kc_dup.shape[0]
    qb = q_col0 // (2 * LANES)
    per = tq // WINDOW
    nw = n // WINDOW
    prev = lambda i, kv: (jnp.maximum(i * per - 1, 0), kv)
    main = lambda i, kv: (i, kv)
    nxt = lambda i, kv: (jnp.minimum((i + 1) * per, nw - 1), kv)
    halo = lambda f: pl.BlockSpec((WINDOW, LANES), f)
    return pl.pallas_call(
        functools.partial(_win_attn_kernel, tq=tq, n=n),
        grid=(n // tq, D_KV_HEADS),
        in_specs=[pl.BlockSpec(memory_space=pltpu.SMEM),
                  pl.BlockSpec((tq, 2 * LANES), lambda i, kv: (i, qb + kv)),
                  halo(prev), pl.BlockSpec((tq, LANES), main), halo(nxt),
                  halo(prev), pl.BlockSpec((tq, LANES), main), halo(nxt),
                  pl.BlockSpec((nc, LANES), lambda i, kv: (0, kv)),
                  pl.BlockSpec((nc, LANES), lambda i, kv: (0, kv))],
        out_specs=pl.BlockSpec((tq, 2 * LANES), lambda i, kv: (i, kv)),
        out_shape=jax.ShapeDtypeStruct((n, D_Q_WIDTH), BF16),
        compiler_params=_params(("parallel", "parallel")),
        name="window_attention",
    )(sink, proj, k_dup, k_dup, k_dup, v_dup, v_dup, v_dup, kc_dup, vc_dup)


def _dup_heads(x):
    n = x.shape[0]
    x = x.reshape(n, D_KV_HEADS, 1, HEAD_DIM)
    return jnp.broadcast_to(x, (n, D_KV_HEADS, 2, HEAD_DIM)).reshape(n, D_KV_HEADS * LANES)


def _out_kernel(a_ref, b_ref, x_ref, w_ref, gpost_ref, gate_ref, gpre_ref, sc_ref, sh_ref, x1_ref, hf_ref):
    half = a_ref.shape[1]
    y = (jnp.dot(a_ref[...], w_ref[:half, :], preferred_element_type=F32)
         + jnp.dot(b_ref[...], w_ref[half:, :], preferred_element_type=F32))
    x1 = x_ref[...] + gate_ref[...] * (_rms(y) * gpost_ref[...])
    x1_ref[...] = x1
    hf_ref[...] = ((_rms(x1) * gpre_ref[...]) * (1.0 + sc_ref[...]) + sh_ref[...]).astype(BF16)


def _out_project(a, b, x, w, g_post, gate, g_pre, scale, shift, *, tm):
    n, d = x.shape
    half = a.shape[1]
    vec = pl.BlockSpec((1, d), lambda i: (0, 0))
    return pl.pallas_call(
        _out_kernel,
        grid=(n // tm,),
        in_specs=[pl.BlockSpec((tm, half), lambda i: (i, 0)),
                  pl.BlockSpec((tm, half), lambda i: (i, 0)),
                  pl.BlockSpec((tm, d), lambda i: (i, 0)),
                  pl.BlockSpec((2 * half, d), lambda i: (0, 0)),
                  vec, vec, vec, vec, vec],
        out_specs=[pl.BlockSpec((tm, d), lambda i: (i, 0)), pl.BlockSpec((tm, d), lambda i: (i, 0))],
        out_shape=[jax.ShapeDtypeStruct((n, d), F32), jax.ShapeDtypeStruct((n, d), BF16)],
        compiler_params=_params(("parallel",)),
        name="out_project",
    )(a, b, x, w, g_post, gate, g_pre, scale, shift)


def _mlp_kernel(hf_ref, w1_ref, w2_ref, x1_ref, gpost_ref, gate_ref, o_ref, acc_scr):
    f = pl.program_id(1)

    @pl.when(f == 0)
    def _():
        acc_scr[...] = jnp.zeros(acc_scr.shape, F32)

    h = jnp.dot(hf_ref[...], w1_ref[...], preferred_element_type=F32)
    h = jnp.square(jnp.maximum(h, 0.0)).astype(BF16)
    acc_scr[...] += jnp.dot(h, w2_ref[...], preferred_element_type=F32)

    @pl.when(f == pl.num_programs(1) - 1)
    def _():
        o_ref[...] = x1_ref[...] + gate_ref[...] * (_rms(acc_scr[...]) * gpost_ref[...])


def _mlp(hf, w1, w2, x1, g_post, gate, *, tm, tf):
    n, d = x1.shape
    ff = w1.shape[1]
    vec = pl.BlockSpec((1, d), lambda i, f: (0, 0))
    return pl.pallas_call(
        _mlp_kernel,
        grid=(n // tm, ff // tf),
        in_specs=[pl.BlockSpec((tm, d), lambda i, f: (i, 0)),
                  pl.BlockSpec((d, tf), lambda i, f: (0, f)),
                  pl.BlockSpec((tf, d), lambda i, f: (f, 0)),
                  pl.BlockSpec((tm, d), lambda i, f: (i, 0)),
                  vec, vec],
        out_specs=pl.BlockSpec((tm, d), lambda i, f: (i, 0)),
        out_shape=jax.ShapeDtypeStruct((n, d), F32),
        scratch_shapes=[pltpu.VMEM((tm, d), F32)],
        compiler_params=_params(("parallel", "arbitrary")),
        name="mlp",
    )(hf, w1, w2, x1, g_post, gate)


def _rope_tables(n):
    rows = n // GRID_W
    row = jnp.repeat(jnp.arange(rows, dtype=F32), GRID_W)
    col = jnp.tile(jnp.arange(GRID_W, dtype=F32), rows)
    pairs = HEAD_DIM // 4
    inv = jnp.power(ROPE_BASE, -jnp.arange(pairs, dtype=F32) / pairs)
    ang = jnp.concatenate([row[:, None] * inv, col[:, None] * inv], axis=-1)
    cos, sin = jnp.cos(ang), jnp.sin(ang)
    return jnp.tile(cos, (1, 4)), jnp.tile(jnp.concatenate([-sin, sin], axis=-1), (1, 2))


def _row(v):
    return v.reshape(1, -1)


def _split_mod(m):
    return [_row(t) for t in jnp.split(m, N_MOD)]


def kernel(x, c, ctx, c_ctx, l0_mod_w, l0_mod_b, l0_norm_mix_pre, l0_norm_mix_post, l0_norm_mlp_pre, l0_norm_mlp_post, l0_w_in, l0_conv_w, l0_conv_b, l0_ln_g, l0_ln_b, l0_lambda_q1, l0_lambda_k1, l0_lambda_q2, l0_lambda_k2, l0_subln_g, l0_w_out, l0_mlp_w1, l0_mlp_w2, l1_mod_w, l1_mod_b, l1_norm_mix_pre, l1_norm_mix_post, l1_norm_mlp_pre, l1_norm_mlp_post, l1_w_in, l1_sconv_w, l1_sink, l1_w_out, l1_mlp_w1, l1_mlp_w2):
    n = x.shape[1]
    nc = ctx.shape[1]
    xs = x[0]
    cs = ctx[0]
    rope = _rope_tables(n)
    cvecs = jnp.zeros((8, D_MODEL), F32).at[0].set(c[0]).at[1].set(c_ctx)

    mod = _modulation(cvecs, l0_mod_w, l0_mod_b)
    sh_m, sc_m, gt_m, sh_f, sc_f, gt_f = _split_mod(mod[0])
    csh_m, csc_m, cgt_m, csh_f, csc_f, cgt_f = _split_mod(mod[1])
    w_in = l0_w_in.astype(BF16)
    w_out = l0_w_out.astype(BF16)
    w1 = l0_mlp_w1.astype(BF16)
    w2 = l0_mlp_w2.astype(BF16)
    g_pre, g_post = _row(l0_norm_mix_pre), _row(l0_norm_mix_post)
    gf_pre, gf_post = _row(l0_norm_mlp_pre), _row(l0_norm_mlp_post)
    q0, k0, v0 = A_IN, A_IN + B_WIDTH, A_IN + 2 * B_WIDTH

    proj = _project(xs, g_pre, sc_m, sh_m, w_in, tm=1024, tn=512, rope=rope,
                    rope_cols=[(k0, v0)], scaled_rope_cols=[(q0, k0)], q_scale=ATTN_SCALE * LOG2E)
    pc = _project(cs, g_pre, csc_m, csh_m, w_in, tm=nc, tn=512, scaled_cols=[(q0, k0)],
                  q_scale=ATTN_SCALE * LOG2E)
    lam_init = 0.8 - 0.6 * math.exp(-0.3 * 0)
    lam_rows = jnp.zeros((8, LANES), F32)
    for r, lv in enumerate((l0_lambda_q1, l0_lambda_k1, l0_lambda_q2, l0_lambda_k2)):
        lam_rows = lam_rows.at[r, :HEAD_DIM].set(lv)

    a_lat = _conformer_conv(proj, l0_conv_w, l0_conv_b, l0_ln_g, l0_ln_b, t=256)
    a_ctx = _conformer_conv(pc, l0_conv_w, l0_conv_b, l0_ln_g, l0_ln_b, t=nc)
    k_c, v_c = pc[:, k0:v0], pc[:, v0:]
    k_all = jnp.concatenate([k_c, proj[:, k0:v0]], axis=0)
    v_all = jnp.concatenate([v_c, proj[:, v0:]], axis=0)
    b_lat = _diff_attention(lam_rows, l0_subln_g, proj, q0, k_all, v_all, tq=1024, tk=1280, lam_init=lam_init)
    b_ctx = _diff_attention(lam_rows, l0_subln_g, pc, q0, k_c, v_c, tq=nc, tk=nc, lam_init=lam_init)

    x1, hf = _out_project(a_lat, b_lat, xs, w_out, g_post, gt_m, gf_pre, sc_f, sh_f, tm=256)
    xs = _mlp(hf, w1, w2, x1, gf_post, gt_f, tm=512, tf=1024)
    c1, hcf = _out_project(a_ctx, b_ctx, cs, w_out, g_post, cgt_m, gf_pre, csc_f, csh_f, tm=nc)
    cs = _mlp(hcf, w1, w2, c1, gf_post, cgt_f, tm=nc, tf=512)

    mod = _modulation(cvecs, l1_mod_w, l1_mod_b)
    sh_m, sc_m, gt_m, sh_f, sc_f, gt_f = _split_mod(mod[0])
    csh_m, csc_m = _split_mod(mod[1])[:2]
    w_in = l1_w_in.astype(BF16)
    w_out = l1_w_out.astype(BF16)
    w1 = l1_mlp_w1.astype(BF16)
    w2 = l1_mlp_w2.astype(BF16)
    g_pre, g_post = _row(l1_norm_mix_pre), _row(l1_norm_mix_post)
    gf_pre, gf_post = _row(l1_norm_mlp_pre), _row(l1_norm_mlp_post)
    q0, k0, v0 = C_IN, C_IN + D_Q_WIDTH, C_IN + D_Q_WIDTH + D_KV_WIDTH

    proj = _project(xs, g_pre, sc_m, sh_m, w_in, tm=1024, tn=512, rope=rope,
                    rope_cols=[(k0, v0)], scaled_rope_cols=[(q0, k0)])
    pc = _project(cs, g_pre, csc_m, csh_m, w_in[:, k0:], tm=nc, tn=512)
    c_lat = _short_conv(proj, l1_sconv_w, t=512)
    d_lat = _window_attention(l1_sink, proj, q0, _dup_heads(proj[:, k0:v0]), _dup_heads(proj[:, v0:]),
                              _dup_heads(pc[:, :D_KV_WIDTH]), _dup_heads(pc[:, D_KV_WIDTH:]), tq=256)
    x1, hf = _out_project(c_lat, d_lat, xs, w_out, g_post, gt_m, gf_pre, sc_f, sh_f, tm=256)
    xs = _mlp(hf, w1, w2, x1, gf_post, gt_f, tm=512, tf=1024)
    return xs[None]
```

```python
import functools
import math

import jax
import jax.numpy as jnp
from jax import lax
from jax.experimental import pallas as pl
from jax.experimental.pallas import tpu as pltpu

F32 = jnp.float32
BF16 = jnp.bfloat16

D_MODEL = 2048
HEAD_DIM = 64
HALF = HEAD_DIM // 2
GRID_W = 64
ROPE_BASE = 10000.0
NORM_EPS = 1e-6
LN_EPS = 1e-5
ATTN_SCALE = HEAD_DIM ** -0.5
LOG2E = math.log2(math.e)
NEG_INF = -1e30
N_MOD = 6
WINDOW = 128
LANES = 128
HALO = 16

A_WIDTH = 1024
A_IN = 2 * A_WIDTH
CONV_A_WIDTH = 31
B_HEADS = 8
B_WIDTH = 1024
C_WIDTH = 1024
C_IN = 3 * C_WIDTH
D_HEADS = 16
D_KV_HEADS = 4
D_GROUP = 4
D_Q_WIDTH = 1024
D_KV_WIDTH = 256
D_FF = 4 * D_MODEL

VMEM_LIMIT = 56 * 1024 * 1024


def _params(sem):
    return pltpu.CompilerParams(dimension_semantics=sem, vmem_limit_bytes=VMEM_LIMIT)


def _rms(x, eps=NORM_EPS):
    return x * lax.rsqrt(jnp.mean(x * x, axis=-1, keepdims=True) + eps)


def _dot_nt(a, b):
    return lax.dot_general(a, b, (((1,), (1,)), ((), ())), preferred_element_type=F32)


def _mod_kernel(c_ref, w_ref, b_ref, o_ref):
    c = c_ref[...]
    s = c * jax.nn.sigmoid(c)
    o_ref[...] = jnp.dot(s, w_ref[...], preferred_element_type=F32) + b_ref[...]


def _modulation(cvecs, w, b):
    tn = 1024
    n_out = w.shape[1]
    return pl.pallas_call(
        _mod_kernel,
        grid=(n_out // tn,),
        in_specs=[pl.BlockSpec((8, D_MODEL), lambda j: (0, 0)),
                  pl.BlockSpec((D_MODEL, tn), lambda j: (0, j)),
                  pl.BlockSpec((1, tn), lambda j: (0, j))],
        out_specs=pl.BlockSpec((8, tn), lambda j: (0, j)),
        out_shape=jax.ShapeDtypeStruct((8, n_out), F32),
        compiler_params=_params(("arbitrary",)),
        name="modulation",
    )(cvecs, w, b.reshape(1, n_out))


PROJ_SUB = 512


def _proj_kernel(*refs, tn):
    nsub = tn // PROJ_SUB
    x_ref, g_ref, sc_ref, sh_ref = refs[:4]
    tabs = refs[4:4 + nsub]
    w_ref, o_ref, h_scr = refs[4 + nsub:]

    @pl.when(pl.program_id(1) == 0)
    def _():
        y = _rms(x_ref[...]) * g_ref[...]
        h_scr[...] = (y * (1.0 + sc_ref[...]) + sh_ref[...]).astype(BF16)

    acc = jnp.dot(h_scr[...], w_ref[...], preferred_element_type=F32)
    lane = lax.broadcasted_iota(jnp.int32, (acc.shape[0], LANES), 1)
    first_half = (lane & (HEAD_DIM - 1)) < HALF
    for s in range(nsub):
        a = tabs[s][0, 0]
        b = tabs[s][0, 1]
        for c in range(s * PROJ_SUB // LANES, (s + 1) * PROJ_SUB // LANES):
            y = acc[:, c * LANES:(c + 1) * LANES]
            swapped = jnp.where(first_half, pltpu.roll(y, LANES - HALF, 1), pltpu.roll(y, HALF, 1))
            o_ref[:, c * LANES:(c + 1) * LANES] = (y * a + swapped * b).astype(BF16)


def _kind_index(j, kinds):
    idx = jnp.int32(kinds[0])
    for jj in range(1, len(kinds)):
        if kinds[jj] != kinds[jj - 1]:
            idx = jnp.where(j >= jj, kinds[jj], idx)
    return idx


def _project(x, g, scale, shift, w, tables, kinds, *, tm, tn):
    n, d = x.shape
    p = w.shape[1]
    nsub = tn // PROJ_SUB
    assert len(kinds) == p // PROJ_SUB
    vec = pl.BlockSpec((1, d), lambda i, j: (0, 0))
    in_specs = [pl.BlockSpec((tm, d), lambda i, j: (i, 0)), vec, vec, vec]
    for s in range(nsub):
        in_specs.append(pl.BlockSpec(
            (1, 2, tm, LANES), lambda i, j, k=tuple(kinds[s::nsub]): (_kind_index(j, k), 0, i, 0)))
    in_specs.append(pl.BlockSpec((d, tn), lambda i, j: (0, j)))
    args = [x, g, scale, shift] + [tables] * nsub + [w]
    return pl.pallas_call(
        functools.partial(_proj_kernel, tn=tn),
        grid=(n // tm, p // tn),
        in_specs=in_specs,
        out_specs=pl.BlockSpec((tm, tn), lambda i, j: (i, j)),
        out_shape=jax.ShapeDtypeStruct((n, p), BF16),
        scratch_shapes=[pltpu.VMEM((tm, d), BF16)],
        compiler_params=_params(("parallel", "arbitrary")),
        name="norm_project",
    )(*args)


CONV_A_ROWS = 32


def _conv_a_kernel(main_ref, left_ref, right_ref, w_ref, cb_ref, lg_ref, lb_ref, o_ref, ext_scr, sh_scr, *, t):
    i = pl.program_id(0)
    last = pl.num_programs(0) - 1

    def glu(ref):
        v = ref[...].astype(F32)
        return v[:, :A_WIDTH] * jax.nn.sigmoid(v[:, A_WIDTH:])

    ext_scr[0:HALO, :] = jnp.where(i > 0, glu(left_ref), 0.0)
    ext_scr[HALO:HALO + t, :] = glu(main_ref)
    ext_scr[HALO + t:, :] = jnp.where(i < last, glu(right_ref), 0.0)
    ext = ext_scr[...]
    rows = t + 2 * HALO
    srows = t + 3 * 8
    sh_scr[0] = ext[:srows]
    for b in range(1, 8):
        sh_scr[b] = pltpu.roll(ext, rows - b, 0)[:srows]

    def chunk(c, carry):
        r0 = pl.multiple_of(c * CONV_A_ROWS, CONV_A_ROWS)
        acc = jnp.zeros((CONV_A_ROWS, A_WIDTH), F32)
        for k in range(CONV_A_WIDTH):
            a, b = divmod(k + 1, 8)
            acc = acc + sh_scr[b, pl.ds(r0 + 8 * a, CONV_A_ROWS), :] * w_ref[k:k + 1, :]
        v = acc + cb_ref[...]
        mu = jnp.mean(v, axis=-1, keepdims=True)
        vc = v - mu
        var = jnp.mean(vc * vc, axis=-1, keepdims=True)
        y = vc * lax.rsqrt(var + LN_EPS) * lg_ref[...] + lb_ref[...]
        o_ref[pl.ds(r0, CONV_A_ROWS), :] = (y * jax.nn.sigmoid(y)).astype(BF16)
        return carry

    lax.fori_loop(0, t // CONV_A_ROWS, chunk, 0, unroll=2)


def _conformer_conv(proj, conv_w, conv_b, ln_g, ln_b, *, t):
    n = proj.shape[0]
    per = t // HALO
    nh = n // HALO
    vec = pl.BlockSpec((1, A_WIDTH), lambda i: (0, 0))
    return pl.pallas_call(
        functools.partial(_conv_a_kernel, t=t),
        grid=(n // t,),
        in_specs=[pl.BlockSpec((t, A_IN), lambda i: (i, 0)),
                  pl.BlockSpec((HALO, A_IN), lambda i: (jnp.maximum(i * per - 1, 0), 0)),
                  pl.BlockSpec((HALO, A_IN), lambda i: (jnp.minimum((i + 1) * per, nh - 1), 0)),
                  pl.BlockSpec((CONV_A_WIDTH, A_WIDTH), lambda i: (0, 0)),
                  vec, vec, vec],
        out_specs=pl.BlockSpec((t, A_WIDTH), lambda i: (i, 0)),
        out_shape=jax.ShapeDtypeStruct((n, A_WIDTH), BF16),
        scratch_shapes=[pltpu.VMEM((t + 2 * HALO, A_WIDTH), F32),
                        pltpu.VMEM((8, t + 24, A_WIDTH), F32)],
        compiler_params=_params(("parallel",)),
        name="conformer_conv",
    )(proj, proj, proj, conv_w, conv_b.reshape(1, -1), ln_g.reshape(1, -1), ln_b.reshape(1, -1))


def _tile_lanes(x, reps):
    return x if reps == 1 else jnp.concatenate([x] * reps, axis=1)


def _diff_attn_kernel(lam_ref, sub_ref, q_ref, k_ref, v_ref, o_ref, qm_scr, m_scr, acc_scr, *, lam_init, tk):
    q = q_ref[...]
    lane = lax.broadcasted_iota(jnp.int32, q.shape, 1)
    zero = jnp.zeros_like(q)
    qm_scr[0] = jnp.where(lane < HEAD_DIM, q, zero)
    qm_scr[1] = jnp.where(lane >= HEAD_DIM, q, zero)
    m_scr[...] = jnp.full(m_scr.shape, -jnp.inf, F32)
    acc_scr[...] = jnp.zeros(acc_scr.shape, F32)
    ones = jnp.ones((tk, LANES), BF16)

    def chunk(c, carry):
        r0 = pl.multiple_of(c * tk, tk)
        k = k_ref[pl.ds(r0, tk), :]
        v_ext = jnp.concatenate([v_ref[pl.ds(r0, tk), :], ones], axis=1)
        scores = [_dot_nt(qm_scr[m], k) for m in range(2)]
        for m in range(2):
            s = scores[m]
            m_prev = m_scr[m]
            m_new = jnp.maximum(m_prev, jnp.max(s, axis=1, keepdims=True))
            alpha = jnp.exp2(m_prev - m_new)
            p = jnp.exp2(s - _tile_lanes(m_new, tk // LANES))
            acc_scr[m] = (_tile_lanes(alpha, 2) * acc_scr[m]
                          + jnp.dot(p.astype(BF16), v_ext, preferred_element_type=F32))
            m_scr[m] = m_new
        return carry

    lax.fori_loop(0, k_ref.shape[0] // tk, chunk, 0)

    lp = lam_ref[...]
    lam = (jnp.exp(jnp.sum(lp[0:1] * lp[1:2], axis=1, keepdims=True))
           - jnp.exp(jnp.sum(lp[2:3] * lp[3:4], axis=1, keepdims=True)) + lam_init)
    o = (acc_scr[0, :, :LANES] / acc_scr[0, :, LANES:]
         - lam * (acc_scr[1, :, :LANES] / acc_scr[1, :, LANES:]))
    o = _rms(o) * sub_ref[...] * (1.0 - lam_init)
    o_ref[...] = o.astype(BF16)


def _diff_attention(lam_rows, subln_g, q_src, q_col0, k_all, v_all, *, tq, tk, lam_init):
    n = q_src.shape[0]
    nk = k_all.shape[0]
    qb = q_col0 // LANES
    return pl.pallas_call(
        functools.partial(_diff_attn_kernel, lam_init=lam_init, tk=tk),
        grid=(B_HEADS, n // tq),
        in_specs=[pl.BlockSpec((8, LANES), lambda h, i: (0, 0)),
                  pl.BlockSpec((1, LANES), lambda h, i: (0, 0)),
                  pl.BlockSpec((tq, LANES), lambda h, i: (i, qb + h)),
                  pl.BlockSpec((nk, LANES), lambda h, i: (0, h), pipeline_mode=pl.Buffered(1)),
                  pl.BlockSpec((nk, LANES), lambda h, i: (0, h), pipeline_mode=pl.Buffered(1))],
        out_specs=pl.BlockSpec((tq, LANES), lambda h, i: (i, h)),
        out_shape=jax.ShapeDtypeStruct((n, B_WIDTH), BF16),
        scratch_shapes=[pltpu.VMEM((2, tq, LANES), BF16),
                        pltpu.VMEM((2, tq, LANES), F32),
                        pltpu.VMEM((2, tq, 2 * LANES), F32)],
        compiler_params=_params(("parallel", "arbitrary")),
        name="diff_attention",
    )(lam_rows, subln_g.reshape(1, LANES), q_src, k_all, v_all)


def _conv_c_kernel(main_ref, left_ref, right_ref, w_ref, o_ref, ext_scr, *, t):
    i = pl.program_id(0)
    last = pl.num_programs(0) - 1

    def gated(ref):
        v = ref[...].astype(F32)
        return v[:, C_WIDTH:2 * C_WIDTH] * v[:, 2 * C_WIDTH:]

    ext_scr[0:HALO, :] = jnp.where(i > 0, gated(left_ref), 0.0)
    ext_scr[HALO:HALO + t, :] = gated(main_ref)
    ext_scr[HALO + t:, :] = jnp.where(i < last, gated(right_ref), 0.0)
    ext = ext_scr[...]
    rows = t + 2 * HALO
    prev = pltpu.roll(ext, 1, 0)[HALO:HALO + t]
    nxt = pltpu.roll(ext, rows - 1, 0)[HALO:HALO + t]
    conv = prev * w_ref[0:1, :] + ext[HALO:HALO + t] * w_ref[1:2, :] + nxt * w_ref[2:3, :]
    o_ref[...] = (main_ref[:, :C_WIDTH].astype(F32) * conv).astype(BF16)


def _short_conv(proj, w, *, t):
    n = proj.shape[0]
    per = t // HALO
    nh = n // HALO
    return pl.pallas_call(
        functools.partial(_conv_c_kernel, t=t),
        grid=(n // t,),
        in_specs=[pl.BlockSpec((t, C_IN), lambda i: (i, 0)),
                  pl.BlockSpec((HALO, C_IN), lambda i: (jnp.maximum(i * per - 1, 0), 0)),
                  pl.BlockSpec((HALO, C_IN), lambda i: (jnp.minimum((i + 1) * per, nh - 1), 0)),
                  pl.BlockSpec((3, C_WIDTH), lambda i: (0, 0))],
        out_specs=pl.BlockSpec((t, C_WIDTH), lambda i: (i, 0)),
        out_shape=jax.ShapeDtypeStruct((n, C_WIDTH), BF16),
        scratch_shapes=[pltpu.VMEM((t + 2 * HALO, C_WIDTH), F32)],
        compiler_params=_params(("parallel",)),
        name="short_conv",
    )(proj, proj, proj, w)


def _win_attn_kernel(sink_ref, q_ref, kp_ref, km_ref, kn_ref, vp_ref, vm_ref, vn_ref, kc_ref, vc_ref, o_ref,
                     *, tq, n):
    i = pl.program_id(0)
    kv = pl.program_id(1)
    kw = jnp.concatenate([kp_ref[...], km_ref[...], kn_ref[...]], axis=0)
    vw = jnp.concatenate([vp_ref[...], vm_ref[...], vn_ref[...]], axis=0)
    kc = kc_ref[...]
    vc = vc_ref[...]
    wk = tq + 2 * WINDOW
    row = lax.broadcasted_iota(jnp.int32, (tq, wk), 0)
    col = lax.broadcasted_iota(jnp.int32, (tq, wk), 1)
    pos = i * tq - WINDOW + col
    mask = (col >= row) & (col <= row + 2 * WINDOW) & (pos >= 0) & (pos < n)
    lane = lax.broadcasted_iota(jnp.int32, (tq, LANES), 1)
    qms = []
    for g in range(D_GROUP):
        qa = q_ref[:, (g // 2) * LANES:(g // 2 + 1) * LANES]
        sel = (lane < HEAD_DIM) if g % 2 == 0 else (lane >= HEAD_DIM)
        qms.append(jnp.where(sel, qa, jnp.zeros_like(qa)))
    s_cs = [_dot_nt(qm, kc) for qm in qms]
    s_ws = [_dot_nt(qm, kw) for qm in qms]
    probs = []
    for g in range(D_GROUP):
        s_c = s_cs[g]
        s_w = jnp.where(mask, s_ws[g], NEG_INF)
        sk = sink_ref[kv * D_GROUP + g]
        mx = jnp.maximum(jnp.maximum(jnp.max(s_c, axis=1, keepdims=True),
                                     jnp.max(s_w, axis=1, keepdims=True)), sk)
        p_c = jnp.exp(s_c - mx)
        p_w = jnp.exp(s_w - mx)
        den = (jnp.sum(p_c, axis=1, keepdims=True) + jnp.sum(p_w, axis=1, keepdims=True)
               + jnp.exp(sk - mx))
        probs.append((p_c.astype(BF16), p_w.astype(BF16), den))
    outs = [(jnp.dot(p_c, vc, preferred_element_type=F32) + jnp.dot(p_w, vw, preferred_element_type=F32)) / den
            for p_c, p_w, den in probs]
    for a in range(D_GROUP // 2):
        o_ref[:, a * LANES:(a + 1) * LANES] = jnp.where(lane < HEAD_DIM, outs[2 * a], outs[2 * a + 1]).astype(BF16)


def _window_attention(sink, proj, q_col0, k_col0, v_col0, pc, *, tq):
    n = proj.shape[0]
    nc = pc.shape[0]
    qb = q_col0 // (2 * LANES)
    kb, vb = k_col0 // LANES, v_col0 // LANES
    per = tq // WINDOW
    nw = n // WINDOW

    def window(col0):
        return [pl.BlockSpec((WINDOW, LANES), lambda i, kv: (jnp.maximum(i * per - 1, 0), col0 + kv)),
                pl.BlockSpec((tq, LANES), lambda i, kv: (i, col0 + kv)),
                pl.BlockSpec((WINDOW, LANES), lambda i, kv: (jnp.minimum((i + 1) * per, nw - 1), col0 + kv))]

    return pl.pallas_call(
        functools.partial(_win_attn_kernel, tq=tq, n=n),
        grid=(n // tq, D_KV_HEADS),
        in_specs=[pl.BlockSpec(memory_space=pltpu.SMEM),
                  pl.BlockSpec((tq, 2 * LANES), lambda i, kv: (i, qb + kv))]
                 + window(kb) + window(vb)
                 + [pl.BlockSpec((nc, LANES), lambda i, kv: (0, kv)),
                    pl.BlockSpec((nc, LANES), lambda i, kv: (0, D_KV_HEADS + kv))],
        out_specs=pl.BlockSpec((tq, 2 * LANES), lambda i, kv: (i, kv)),
        out_shape=jax.ShapeDtypeStruct((n, D_Q_WIDTH), BF16),
        compiler_params=_params(("parallel", "parallel")),
        name="window_attention",
    )(sink, proj, proj, proj, proj, proj, proj, proj, pc, pc)


def _dup_head_cols(w):
    d = w.shape[0]
    w = w.reshape(d, D_KV_HEADS, 1, HEAD_DIM)
    return jnp.broadcast_to(w, (d, D_KV_HEADS, 2, HEAD_DIM)).reshape(d, D_KV_HEADS * LANES)


def _out_kernel(a_ref, b_ref, x_ref, w_ref, gpost_ref, gate_ref, gpre_ref, sc_ref, sh_ref, x1_ref, hf_ref):
    half = a_ref.shape[1]
    y = (jnp.dot(a_ref[...], w_ref[:half, :], preferred_element_type=F32)
         + jnp.dot(b_ref[...], w_ref[half:, :], preferred_element_type=F32))
    x1 = x_ref[...] + gate_ref[...] * (_rms(y) * gpost_ref[...])
    x1_ref[...] = x1
    hf_ref[...] = ((_rms(x1) * gpre_ref[...]) * (1.0 + sc_ref[...]) + sh_ref[...]).astype(BF16)


def _out_project(a, b, x, w, g_post, gate, g_pre, scale, shift, *, tm):
    n, d = x.shape
    half = a.shape[1]
    vec = pl.BlockSpec((1, d), lambda i: (0, 0))
    return pl.pallas_call(
        _out_kernel,
        grid=(n // tm,),
        in_specs=[pl.BlockSpec((tm, half), lambda i: (i, 0)),
                  pl.BlockSpec((tm, half), lambda i: (i, 0)),
                  pl.BlockSpec((tm, d), lambda i: (i, 0)),
                  pl.BlockSpec((2 * half, d), lambda i: (0, 0)),
                  vec, vec, vec, vec, vec],
        out_specs=[pl.BlockSpec((tm, d), lambda i: (i, 0)), pl.BlockSpec((tm, d), lambda i: (i, 0))],
        out_shape=[jax.ShapeDtypeStruct((n, d), F32), jax.ShapeDtypeStruct((n, d), BF16)],
        compiler_params=_params(("parallel",)),
        name="out_project",
    )(a, b, x, w, g_post, gate, g_pre, scale, shift)


def _mlp_kernel(hf_ref, w1_ref, w2_ref, x1_ref, gpost_ref, gate_ref, o_ref, acc_scr):
    f = pl.program_id(1)

    @pl.when(f == 0)
    def _():
        acc_scr[...] = jnp.zeros(acc_scr.shape, F32)

    h = jnp.dot(hf_ref[...], w1_ref[...], preferred_element_type=F32)
    h = jnp.square(jnp.maximum(h, 0.0)).astype(BF16)
    acc_scr[...] += jnp.dot(h, w2_ref[...], preferred_element_type=F32)

    @pl.when(f == pl.num_programs(1) - 1)
    def _():
        o_ref[...] = x1_ref[...] + gate_ref[...] * (_rms(acc_scr[...]) * gpost_ref[...])


def _mlp(hf, w1, w2, x1, g_post, gate, *, tm, tf):
    n, d = x1.shape
    ff = w1.shape[1]
    vec = pl.BlockSpec((1, d), lambda i, f: (0, 0))
    return pl.pallas_call(
        _mlp_kernel,
        grid=(n // tm, ff // tf),
        in_specs=[pl.BlockSpec((tm, d), lambda i, f: (i, 0)),
                  pl.BlockSpec((d, tf), lambda i, f: (0, f)),
                  pl.BlockSpec((tf, d), lambda i, f: (f, 0)),
                  pl.BlockSpec((tm, d), lambda i, f: (i, 0)),
                  vec, vec],
        out_specs=pl.BlockSpec((tm, d), lambda i, f: (i, 0)),
        out_shape=jax.ShapeDtypeStruct((n, d), F32),
        scratch_shapes=[pltpu.VMEM((tm, d), F32)],
        compiler_params=_params(("parallel", "arbitrary")),
        name="mlp",
    )(hf, w1, w2, x1, g_post, gate)


K_PLAIN, K_ROPE, K_ROPE_Q0, K_ROPE_Q1 = range(4)
CK_PLAIN, CK_Q0 = range(2)
Q0_SCALE = ATTN_SCALE * LOG2E
Q1_SCALE = ATTN_SCALE


def _latent_tables(n):
    rows = n // GRID_W
    pairs = HEAD_DIM // 4
    inv = jnp.power(ROPE_BASE, -jnp.arange(pairs, dtype=F32) / pairs)
    ang_r = jnp.arange(rows, dtype=F32)[:, None] * inv
    ang_c = jnp.arange(GRID_W, dtype=F32)[:, None] * inv

    def grid(f):
        r = jnp.broadcast_to(f(ang_r)[:, None, :], (rows, GRID_W, pairs))
        c = jnp.broadcast_to(f(ang_c)[None, :, :], (rows, GRID_W, pairs))
        return jnp.concatenate([r, c], axis=-1).reshape(n, 2 * pairs)

    cos, sin = grid(jnp.cos), grid(jnp.sin)
    rope = jnp.stack([jnp.tile(cos, (1, 4)), jnp.tile(jnp.concatenate([-sin, sin], axis=-1), (1, 2))])
    plain = jnp.stack([jnp.ones((n, LANES), F32), jnp.zeros((n, LANES), F32)])
    return jnp.stack([plain, rope, rope * Q0_SCALE, rope * Q1_SCALE])


def _context_tables(nc):
    one, zero = jnp.ones((nc, LANES), F32), jnp.zeros((nc, LANES), F32)
    return jnp.stack([jnp.stack([one, zero]), jnp.stack([one * Q0_SCALE, zero])])


def _row(v):
    return v.reshape(1, -1)


def _split_mod(m):
    return [_row(t) for t in jnp.split(m, N_MOD)]


def kernel(x, c, ctx, c_ctx, l0_mod_w, l0_mod_b, l0_norm_mix_pre, l0_norm_mix_post, l0_norm_mlp_pre, l0_norm_mlp_post, l0_w_in, l0_conv_w, l0_conv_b, l0_ln_g, l0_ln_b, l0_lambda_q1, l0_lambda_k1, l0_lambda_q2, l0_lambda_k2, l0_subln_g, l0_w_out, l0_mlp_w1, l0_mlp_w2, l1_mod_w, l1_mod_b, l1_norm_mix_pre, l1_norm_mix_post, l1_norm_mlp_pre, l1_norm_mlp_post, l1_w_in, l1_sconv_w, l1_sink, l1_w_out, l1_mlp_w1, l1_mlp_w2):
    n = x.shape[1]
    nc = ctx.shape[1]
    xs = x[0]
    cs = ctx[0]
    tabs = _latent_tables(n)
    ctabs = _context_tables(nc)
    cvecs = jnp.zeros((8, D_MODEL), F32).at[0].set(c[0]).at[1].set(c_ctx)

    mod = _modulation(cvecs, l0_mod_w, l0_mod_b)
    sh_m, sc_m, gt_m, sh_f, sc_f, gt_f = _split_mod(mod[0])
    csh_m, csc_m, cgt_m, csh_f, csc_f, cgt_f = _split_mod(mod[1])
    w_in = l0_w_in.astype(BF16)
    w_out = l0_w_out.astype(BF16)
    w1 = l0_mlp_w1.astype(BF16)
    w2 = l0_mlp_w2.astype(BF16)
    g_pre, g_post = _row(l0_norm_mix_pre), _row(l0_norm_mix_post)
    gf_pre, gf_post = _row(l0_norm_mlp_pre), _row(l0_norm_mlp_post)
    q0, k0, v0 = A_IN, A_IN + B_WIDTH, A_IN + 2 * B_WIDTH

    blocks = lambda *widths_kinds: [k for w, k in widths_kinds for _ in range(w // PROJ_SUB)]
    proj = _project(xs, g_pre, sc_m, sh_m, w_in, tabs,
                    blocks((A_IN, K_PLAIN), (B_WIDTH, K_ROPE_Q0), (B_WIDTH, K_ROPE), (B_WIDTH, K_PLAIN)),
                    tm=1024, tn=1024)
    pc = _project(cs, g_pre, csc_m, csh_m, w_in, ctabs,
                  blocks((A_IN, CK_PLAIN), (B_WIDTH, CK_Q0), (2 * B_WIDTH, CK_PLAIN)), tm=nc, tn=1024)
    lam_init = 0.8 - 0.6 * math.exp(-0.3 * 0)
    lam_rows = jnp.zeros((8, LANES), F32)
    for r, lv in enumerate((l0_lambda_q1, l0_lambda_k1, l0_lambda_q2, l0_lambda_k2)):
        lam_rows = lam_rows.at[r, :HEAD_DIM].set(lv)

    a_lat = _conformer_conv(proj, l0_conv_w, l0_conv_b, l0_ln_g, l0_ln_b, t=256)
    a_ctx = _conformer_conv(pc, l0_conv_w, l0_conv_b, l0_ln_g, l0_ln_b, t=nc)
    k_c, v_c = pc[:, k0:v0], pc[:, v0:]
    k_all = jnp.concatenate([k_c, proj[:, k0:v0]], axis=0)
    v_all = jnp.concatenate([v_c, proj[:, v0:]], axis=0)
    b_lat = _diff_attention(lam_rows, l0_subln_g, proj, q0, k_all, v_all, tq=1024, tk=3328, lam_init=lam_init)
    b_ctx = _diff_attention(lam_rows, l0_subln_g, pc, q0, k_c, v_c, tq=nc, tk=nc, lam_init=lam_init)

    x1, hf = _out_project(a_lat, b_lat, xs, w_out, g_post, gt_m, gf_pre, sc_f, sh_f, tm=256)
    xs = _mlp(hf, w1, w2, x1, gf_post, gt_f, tm=512, tf=1024)
    c1, hcf = _out_project(a_ctx, b_ctx, cs, w_out, g_post, cgt_m, gf_pre, csc_f, csh_f, tm=nc)
    cs = _mlp(hcf, w1, w2, c1, gf_post, cgt_f, tm=nc, tf=512)

    mod = _modulation(cvecs, l1_mod_w, l1_mod_b)
    sh_m, sc_m, gt_m, sh_f, sc_f, gt_f = _split_mod(mod[0])
    csh_m, csc_m = _split_mod(mod[1])[:2]
    w_out = l1_w_out.astype(BF16)
    w1 = l1_mlp_w1.astype(BF16)
    w2 = l1_mlp_w2.astype(BF16)
    g_pre, g_post = _row(l1_norm_mix_pre), _row(l1_norm_mix_post)
    gf_pre, gf_post = _row(l1_norm_mlp_pre), _row(l1_norm_mlp_post)
    q0, k0, v0 = C_IN, C_IN + D_Q_WIDTH, C_IN + D_Q_WIDTH + D_KV_WIDTH
    w_in = jnp.concatenate([l1_w_in[:, :k0], _dup_head_cols(l1_w_in[:, k0:v0]), _dup_head_cols(l1_w_in[:, v0:])],
                           axis=1).astype(BF16)
    kd0, vd0 = k0, k0 + 2 * D_KV_WIDTH

    proj = _project(xs, g_pre, sc_m, sh_m, w_in, tabs,
                    blocks((C_IN, K_PLAIN), (D_Q_WIDTH, K_ROPE_Q1), (2 * D_KV_WIDTH, K_ROPE),
                           (2 * D_KV_WIDTH, K_PLAIN)), tm=1024, tn=1024)
    pc = _project(cs, g_pre, csc_m, csh_m, w_in[:, kd0:], ctabs, blocks((4 * D_KV_WIDTH, CK_PLAIN)),
                  tm=nc, tn=1024)
    c_lat = _short_conv(proj, l1_sconv_w, t=512)
    d_lat = _window_attention(l1_sink, proj, q0, kd0, vd0, pc, tq=256)
    x1, hf = _out_project(c_lat, d_lat, xs, w_out, g_post, gt_m, gf_pre, sc_f, sh_f, tm=256)
    xs = _mlp(hf, w1, w2, x1, gf_post, gt_f, tm=512, tf=1024)
    return xs[None]
```

```python
import functools
import math

import jax
import jax.numpy as jnp
from jax import lax
from jax.experimental import pallas as pl
from jax.experimental.pallas import tpu as pltpu

F32 = jnp.float32
BF16 = jnp.bfloat16

D_MODEL = 2048
HEAD_DIM = 64
HALF = HEAD_DIM // 2
GRID_W = 64
ROPE_BASE = 10000.0
NORM_EPS = 1e-6
LN_EPS = 1e-5
ATTN_SCALE = HEAD_DIM ** -0.5
LOG2E = math.log2(math.e)
NEG_INF = -1e30
N_MOD = 6
WINDOW = 128
LANES = 128
HALO = 16

A_WIDTH = 1024
A_IN = 2 * A_WIDTH
CONV_A_WIDTH = 31
B_HEADS = 8
B_WIDTH = 1024
C_WIDTH = 1024
C_IN = 3 * C_WIDTH
D_HEADS = 16
D_KV_HEADS = 4
D_GROUP = 4
D_Q_WIDTH = 1024
D_KV_WIDTH = 256
D_FF = 4 * D_MODEL

VMEM_LIMIT = 56 * 1024 * 1024


def _params(sem):
    return pltpu.CompilerParams(dimension_semantics=sem, vmem_limit_bytes=VMEM_LIMIT)


def _rms(x, eps=NORM_EPS):
    return x * lax.rsqrt(jnp.mean(x * x, axis=-1, keepdims=True) + eps)


def _dot_nt(a, b):
    return lax.dot_general(a, b, (((1,), (1,)), ((), ())), preferred_element_type=F32)


def _mod_kernel(c_ref, w_ref, b_ref, o_ref):
    c = c_ref[...]
    s = c * jax.nn.sigmoid(c)
    o_ref[...] = jnp.dot(s, w_ref[...], preferred_element_type=F32) + b_ref[...]


def _modulation(cvecs, w, b):
    tn = 1024
    n_out = w.shape[1]
    return pl.pallas_call(
        _mod_kernel,
        grid=(n_out // tn,),
        in_specs=[pl.BlockSpec((8, D_MODEL), lambda j: (0, 0)),
                  pl.BlockSpec((D_MODEL, tn), lambda j: (0, j)),
                  pl.BlockSpec((1, tn), lambda j: (0, j))],
        out_specs=pl.BlockSpec((8, tn), lambda j: (0, j)),
        out_shape=jax.ShapeDtypeStruct((8, n_out), F32),
        compiler_params=_params(("arbitrary",)),
        name="modulation",
    )(cvecs, w, b.reshape(1, n_out))


PROJ_SUB = 512


def _proj_kernel(coef_ref, x_ref, g_ref, sc_ref, sh_ref, tab_ref, w_ref, o_ref, h_scr, *, tn):
    nsub = tn // PROJ_SUB
    j = pl.program_id(1)

    @pl.when(j == 0)
    def _():
        y = _rms(x_ref[...]) * g_ref[...]
        h_scr[...] = (y * (1.0 + sc_ref[...]) + sh_ref[...]).astype(BF16)

    acc = jnp.dot(h_scr[...], w_ref[...], preferred_element_type=F32)
    cosf = tab_ref[0]
    sinf = tab_ref[1]
    lane = lax.broadcasted_iota(jnp.int32, cosf.shape, 1)
    first_half = (lane & (HEAD_DIM - 1)) < HALF
    for s in range(nsub):
        alpha = coef_ref[2 * (j * nsub + s)]
        beta = coef_ref[2 * (j * nsub + s) + 1]
        a = alpha * cosf + beta
        b = alpha * sinf
        for c in range(s * PROJ_SUB // LANES, (s + 1) * PROJ_SUB // LANES):
            y = acc[:, c * LANES:(c + 1) * LANES]
            swapped = jnp.where(first_half, pltpu.roll(y, LANES - HALF, 1), pltpu.roll(y, HALF, 1))
            o_ref[:, c * LANES:(c + 1) * LANES] = (y * a + swapped * b).astype(BF16)


def _project(x, g, scale, shift, w, rope_tab, coefs, *, tm, tn):
    n, d = x.shape
    p = w.shape[1]
    assert len(coefs) == p // PROJ_SUB
    vec = pl.BlockSpec((1, d), lambda i, j: (0, 0))
    return pl.pallas_call(
        functools.partial(_proj_kernel, tn=tn),
        grid=(n // tm, p // tn),
        in_specs=[pl.BlockSpec(memory_space=pltpu.SMEM),
                  pl.BlockSpec((tm, d), lambda i, j: (i, 0)), vec, vec, vec,
                  pl.BlockSpec((2, tm, LANES), lambda i, j: (0, i, 0)),
                  pl.BlockSpec((d, tn), lambda i, j: (0, j))],
        out_specs=pl.BlockSpec((tm, tn), lambda i, j: (i, j)),
        out_shape=jax.ShapeDtypeStruct((n, p), BF16),
        scratch_shapes=[pltpu.VMEM((tm, d), BF16)],
        compiler_params=_params(("parallel", "arbitrary")),
        name="norm_project",
    )(jnp.asarray(coefs, F32).reshape(-1), x, g, scale, shift, rope_tab, w)


CONV_A_ROWS = 32


def _conv_a_kernel(main_ref, left_ref, right_ref, w_ref, cb_ref, lg_ref, lb_ref, o_ref, ext_scr, sh_scr, *, t):
    i = pl.program_id(0)
    last = pl.num_programs(0) - 1

    def glu(ref):
        v = ref[...].astype(F32)
        return v[:, :A_WIDTH] * jax.nn.sigmoid(v[:, A_WIDTH:])

    ext_scr[0:HALO, :] = jnp.where(i > 0, glu(left_ref), 0.0)
    ext_scr[HALO:HALO + t, :] = glu(main_ref)
    ext_scr[HALO + t:, :] = jnp.where(i < last, glu(right_ref), 0.0)
    ext = ext_scr[...]
    rows = t + 2 * HALO
    srows = t + 3 * 8
    sh_scr[0] = ext[:srows]
    for b in range(1, 8):
        sh_scr[b] = pltpu.roll(ext, rows - b, 0)[:srows]

    def chunk(c, carry):
        r0 = pl.multiple_of(c * CONV_A_ROWS, CONV_A_ROWS)
        acc = jnp.zeros((CONV_A_ROWS, A_WIDTH), F32)
        for k in range(CONV_A_WIDTH):
            a, b = divmod(k + 1, 8)
            acc = acc + sh_scr[b, pl.ds(r0 + 8 * a, CONV_A_ROWS), :] * w_ref[k:k + 1, :]
        v = acc + cb_ref[...]
        mu = jnp.mean(v, axis=-1, keepdims=True)
        vc = v - mu
        var = jnp.mean(vc * vc, axis=-1, keepdims=True)
        y = vc * lax.rsqrt(var + LN_EPS) * lg_ref[...] + lb_ref[...]
        o_ref[pl.ds(r0, CONV_A_ROWS), :] = (y * jax.nn.sigmoid(y)).astype(BF16)
        return carry

    lax.fori_loop(0, t // CONV_A_ROWS, chunk, 0, unroll=2)


def _conformer_conv(proj, conv_w, conv_b, ln_g, ln_b, *, t):
    n = proj.shape[0]
    per = t // HALO
    nh = n // HALO
    vec = pl.BlockSpec((1, A_WIDTH), lambda i: (0, 0))
    return pl.pallas_call(
        functools.partial(_conv_a_kernel, t=t),
        grid=(n // t,),
        in_specs=[pl.BlockSpec((t, A_IN), lambda i: (i, 0)),
                  pl.BlockSpec((HALO, A_IN), lambda i: (jnp.maximum(i * per - 1, 0), 0)),
                  pl.BlockSpec((HALO, A_IN), lambda i: (jnp.minimum((i + 1) * per, nh - 1), 0)),
                  pl.BlockSpec((CONV_A_WIDTH, A_WIDTH), lambda i: (0, 0)),
                  vec, vec, vec],
        out_specs=pl.BlockSpec((t, A_WIDTH), lambda i: (i, 0)),
        out_shape=jax.ShapeDtypeStruct((n, A_WIDTH), BF16),
        scratch_shapes=[pltpu.VMEM((t + 2 * HALO, A_WIDTH), F32),
                        pltpu.VMEM((8, t + 24, A_WIDTH), F32)],
        compiler_params=_params(("parallel",)),
        name="conformer_conv",
    )(proj, proj, proj, conv_w, conv_b.reshape(1, -1), ln_g.reshape(1, -1), ln_b.reshape(1, -1))


def _tile_lanes(x, reps):
    return x if reps == 1 else jnp.concatenate([x] * reps, axis=1)


def _diff_attn_kernel(lam_ref, sub_ref, q_ref, k_ref, v_ref, o_ref, qm_scr, m_scr, acc_scr, *, lam_init, tk):
    q = q_ref[...]
    lane = lax.broadcasted_iota(jnp.int32, q.shape, 1)
    zero = jnp.zeros_like(q)
    qm_scr[0] = jnp.where(lane < HEAD_DIM, q, zero)
    qm_scr[1] = jnp.where(lane >= HEAD_DIM, q, zero)
    m_scr[...] = jnp.full(m_scr.shape, -jnp.inf, F32)
    acc_scr[...] = jnp.zeros(acc_scr.shape, F32)
    ones = jnp.ones((tk, LANES), BF16)

    def chunk(c, carry):
        r0 = pl.multiple_of(c * tk, tk)
        k = k_ref[pl.ds(r0, tk), :]
        v_ext = jnp.concatenate([v_ref[pl.ds(r0, tk), :], ones], axis=1)
        scores = [_dot_nt(qm_scr[m], k) for m in range(2)]
        for m in range(2):
            s = scores[m]
            m_prev = m_scr[m]
            m_new = jnp.maximum(m_prev, jnp.max(s, axis=1, keepdims=True))
            alpha = jnp.exp2(m_prev - m_new)
            p = jnp.exp2(s - _tile_lanes(m_new, tk // LANES))
            acc_scr[m] = (_tile_lanes(alpha, 2) * acc_scr[m]
                          + jnp.dot(p.astype(BF16), v_ext, preferred_element_type=F32))
            m_scr[m] = m_new
        return carry

    lax.fori_loop(0, k_ref.shape[0] // tk, chunk, 0)

    lp = lam_ref[...]
    lam = (jnp.exp(jnp.sum(lp[0:1] * lp[1:2], axis=1, keepdims=True))
           - jnp.exp(jnp.sum(lp[2:3] * lp[3:4], axis=1, keepdims=True)) + lam_init)
    o = (acc_scr[0, :, :LANES] / acc_scr[0, :, LANES:]
         - lam * (acc_scr[1, :, :LANES] / acc_scr[1, :, LANES:]))
    o = _rms(o) * sub_ref[...] * (1.0 - lam_init)
    o_ref[...] = o.astype(BF16)


def _diff_attention(lam_rows, subln_g, q_src, q_col0, k_all, v_all, *, tq, tk, lam_init):
    n = q_src.shape[0]
    nk = k_all.shape[0]
    qb = q_col0 // LANES
    return pl.pallas_call(
        functools.partial(_diff_attn_kernel, lam_init=lam_init, tk=tk),
        grid=(B_HEADS, n // tq),
        in_specs=[pl.BlockSpec((8, LANES), lambda h, i: (0, 0)),
                  pl.BlockSpec((1, LANES), lambda h, i: (0, 0)),
                  pl.BlockSpec((tq, LANES), lambda h, i: (i, qb + h)),
                  pl.BlockSpec((nk, LANES), lambda h, i: (0, h), pipeline_mode=pl.Buffered(1)),
                  pl.BlockSpec((nk, LANES), lambda h, i: (0, h), pipeline_mode=pl.Buffered(1))],
        out_specs=pl.BlockSpec((tq, LANES), lambda h, i: (i, h)),
        out_shape=jax.ShapeDtypeStruct((n, B_WIDTH), BF16),
        scratch_shapes=[pltpu.VMEM((2, tq, LANES), BF16),
                        pltpu.VMEM((2, tq, LANES), F32),
                        pltpu.VMEM((2, tq, 2 * LANES), F32)],
        compiler_params=_params(("parallel", "arbitrary")),
        name="diff_attention",
    )(lam_rows, subln_g.reshape(1, LANES), q_src, k_all, v_all)


def _conv_c_kernel(main_ref, left_ref, right_ref, w_ref, o_ref, ext_scr, *, t):
    i = pl.program_id(0)
    last = pl.num_programs(0) - 1

    def gated(ref):
        v = ref[...].astype(F32)
        return v[:, C_WIDTH:2 * C_WIDTH] * v[:, 2 * C_WIDTH:]

    ext_scr[0:HALO, :] = jnp.where(i > 0, gated(left_ref), 0.0)
    ext_scr[HALO:HALO + t, :] = gated(main_ref)
    ext_scr[HALO + t:, :] = jnp.where(i < last, gated(right_ref), 0.0)
    ext = ext_scr[...]
    rows = t + 2 * HALO
    prev = pltpu.roll(ext, 1, 0)[HALO:HALO + t]
    nxt = pltpu.roll(ext, rows - 1, 0)[HALO:HALO + t]
    conv = prev * w_ref[0:1, :] + ext[HALO:HALO + t] * w_ref[1:2, :] + nxt * w_ref[2:3, :]
    o_ref[...] = (main_ref[:, :C_WIDTH].astype(F32) * conv).astype(BF16)


def _short_conv(proj, w, *, t):
    n = proj.shape[0]
    per = t // HALO
    nh = n // HALO
    return pl.pallas_call(
        functools.partial(_conv_c_kernel, t=t),
        grid=(n // t,),
        in_specs=[pl.BlockSpec((t, C_IN), lambda i: (i, 0)),
                  pl.BlockSpec((HALO, C_IN), lambda i: (jnp.maximum(i * per - 1, 0), 0)),
                  pl.BlockSpec((HALO, C_IN), lambda i: (jnp.minimum((i + 1) * per, nh - 1), 0)),
                  pl.BlockSpec((3, C_WIDTH), lambda i: (0, 0))],
        out_specs=pl.BlockSpec((t, C_WIDTH), lambda i: (i, 0)),
        out_shape=jax.ShapeDtypeStruct((n, C_WIDTH), BF16),
        scratch_shapes=[pltpu.VMEM((t + 2 * HALO, C_WIDTH), F32)],
        compiler_params=_params(("parallel",)),
        name="short_conv",
    )(proj, proj, proj, w)


def _win_attn_kernel(sink_ref, q_ref, kp_ref, km_ref, kn_ref, vp_ref, vm_ref, vn_ref, c_ref, o_ref, *, tq, n):
    i = pl.program_id(0)
    lane = lax.broadcasted_iota(jnp.int32, (tq, LANES), 1)
    low = lane < HEAD_DIM

    def with_ones(v, keep_low):
        col = lax.broadcasted_iota(jnp.int32, v.shape, 1)
        return jnp.where((col < HEAD_DIM) if keep_low else (col >= HEAD_DIM), v, jnp.ones_like(v))

    nb = tq // WINDOW
    span = 3 * WINDOW
    row = lax.broadcasted_iota(jnp.int32, (WINDOW, span), 0)
    col = lax.broadcasted_iota(jnp.int32, (WINDOW, span), 1)
    band = (col >= row) & (col <= row + 2 * WINDOW)
    masks = []
    for blk in range(nb):
        pos = i * tq + (blk - 1) * WINDOW + col
        masks.append(band & (pos >= 0) & (pos < n))
    zeros = jnp.zeros((WINDOW, WINDOW), BF16)

    def scores(kv):
        tile = slice(kv * LANES, (kv + 1) * LANES)
        kw = jnp.concatenate([kp_ref[:, tile], km_ref[:, tile], kn_ref[:, tile]], axis=0)
        kc = c_ref[:, tile]
        qms = []
        for g in range(D_GROUP):
            c0 = (kv * D_GROUP + g) // 2 * LANES
            qa = q_ref[:, c0:c0 + LANES]
            qms.append(jnp.where(low if g % 2 == 0 else (lane >= HEAD_DIM), qa, jnp.zeros_like(qa)))
        q_all = jnp.concatenate(qms, axis=0)
        return _dot_nt(q_all, kc), _dot_nt(q_all, kw)

    def softmax(kv, s_c_all, s_w_all):
        probs = []
        for g in range(D_GROUP):
            sk = sink_ref[kv * D_GROUP + g] * LOG2E
            p_cs, p_ws, sinks = [], [], []
            for blk in range(nb):
                rows = slice(g * tq + blk * WINDOW, g * tq + (blk + 1) * WINDOW)
                s_c = s_c_all[rows]
                s_w = jnp.where(masks[blk], s_w_all[rows, blk * WINDOW:blk * WINDOW + span], NEG_INF)
                mx = jnp.maximum(jnp.maximum(jnp.max(s_c, axis=1, keepdims=True),
                                             jnp.max(s_w, axis=1, keepdims=True)), sk)
                p_cs.append(jnp.exp2(s_c - mx).astype(BF16))
                p_ws.append(jnp.concatenate([zeros] * blk + [jnp.exp2(s_w - mx).astype(BF16)]
                                            + [zeros] * (nb - 1 - blk), axis=1))
                sinks.append(jnp.exp2(sk - mx))
            probs.append((jnp.concatenate(p_cs, axis=0), jnp.concatenate(p_ws, axis=0),
                          jnp.concatenate(sinks, axis=0)))
        return probs

    def outputs(kv, probs):
        tile = slice(kv * LANES, (kv + 1) * LANES)
        vw = jnp.concatenate([vp_ref[:, tile], vm_ref[:, tile], vn_ref[:, tile]], axis=0)
        vc = c_ref[:, D_KV_HEADS * LANES + kv * LANES:D_KV_HEADS * LANES + (kv + 1) * LANES]
        outs = [None] * D_GROUP
        for parity in range(2):
            heads = list(range(parity, D_GROUP, 2))
            p_c = jnp.concatenate([probs[g][0] for g in heads], axis=0)
            p_w = jnp.concatenate([probs[g][1] for g in heads], axis=0)
            p_sink = jnp.concatenate([probs[g][2] for g in heads], axis=0)
            o = (jnp.dot(p_c, with_ones(vc, parity == 0), preferred_element_type=F32)
                 + jnp.dot(p_w, with_ones(vw, parity == 0), preferred_element_type=F32))
            o = o / (pltpu.roll(o, HEAD_DIM, 1) + p_sink)
            for idx, g in enumerate(heads):
                outs[g] = o[idx * tq:(idx + 1) * tq]
        for a in range(D_GROUP // 2):
            c0 = (kv * D_GROUP // 2 + a) * LANES
            o_ref[:, c0:c0 + LANES] = jnp.where(low, outs[2 * a], outs[2 * a + 1]).astype(BF16)

    for kv in range(D_KV_HEADS):
        outputs(kv, softmax(kv, *scores(kv)))


def _window_attention(sink, proj, q_col0, k_col0, v_col0, pc, *, tq):
    n = proj.shape[0]
    nc = pc.shape[0]
    kvw = D_KV_HEADS * LANES
    per = tq // WINDOW
    nw = n // WINDOW

    def window(col0):
        cb = col0 // kvw
        return [pl.BlockSpec((WINDOW, kvw), lambda i: (jnp.maximum(i * per - 1, 0), cb)),
                pl.BlockSpec((tq, kvw), lambda i: (i, cb)),
                pl.BlockSpec((WINDOW, kvw), lambda i: (jnp.minimum((i + 1) * per, nw - 1), cb))]

    return pl.pallas_call(
        functools.partial(_win_attn_kernel, tq=tq, n=n),
        grid=(n // tq,),
        in_specs=[pl.BlockSpec(memory_space=pltpu.SMEM),
                  pl.BlockSpec((tq, D_Q_WIDTH), lambda i: (i, q_col0 // D_Q_WIDTH))]
                 + window(k_col0) + window(v_col0)
                 + [pl.BlockSpec((nc, 2 * kvw), lambda i: (0, 0))],
        out_specs=pl.BlockSpec((tq, D_Q_WIDTH), lambda i: (i, 0)),
        out_shape=jax.ShapeDtypeStruct((n, D_Q_WIDTH), BF16),
        compiler_params=_params(("parallel",)),
        name="window_attention",
    )(sink, proj, proj, proj, proj, proj, proj, proj, pc)


def _dup_head_cols(w):
    d = w.shape[0]
    w = w.reshape(d, D_KV_HEADS, 1, HEAD_DIM)
    return jnp.broadcast_to(w, (d, D_KV_HEADS, 2, HEAD_DIM)).reshape(d, D_KV_HEADS * LANES)


OUT_ROWS = 256


def _out_kernel(a_ref, b_ref, x_ref, w_ref, gpost_ref, gate_ref, gpre_ref, sc_ref, sh_ref, x1_ref, hf_ref):
    half = a_ref.shape[1]
    for r in range(x_ref.shape[0] // OUT_ROWS):
        rows = slice(r * OUT_ROWS, (r + 1) * OUT_ROWS)
        y = (jnp.dot(a_ref[rows, :], w_ref[:half, :], preferred_element_type=F32)
             + jnp.dot(b_ref[rows, :], w_ref[half:, :], preferred_element_type=F32))
        x1 = x_ref[rows, :] + gate_ref[...] * (_rms(y) * gpost_ref[...])
        x1_ref[rows, :] = x1
        hf_ref[rows, :] = ((_rms(x1) * gpre_ref[...]) * (1.0 + sc_ref[...]) + sh_ref[...]).astype(BF16)


def _out_project(a, b, x, w, g_post, gate, g_pre, scale, shift, *, tm):
    n, d = x.shape
    half = a.shape[1]
    vec = pl.BlockSpec((1, d), lambda i: (0, 0))
    return pl.pallas_call(
        _out_kernel,
        grid=(n // tm,),
        in_specs=[pl.BlockSpec((tm, half), lambda i: (i, 0)),
                  pl.BlockSpec((tm, half), lambda i: (i, 0)),
                  pl.BlockSpec((tm, d), lambda i: (i, 0)),
                  pl.BlockSpec((2 * half, d), lambda i: (0, 0), pipeline_mode=pl.Buffered(1)),
                  vec, vec, vec, vec, vec],
        out_specs=[pl.BlockSpec((tm, d), lambda i: (i, 0)), pl.BlockSpec((tm, d), lambda i: (i, 0))],
        out_shape=[jax.ShapeDtypeStruct((n, d), F32), jax.ShapeDtypeStruct((n, d), BF16)],
        compiler_params=_params(("parallel",)),
        name="out_project",
    )(a, b, x, w, g_post, gate, g_pre, scale, shift)


def _mlp_kernel(hf_ref, w1_ref, w2_ref, x1_ref, gpost_ref, gate_ref, o_ref, acc_scr):
    f = pl.program_id(1)

    @pl.when(f == 0)
    def _():
        acc_scr[...] = jnp.zeros(acc_scr.shape, F32)

    h = jnp.dot(hf_ref[...], w1_ref[...], preferred_element_type=F32)
    h = jnp.square(jnp.maximum(h, 0.0)).astype(BF16)
    acc_scr[...] += jnp.dot(h, w2_ref[...], preferred_element_type=F32)

    @pl.when(f == pl.num_programs(1) - 1)
    def _():
        o_ref[...] = x1_ref[...] + gate_ref[...] * (_rms(acc_scr[...]) * gpost_ref[...])


def _mlp(hf, w1, w2, x1, g_post, gate, *, tm, tf):
    n, d = x1.shape
    ff = w1.shape[1]
    vec = pl.BlockSpec((1, d), lambda i, f: (0, 0))
    return pl.pallas_call(
        _mlp_kernel,
        grid=(n // tm, ff // tf),
        in_specs=[pl.BlockSpec((tm, d), lambda i, f: (i, 0)),
                  pl.BlockSpec((d, tf), lambda i, f: (0, f)),
                  pl.BlockSpec((tf, d), lambda i, f: (f, 0)),
                  pl.BlockSpec((tm, d), lambda i, f: (i, 0)),
                  vec, vec],
        out_specs=pl.BlockSpec((tm, d), lambda i, f: (i, 0)),
        out_shape=jax.ShapeDtypeStruct((n, d), F32),
        scratch_shapes=[pltpu.VMEM((tm, d), F32)],
        compiler_params=_params(("parallel", "arbitrary")),
        name="mlp",
    )(hf, w1, w2, x1, g_post, gate)


Q0_SCALE = ATTN_SCALE * LOG2E
Q1_SCALE = ATTN_SCALE * LOG2E
PLAIN = (0.0, 1.0)


def _rope_table(n):
    rows = n // GRID_W
    pairs = HEAD_DIM // 4
    inv = jnp.power(ROPE_BASE, -jnp.arange(pairs, dtype=F32) / pairs)
    ang_r = jnp.arange(rows, dtype=F32)[:, None] * inv
    ang_c = jnp.arange(GRID_W, dtype=F32)[:, None] * inv

    def grid(f):
        r = jnp.broadcast_to(f(ang_r)[:, None, :], (rows, GRID_W, pairs))
        c = jnp.broadcast_to(f(ang_c)[None, :, :], (rows, GRID_W, pairs))
        return jnp.concatenate([r, c], axis=-1).reshape(n, 2 * pairs)

    cos, sin = grid(jnp.cos), grid(jnp.sin)
    return jnp.stack([jnp.tile(cos, (1, 4)), jnp.tile(jnp.concatenate([-sin, sin], axis=-1), (1, 2))])


def _row(v):
    return v.reshape(1, -1)


def _split_mod(m):
    return [_row(t) for t in jnp.split(m, N_MOD)]


def kernel(x, c, ctx, c_ctx, l0_mod_w, l0_mod_b, l0_norm_mix_pre, l0_norm_mix_post, l0_norm_mlp_pre, l0_norm_mlp_post, l0_w_in, l0_conv_w, l0_conv_b, l0_ln_g, l0_ln_b, l0_lambda_q1, l0_lambda_k1, l0_lambda_q2, l0_lambda_k2, l0_subln_g, l0_w_out, l0_mlp_w1, l0_mlp_w2, l1_mod_w, l1_mod_b, l1_norm_mix_pre, l1_norm_mix_post, l1_norm_mlp_pre, l1_norm_mlp_post, l1_w_in, l1_sconv_w, l1_sink, l1_w_out, l1_mlp_w1, l1_mlp_w2):
    n = x.shape[1]
    nc = ctx.shape[1]
    xs = x[0]
    cs = ctx[0]
    rope = _rope_table(n)
    cvecs = jnp.zeros((8, D_MODEL), F32).at[0].set(c[0]).at[1].set(c_ctx)

    mod = _modulation(cvecs, l0_mod_w, l0_mod_b)
    sh_m, sc_m, gt_m, sh_f, sc_f, gt_f = _split_mod(mod[0])
    csh_m, csc_m, cgt_m, csh_f, csc_f, cgt_f = _split_mod(mod[1])
    w_in = l0_w_in.astype(BF16)
    w_out = l0_w_out.astype(BF16)
    w1 = l0_mlp_w1.astype(BF16)
    w2 = l0_mlp_w2.astype(BF16)
    g_pre, g_post = _row(l0_norm_mix_pre), _row(l0_norm_mix_post)
    gf_pre, gf_post = _row(l0_norm_mlp_pre), _row(l0_norm_mlp_post)
    q0, k0, v0 = A_IN, A_IN + B_WIDTH, A_IN + 2 * B_WIDTH

    blocks = lambda *widths_coefs: [c for w, c in widths_coefs for _ in range(w // PROJ_SUB)]
    proj = _project(xs, g_pre, sc_m, sh_m, w_in, rope,
                    blocks((A_IN, PLAIN), (B_WIDTH, (Q0_SCALE, 0.0)), (B_WIDTH, (1.0, 0.0)), (B_WIDTH, PLAIN)),
                    tm=1024, tn=1024)
    pc = _project(cs, g_pre, csc_m, csh_m, w_in, rope,
                  blocks((A_IN, PLAIN), (B_WIDTH, (0.0, Q0_SCALE)), (2 * B_WIDTH, PLAIN)), tm=nc, tn=1024)
    lam_init = 0.8 - 0.6 * math.exp(-0.3 * 0)
    lam_rows = jnp.zeros((8, LANES), F32)
    for r, lv in enumerate((l0_lambda_q1, l0_lambda_k1, l0_lambda_q2, l0_lambda_k2)):
        lam_rows = lam_rows.at[r, :HEAD_DIM].set(lv)

    a_lat = _conformer_conv(proj, l0_conv_w, l0_conv_b, l0_ln_g, l0_ln_b, t=256)
    a_ctx = _conformer_conv(pc, l0_conv_w, l0_conv_b, l0_ln_g, l0_ln_b, t=nc)
    k_c, v_c = pc[:, k0:v0], pc[:, v0:]
    k_all = jnp.concatenate([k_c, proj[:, k0:v0]], axis=0)
    v_all = jnp.concatenate([v_c, proj[:, v0:]], axis=0)
    b_lat = _diff_attention(lam_rows, l0_subln_g, proj, q0, k_all, v_all, tq=1024, tk=3328, lam_init=lam_init)
    b_ctx = _diff_attention(lam_rows, l0_subln_g, pc, q0, k_c, v_c, tq=nc, tk=nc, lam_init=lam_init)

    x1, hf = _out_project(a_lat, b_lat, xs, w_out, g_post, gt_m, gf_pre, sc_f, sh_f, tm=512)
    xs = _mlp(hf, w1, w2, x1, gf_post, gt_f, tm=512, tf=1024)
    c1, hcf = _out_project(a_ctx, b_ctx, cs, w_out, g_post, cgt_m, gf_pre, csc_f, csh_f, tm=nc)
    cs = _mlp(hcf, w1, w2, c1, gf_post, cgt_f, tm=nc, tf=512)

    mod = _modulation(cvecs, l1_mod_w, l1_mod_b)
    sh_m, sc_m, gt_m, sh_f, sc_f, gt_f = _split_mod(mod[0])
    csh_m, csc_m = _split_mod(mod[1])[:2]
    w_out = l1_w_out.astype(BF16)
    w1 = l1_mlp_w1.astype(BF16)
    w2 = l1_mlp_w2.astype(BF16)
    g_pre, g_post = _row(l1_norm_mix_pre), _row(l1_norm_mix_post)
    gf_pre, gf_post = _row(l1_norm_mlp_pre), _row(l1_norm_mlp_post)
    q0, k0, v0 = C_IN, C_IN + D_Q_WIDTH, C_IN + D_Q_WIDTH + D_KV_WIDTH
    w_in = jnp.concatenate([l1_w_in[:, :k0], _dup_head_cols(l1_w_in[:, k0:v0]), _dup_head_cols(l1_w_in[:, v0:])],
                           axis=1).astype(BF16)
    kd0, vd0 = k0, k0 + 2 * D_KV_WIDTH

    proj = _project(xs, g_pre, sc_m, sh_m, w_in, rope,
                    blocks((C_IN, PLAIN), (D_Q_WIDTH, (Q1_SCALE, 0.0)), (2 * D_KV_WIDTH, (1.0, 0.0)),
                           (2 * D_KV_WIDTH, PLAIN)), tm=1024, tn=1024)
    pc = _project(cs, g_pre, csc_m, csh_m, w_in[:, kd0:], rope, blocks((4 * D_KV_WIDTH, PLAIN)), tm=nc, tn=1024)
    c_lat = _short_conv(proj, l1_sconv_w, t=512)
    d_lat = _window_attention(l1_sink, proj, q0, kd0, vd0, pc, tq=256)
    x1, hf = _out_project(c_lat, d_lat, xs, w_out, g_post, gt_m, gf_pre, sc_f, sh_f, tm=512)
    xs = _mlp(hf, w1, w2, x1, gf_post, gt_f, tm=512, tf=1024)
    return xs[None]
```

```python
import functools
import math

import jax
import jax.numpy as jnp
from jax import lax
from jax.experimental import pallas as pl
from jax.experimental.pallas import tpu as pltpu

F32 = jnp.float32
BF16 = jnp.bfloat16

D_MODEL = 2048
HEAD_DIM = 64
HALF = HEAD_DIM // 2
GRID_W = 64
ROPE_BASE = 10000.0
NORM_EPS = 1e-6
LN_EPS = 1e-5
ATTN_SCALE = HEAD_DIM ** -0.5
LOG2E = math.log2(math.e)
NEG_INF = -1e30
N_MOD = 6
WINDOW = 128
LANES = 128
HALO = 16

A_WIDTH = 1024
A_IN = 2 * A_WIDTH
CONV_A_WIDTH = 31
B_HEADS = 8
B_WIDTH = 1024
C_WIDTH = 1024
C_IN = 3 * C_WIDTH
D_HEADS = 16
D_KV_HEADS = 4
D_GROUP = 4
D_Q_WIDTH = 1024
D_KV_WIDTH = 256
D_FF = 4 * D_MODEL

VMEM_LIMIT = 56 * 1024 * 1024


def _params(sem):
    return pltpu.CompilerParams(dimension_semantics=sem, vmem_limit_bytes=VMEM_LIMIT)


def _rms(x, eps=NORM_EPS):
    return x * lax.rsqrt(jnp.mean(x * x, axis=-1, keepdims=True) + eps)


def _dot_nt(a, b):
    return lax.dot_general(a, b, (((1,), (1,)), ((), ())), preferred_element_type=F32)


def _mod_kernel(c_ref, w_ref, b_ref, o_ref):
    c = c_ref[...]
    s = c * jax.nn.sigmoid(c)
    o_ref[...] = jnp.dot(s, w_ref[...], preferred_element_type=F32) + b_ref[...]


def _modulation(cvecs, w, b):
    tn = 1024
    n_out = w.shape[1]
    return pl.pallas_call(
        _mod_kernel,
        grid=(n_out // tn,),
        in_specs=[pl.BlockSpec((8, D_MODEL), lambda j: (0, 0)),
                  pl.BlockSpec((D_MODEL, tn), lambda j: (0, j)),
                  pl.BlockSpec((1, tn), lambda j: (0, j))],
        out_specs=pl.BlockSpec((8, tn), lambda j: (0, j)),
        out_shape=jax.ShapeDtypeStruct((8, n_out), F32),
        compiler_params=_params(("arbitrary",)),
        name="modulation",
    )(cvecs, w, b.reshape(1, n_out))


PROJ_SUB = 512


def _proj_kernel(coef_ref, x_ref, g_ref, sc_ref, sh_ref, tab_ref, w_ref, o_ref, h_scr, *, tn):
    nsub = tn // PROJ_SUB
    j = pl.program_id(1)

    @pl.when(j == 0)
    def _():
        y = _rms(x_ref[...]) * g_ref[...]
        h_scr[...] = (y * (1.0 + sc_ref[...]) + sh_ref[...]).astype(BF16)

    acc = jnp.dot(h_scr[...], w_ref[...], preferred_element_type=F32)
    cosf = tab_ref[0]
    sinf = tab_ref[1]
    lane = lax.broadcasted_iota(jnp.int32, cosf.shape, 1)
    first_half = (lane & (HEAD_DIM - 1)) < HALF
    for s in range(nsub):
        alpha = coef_ref[2 * (j * nsub + s)]
        beta = coef_ref[2 * (j * nsub + s) + 1]
        a = alpha * cosf + beta
        b = alpha * sinf
        for c in range(s * PROJ_SUB // LANES, (s + 1) * PROJ_SUB // LANES):
            y = acc[:, c * LANES:(c + 1) * LANES]
            swapped = jnp.where(first_half, pltpu.roll(y, LANES - HALF, 1), pltpu.roll(y, HALF, 1))
            o_ref[:, c * LANES:(c + 1) * LANES] = (y * a + swapped * b).astype(BF16)


def _project(x, g, scale, shift, w, rope_tab, coefs, *, tm, tn):
    n, d = x.shape
    p = w.shape[1]
    assert len(coefs) == p // PROJ_SUB
    vec = pl.BlockSpec((1, d), lambda i, j: (0, 0))
    return pl.pallas_call(
        functools.partial(_proj_kernel, tn=tn),
        grid=(n // tm, p // tn),
        in_specs=[pl.BlockSpec(memory_space=pltpu.SMEM),
                  pl.BlockSpec((tm, d), lambda i, j: (i, 0)), vec, vec, vec,
                  pl.BlockSpec((2, tm, LANES), lambda i, j: (0, i, 0)),
                  pl.BlockSpec((d, tn), lambda i, j: (0, j))],
        out_specs=pl.BlockSpec((tm, tn), lambda i, j: (i, j)),
        out_shape=jax.ShapeDtypeStruct((n, p), BF16),
        scratch_shapes=[pltpu.VMEM((tm, d), BF16)],
        compiler_params=_params(("parallel", "arbitrary")),
        name="norm_project",
    )(jnp.asarray(coefs, F32).reshape(-1), x, g, scale, shift, rope_tab, w)


CONV_A_ROWS = 32


def _conv_a_kernel(main_ref, left_ref, right_ref, w_ref, cb_ref, lg_ref, lb_ref, o_ref, ext_scr, sh_scr, *, t):
    i = pl.program_id(0)
    last = pl.num_programs(0) - 1

    def glu(ref):
        v = ref[...].astype(F32)
        return v[:, :A_WIDTH] * jax.nn.sigmoid(v[:, A_WIDTH:])

    ext_scr[0:HALO, :] = jnp.where(i > 0, glu(left_ref), 0.0)
    ext_scr[HALO:HALO + t, :] = glu(main_ref)
    ext_scr[HALO + t:, :] = jnp.where(i < last, glu(right_ref), 0.0)
    ext = ext_scr[...]
    rows = t + 2 * HALO
    srows = t + 3 * 8
    sh_scr[0] = ext[:srows]
    for b in range(1, 8):
        sh_scr[b] = pltpu.roll(ext, rows - b, 0)[:srows]

    def chunk(c, carry):
        r0 = pl.multiple_of(c * CONV_A_ROWS, CONV_A_ROWS)
        acc = jnp.zeros((CONV_A_ROWS, A_WIDTH), F32)
        for k in range(CONV_A_WIDTH):
            a, b = divmod(k + 1, 8)
            acc = acc + sh_scr[b, pl.ds(r0 + 8 * a, CONV_A_ROWS), :] * w_ref[k:k + 1, :]
        v = acc + cb_ref[...]
        mu = jnp.mean(v, axis=-1, keepdims=True)
        vc = v - mu
        var = jnp.mean(vc * vc, axis=-1, keepdims=True)
        y = vc * lax.rsqrt(var + LN_EPS) * lg_ref[...] + lb_ref[...]
        o_ref[pl.ds(r0, CONV_A_ROWS), :] = (y * jax.nn.sigmoid(y)).astype(BF16)
        return carry

    lax.fori_loop(0, t // CONV_A_ROWS, chunk, 0, unroll=2)


def _conformer_conv(proj, conv_w, conv_b, ln_g, ln_b, *, t):
    n = proj.shape[0]
    per = t // HALO
    nh = n // HALO
    vec = pl.BlockSpec((1, A_WIDTH), lambda i: (0, 0))
    return pl.pallas_call(
        functools.partial(_conv_a_kernel, t=t),
        grid=(n // t,),
        in_specs=[pl.BlockSpec((t, A_IN), lambda i: (i, 0)),
                  pl.BlockSpec((HALO, A_IN), lambda i: (jnp.maximum(i * per - 1, 0), 0)),
                  pl.BlockSpec((HALO, A_IN), lambda i: (jnp.minimum((i + 1) * per, nh - 1), 0)),
                  pl.BlockSpec((CONV_A_WIDTH, A_WIDTH), lambda i: (0, 0)),
                  vec, vec, vec],
        out_specs=pl.BlockSpec((t, A_WIDTH), lambda i: (i, 0)),
        out_shape=jax.ShapeDtypeStruct((n, A_WIDTH), BF16),
        scratch_shapes=[pltpu.VMEM((t + 2 * HALO, A_WIDTH), F32),
                        pltpu.VMEM((8, t + 24, A_WIDTH), F32)],
        compiler_params=_params(("parallel",)),
        name="conformer_conv",
    )(proj, proj, proj, conv_w, conv_b.reshape(1, -1), ln_g.reshape(1, -1), ln_b.reshape(1, -1))


def _tile_lanes(x, reps):
    return x if reps == 1 else jnp.concatenate([x] * reps, axis=1)


def _diff_attn_kernel(lam_ref, sub_ref, q_ref, k_ref, v_ref, o_ref, qm_scr, m_scr, acc_scr, *, lam_init, tk):
    q = q_ref[...]
    lane = lax.broadcasted_iota(jnp.int32, q.shape, 1)
    zero = jnp.zeros_like(q)
    qm_scr[0] = jnp.where(lane < HEAD_DIM, q, zero)
    qm_scr[1] = jnp.where(lane >= HEAD_DIM, q, zero)
    m_scr[...] = jnp.full(m_scr.shape, -jnp.inf, F32)
    acc_scr[...] = jnp.zeros(acc_scr.shape, F32)
    ones = jnp.ones((tk, LANES), BF16)

    def chunk(c, carry):
        r0 = pl.multiple_of(c * tk, tk)
        k = k_ref[pl.ds(r0, tk), :]
        v_ext = jnp.concatenate([v_ref[pl.ds(r0, tk), :], ones], axis=1)
        scores = [_dot_nt(qm_scr[m], k) for m in range(2)]
        for m in range(2):
            s = scores[m]
            m_prev = m_scr[m]
            m_new = jnp.maximum(m_prev, jnp.max(s, axis=1, keepdims=True))
            alpha = jnp.exp2(m_prev - m_new)
            p = jnp.exp2(s - _tile_lanes(m_new, tk // LANES))
            acc_scr[m] = (_tile_lanes(alpha, 2) * acc_scr[m]
                          + jnp.dot(p.astype(BF16), v_ext, preferred_element_type=F32))
            m_scr[m] = m_new
        return carry

    lax.fori_loop(0, k_ref.shape[0] // tk, chunk, 0)

    lp = lam_ref[...]
    lam = (jnp.exp(jnp.sum(lp[0:1] * lp[1:2], axis=1, keepdims=True))
           - jnp.exp(jnp.sum(lp[2:3] * lp[3:4], axis=1, keepdims=True)) + lam_init)
    o = (acc_scr[0, :, :LANES] / acc_scr[0, :, LANES:]
         - lam * (acc_scr[1, :, :LANES] / acc_scr[1, :, LANES:]))
    o = _rms(o) * sub_ref[...] * (1.0 - lam_init)
    o_ref[...] = o.astype(BF16)


def _diff_attention(lam_rows, subln_g, q_src, q_col0, k_all, v_all, *, tq, tk, lam_init):
    n = q_src.shape[0]
    nk = k_all.shape[0]
    qb = q_col0 // LANES
    return pl.pallas_call(
        functools.partial(_diff_attn_kernel, lam_init=lam_init, tk=tk),
        grid=(B_HEADS, n // tq),
        in_specs=[pl.BlockSpec((8, LANES), lambda h, i: (0, 0)),
                  pl.BlockSpec((1, LANES), lambda h, i: (0, 0)),
                  pl.BlockSpec((tq, LANES), lambda h, i: (i, qb + h)),
                  pl.BlockSpec((nk, LANES), lambda h, i: (0, h), pipeline_mode=pl.Buffered(1)),
                  pl.BlockSpec((nk, LANES), lambda h, i: (0, h), pipeline_mode=pl.Buffered(1))],
        out_specs=pl.BlockSpec((tq, LANES), lambda h, i: (i, h)),
        out_shape=jax.ShapeDtypeStruct((n, B_WIDTH), BF16),
        scratch_shapes=[pltpu.VMEM((2, tq, LANES), BF16),
                        pltpu.VMEM((2, tq, LANES), F32),
                        pltpu.VMEM((2, tq, 2 * LANES), F32)],
        compiler_params=_params(("parallel", "arbitrary")),
        name="diff_attention",
    )(lam_rows, subln_g.reshape(1, LANES), q_src, k_all, v_all)


def _conv_c_kernel(main_ref, left_ref, right_ref, w_ref, o_ref, ext_scr, *, t):
    i = pl.program_id(0)
    last = pl.num_programs(0) - 1

    def gated(ref):
        v = ref[...].astype(F32)
        return v[:, C_WIDTH:2 * C_WIDTH] * v[:, 2 * C_WIDTH:]

    ext_scr[0:HALO, :] = jnp.where(i > 0, gated(left_ref), 0.0)
    ext_scr[HALO:HALO + t, :] = gated(main_ref)
    ext_scr[HALO + t:, :] = jnp.where(i < last, gated(right_ref), 0.0)
    ext = ext_scr[...]
    rows = t + 2 * HALO
    prev = pltpu.roll(ext, 1, 0)[HALO:HALO + t]
    nxt = pltpu.roll(ext, rows - 1, 0)[HALO:HALO + t]
    conv = prev * w_ref[0:1, :] + ext[HALO:HALO + t] * w_ref[1:2, :] + nxt * w_ref[2:3, :]
    o_ref[...] = (main_ref[:, :C_WIDTH].astype(F32) * conv).astype(BF16)


def _short_conv(proj, w, *, t):
    n = proj.shape[0]
    per = t // HALO
    nh = n // HALO
    return pl.pallas_call(
        functools.partial(_conv_c_kernel, t=t),
        grid=(n // t,),
        in_specs=[pl.BlockSpec((t, C_IN), lambda i: (i, 0)),
                  pl.BlockSpec((HALO, C_IN), lambda i: (jnp.maximum(i * per - 1, 0), 0)),
                  pl.BlockSpec((HALO, C_IN), lambda i: (jnp.minimum((i + 1) * per, nh - 1), 0)),
                  pl.BlockSpec((3, C_WIDTH), lambda i: (0, 0))],
        out_specs=pl.BlockSpec((t, C_WIDTH), lambda i: (i, 0)),
        out_shape=jax.ShapeDtypeStruct((n, C_WIDTH), BF16),
        scratch_shapes=[pltpu.VMEM((t + 2 * HALO, C_WIDTH), F32)],
        compiler_params=_params(("parallel",)),
        name="short_conv",
    )(proj, proj, proj, w)


def _win_attn_kernel(sink_ref, q_ref, kp_ref, km_ref, kn_ref, vp_ref, vm_ref, vn_ref, c_ref, o_ref, *, tq, n):
    i = pl.program_id(0)
    lane = lax.broadcasted_iota(jnp.int32, (tq, LANES), 1)
    low = lane < HEAD_DIM

    def with_ones(v, keep_low):
        col = lax.broadcasted_iota(jnp.int32, v.shape, 1)
        return jnp.where((col < HEAD_DIM) if keep_low else (col >= HEAD_DIM), v, jnp.ones_like(v))

    nb = tq // WINDOW
    span = 3 * WINDOW
    row = lax.broadcasted_iota(jnp.int32, (WINDOW, span), 0)
    col = lax.broadcasted_iota(jnp.int32, (WINDOW, span), 1)
    band = (col >= row) & (col <= row + 2 * WINDOW)
    masks = []
    for blk in range(nb):
        pos = i * tq + (blk - 1) * WINDOW + col
        masks.append(band & (pos >= 0) & (pos < n))
    zeros = jnp.zeros((WINDOW, WINDOW), BF16)

    def scores(kv):
        tile = slice(kv * LANES, (kv + 1) * LANES)
        kw = jnp.concatenate([kp_ref[:, tile], km_ref[:, tile], kn_ref[:, tile]], axis=0)
        kc = c_ref[:, tile]
        qms = []
        for g in range(D_GROUP):
            c0 = (kv * D_GROUP + g) // 2 * LANES
            qa = q_ref[:, c0:c0 + LANES]
            qms.append(jnp.where(low if g % 2 == 0 else (lane >= HEAD_DIM), qa, jnp.zeros_like(qa)))
        q_all = jnp.concatenate(qms, axis=0)
        return _dot_nt(q_all, kc), _dot_nt(q_all, kw)

    def softmax(kv, s_c_all, s_w_all):
        probs = []
        for g in range(D_GROUP):
            sk = sink_ref[kv * D_GROUP + g] * LOG2E
            p_cs, p_ws, sinks = [], [], []
            for blk in range(nb):
                rows = slice(g * tq + blk * WINDOW, g * tq + (blk + 1) * WINDOW)
                s_c = s_c_all[rows]
                s_w = jnp.where(masks[blk], s_w_all[rows, blk * WINDOW:blk * WINDOW + span], NEG_INF)
                mx = jnp.maximum(jnp.maximum(jnp.max(s_c, axis=1, keepdims=True),
                                             jnp.max(s_w, axis=1, keepdims=True)), sk)
                p_cs.append(jnp.exp2(s_c - mx).astype(BF16))
                p_ws.append(jnp.concatenate([zeros] * blk + [jnp.exp2(s_w - mx).astype(BF16)]
                                            + [zeros] * (nb - 1 - blk), axis=1))
                sinks.append(jnp.exp2(sk - mx))
            probs.append((jnp.concatenate(p_cs, axis=0), jnp.concatenate(p_ws, axis=0),
                          jnp.concatenate(sinks, axis=0)))
        return probs

    def outputs(kv, probs):
        tile = slice(kv * LANES, (kv + 1) * LANES)
        vw = jnp.concatenate([vp_ref[:, tile], vm_ref[:, tile], vn_ref[:, tile]], axis=0)
        vc = c_ref[:, D_KV_HEADS * LANES + kv * LANES:D_KV_HEADS * LANES + (kv + 1) * LANES]
        outs = [None] * D_GROUP
        for parity in range(2):
            heads = list(range(parity, D_GROUP, 2))
            p_c = jnp.concatenate([probs[g][0] for g in heads], axis=0)
            p_w = jnp.concatenate([probs[g][1] for g in heads], axis=0)
            p_sink = jnp.concatenate([probs[g][2] for g in heads], axis=0)
            o = (jnp.dot(p_c, with_ones(vc, parity == 0), preferred_element_type=F32)
                 + jnp.dot(p_w, with_ones(vw, parity == 0), preferred_element_type=F32))
            o = o / (pltpu.roll(o, HEAD_DIM, 1) + p_sink)
            for idx, g in enumerate(heads):
                outs[g] = o[idx * tq:(idx + 1) * tq]
        for a in range(D_GROUP // 2):
            c0 = (kv * D_GROUP // 2 + a) * LANES
            o_ref[:, c0:c0 + LANES] = jnp.where(low, outs[2 * a], outs[2 * a + 1]).astype(BF16)

    for kv in range(D_KV_HEADS):
        outputs(kv, softmax(kv, *scores(kv)))


def _window_attention(sink, proj, q_col0, k_col0, v_col0, pc, *, tq):
    n = proj.shape[0]
    nc = pc.shape[0]
    kvw = D_KV_HEADS * LANES
    per = tq // WINDOW
    nw = n // WINDOW

    def window(col0):
        cb = col0 // kvw
        return [pl.BlockSpec((WINDOW, kvw), lambda i: (jnp.maximum(i * per - 1, 0), cb)),
                pl.BlockSpec((tq, kvw), lambda i: (i, cb)),
                pl.BlockSpec((WINDOW, kvw), lambda i: (jnp.minimum((i + 1) * per, nw - 1), cb))]

    return pl.pallas_call(
        functools.partial(_win_attn_kernel, tq=tq, n=n),
        grid=(n // tq,),
        in_specs=[pl.BlockSpec(memory_space=pltpu.SMEM),
                  pl.BlockSpec((tq, D_Q_WIDTH), lambda i: (i, q_col0 // D_Q_WIDTH))]
                 + window(k_col0) + window(v_col0)
                 + [pl.BlockSpec((nc, 2 * kvw), lambda i: (0, 0))],
        out_specs=pl.BlockSpec((tq, D_Q_WIDTH), lambda i: (i, 0)),
        out_shape=jax.ShapeDtypeStruct((n, D_Q_WIDTH), BF16),
        compiler_params=_params(("parallel",)),
        name="window_attention",
    )(sink, proj, proj, proj, proj, proj, proj, proj, pc)


def _dup_head_cols(w):
    d = w.shape[0]
    w = w.reshape(d, D_KV_HEADS, 1, HEAD_DIM)
    return jnp.broadcast_to(w, (d, D_KV_HEADS, 2, HEAD_DIM)).reshape(d, D_KV_HEADS * LANES)


OUT_ROWS = 256


def _out_kernel(a_ref, b_ref, x_ref, w_ref, gpost_ref, gate_ref, gpre_ref, sc_ref, sh_ref, x1_ref, hf_ref):
    half = a_ref.shape[1]
    for r in range(x_ref.shape[0] // OUT_ROWS):
        rows = slice(r * OUT_ROWS, (r + 1) * OUT_ROWS)
        y = (jnp.dot(a_ref[rows, :], w_ref[:half, :], preferred_element_type=F32)
             + jnp.dot(b_ref[rows, :], w_ref[half:, :], preferred_element_type=F32))
        x1 = x_ref[rows, :] + gate_ref[...] * (_rms(y) * gpost_ref[...])
        x1_ref[rows, :] = x1
        hf_ref[rows, :] = ((_rms(x1) * gpre_ref[...]) * (1.0 + sc_ref[...]) + sh_ref[...]).astype(BF16)


def _out_project(a, b, x, w, g_post, gate, g_pre, scale, shift, *, tm):
    n, d = x.shape
    half = a.shape[1]
    vec = pl.BlockSpec((1, d), lambda i: (0, 0))
    return pl.pallas_call(
        _out_kernel,
        grid=(n // tm,),
        in_specs=[pl.BlockSpec((tm, half), lambda i: (i, 0)),
                  pl.BlockSpec((tm, half), lambda i: (i, 0)),
                  pl.BlockSpec((tm, d), lambda i: (i, 0)),
                  pl.BlockSpec((2 * half, d), lambda i: (0, 0), pipeline_mode=pl.Buffered(1)),
                  vec, vec, vec, vec, vec],
        out_specs=[pl.BlockSpec((tm, d), lambda i: (i, 0)), pl.BlockSpec((tm, d), lambda i: (i, 0))],
        out_shape=[jax.ShapeDtypeStruct((n, d), F32), jax.ShapeDtypeStruct((n, d), BF16)],
        compiler_params=_params(("parallel",)),
        name="out_project",
    )(a, b, x, w, g_post, gate, g_pre, scale, shift)


def _mlp_kernel(hf_ref, w1_ref, w2_ref, x1_ref, gpost_ref, gate_ref, o_ref, acc_scr):
    f = pl.program_id(1)

    @pl.when(f == 0)
    def _():
        acc_scr[...] = jnp.zeros(acc_scr.shape, F32)

    h = jnp.dot(hf_ref[...], w1_ref[...], preferred_element_type=F32)
    h = jnp.square(jnp.maximum(h, 0.0)).astype(BF16)
    acc_scr[...] += jnp.dot(h, w2_ref[...], preferred_element_type=F32)

    @pl.when(f == pl.num_programs(1) - 1)
    def _():
        o_ref[...] = x1_ref[...] + gate_ref[...] * (_rms(acc_scr[...]) * gpost_ref[...])


def _mlp(hf, w1, w2, x1, g_post, gate, *, tm, tf):
    n, d = x1.shape
    ff = w1.shape[1]
    vec = pl.BlockSpec((1, d), lambda i, f: (0, 0))
    return pl.pallas_call(
        _mlp_kernel,
        grid=(n // tm, ff // tf),
        in_specs=[pl.BlockSpec((tm, d), lambda i, f: (i, 0)),
                  pl.BlockSpec((d, tf), lambda i, f: (0, f)),
                  pl.BlockSpec((tf, d), lambda i, f: (f, 0)),
                  pl.BlockSpec((tm, d), lambda i, f: (i, 0)),
                  vec, vec],
        out_specs=pl.BlockSpec((tm, d), lambda i, f: (i, 0)),
        out_shape=jax.ShapeDtypeStruct((n, d), F32),
        scratch_shapes=[pltpu.VMEM((tm, d), F32)],
        compiler_params=_params(("parallel", "arbitrary")),
        name="mlp",
    )(hf, w1, w2, x1, g_post, gate)


Q0_SCALE = ATTN_SCALE * LOG2E
Q1_SCALE = ATTN_SCALE * LOG2E
PLAIN = (0.0, 1.0)


def _rope_table(n):
    rows = n // GRID_W
    pairs = HEAD_DIM // 4
    inv = jnp.power(ROPE_BASE, -jnp.arange(pairs, dtype=F32) / pairs)
    ang_r = jnp.arange(rows, dtype=F32)[:, None] * inv
    ang_c = jnp.arange(GRID_W, dtype=F32)[:, None] * inv

    def grid(f):
        r = jnp.broadcast_to(f(ang_r)[:, None, :], (rows, GRID_W, pairs))
        c = jnp.broadcast_to(f(ang_c)[None, :, :], (rows, GRID_W, pairs))
        return jnp.concatenate([r, c], axis=-1).reshape(n, 2 * pairs)

    cos = grid(lambda a: lax.optimization_barrier(jnp.cos(a)))
    sin = grid(lambda a: lax.optimization_barrier(jnp.sin(a)))
    return jnp.stack([jnp.tile(cos, (1, 4)), jnp.tile(jnp.concatenate([-sin, sin], axis=-1), (1, 2))])


def _row(v):
    return v.reshape(1, -1)


def _split_mod(m):
    return [_row(t) for t in jnp.split(m, N_MOD)]


def kernel(x, c, ctx, c_ctx, l0_mod_w, l0_mod_b, l0_norm_mix_pre, l0_norm_mix_post, l0_norm_mlp_pre, l0_norm_mlp_post, l0_w_in, l0_conv_w, l0_conv_b, l0_ln_g, l0_ln_b, l0_lambda_q1, l0_lambda_k1, l0_lambda_q2, l0_lambda_k2, l0_subln_g, l0_w_out, l0_mlp_w1, l0_mlp_w2, l1_mod_w, l1_mod_b, l1_norm_mix_pre, l1_norm_mix_post, l1_norm_mlp_pre, l1_norm_mlp_post, l1_w_in, l1_sconv_w, l1_sink, l1_w_out, l1_mlp_w1, l1_mlp_w2):
    n = x.shape[1]
    nc = ctx.shape[1]
    xs = x[0]
    cs = ctx[0]
    rope = _rope_table(n)
    cvecs = jnp.zeros((8, D_MODEL), F32).at[0].set(c[0]).at[1].set(c_ctx)

    mod = _modulation(cvecs, l0_mod_w, l0_mod_b)
    sh_m, sc_m, gt_m, sh_f, sc_f, gt_f = _split_mod(mod[0])
    csh_m, csc_m, cgt_m, csh_f, csc_f, cgt_f = _split_mod(mod[1])
    w_in = l0_w_in.astype(BF16)
    w_out = l0_w_out.astype(BF16)
    w1 = l0_mlp_w1.astype(BF16)
    w2 = l0_mlp_w2.astype(BF16)
    g_pre, g_post = _row(l0_norm_mix_pre), _row(l0_norm_mix_post)
    gf_pre, gf_post = _row(l0_norm_mlp_pre), _row(l0_norm_mlp_post)
    q0, k0, v0 = A_IN, A_IN + B_WIDTH, A_IN + 2 * B_WIDTH

    blocks = lambda *widths_coefs: [c for w, c in widths_coefs for _ in range(w // PROJ_SUB)]
    proj = _project(xs, g_pre, sc_m, sh_m, w_in, rope,
                    blocks((A_IN, PLAIN), (B_WIDTH, (Q0_SCALE, 0.0)), (B_WIDTH, (1.0, 0.0)), (B_WIDTH, PLAIN)),
                    tm=512, tn=2560)
    pc = _project(cs, g_pre, csc_m, csh_m, w_in, rope,
                  blocks((A_IN, PLAIN), (B_WIDTH, (0.0, Q0_SCALE)), (2 * B_WIDTH, PLAIN)), tm=nc, tn=1024)
    lam_init = 0.8 - 0.6 * math.exp(-0.3 * 0)
    lam_rows = jnp.zeros((8, LANES), F32)
    for r, lv in enumerate((l0_lambda_q1, l0_lambda_k1, l0_lambda_q2, l0_lambda_k2)):
        lam_rows = lam_rows.at[r, :HEAD_DIM].set(lv)

    a_lat = _conformer_conv(proj, l0_conv_w, l0_conv_b, l0_ln_g, l0_ln_b, t=512)
    a_ctx = _conformer_conv(pc, l0_conv_w, l0_conv_b, l0_ln_g, l0_ln_b, t=nc)
    k_c, v_c = pc[:, k0:v0], pc[:, v0:]
    k_all = jnp.concatenate([k_c, proj[:, k0:v0]], axis=0)
    v_all = jnp.concatenate([v_c, proj[:, v0:]], axis=0)
    b_lat = _diff_attention(lam_rows, l0_subln_g, proj, q0, k_all, v_all, tq=1024, tk=3328, lam_init=lam_init)
    b_ctx = _diff_attention(lam_rows, l0_subln_g, pc, q0, k_c, v_c, tq=nc, tk=nc, lam_init=lam_init)

    x1, hf = _out_project(a_lat, b_lat, xs, w_out, g_post, gt_m, gf_pre, sc_f, sh_f, tm=512)
    xs = _mlp(hf, w1, w2, x1, gf_post, gt_f, tm=512, tf=1024)
    c1, hcf = _out_project(a_ctx, b_ctx, cs, w_out, g_post, cgt_m, gf_pre, csc_f, csh_f, tm=nc)
    cs = _mlp(hcf, w1, w2, c1, gf_post, cgt_f, tm=nc, tf=512)

    mod = _modulation(cvecs, l1_mod_w, l1_mod_b)
    sh_m, sc_m, gt_m, sh_f, sc_f, gt_f = _split_mod(mod[0])
    csh_m, csc_m = _split_mod(mod[1])[:2]
    w_out = l1_w_out.astype(BF16)
    w1 = l1_mlp_w1.astype(BF16)
    w2 = l1_mlp_w2.astype(BF16)
    g_pre, g_post = _row(l1_norm_mix_pre), _row(l1_norm_mix_post)
    gf_pre, gf_post = _row(l1_norm_mlp_pre), _row(l1_norm_mlp_post)
    q0, k0, v0 = C_IN, C_IN + D_Q_WIDTH, C_IN + D_Q_WIDTH + D_KV_WIDTH
    w_in = jnp.concatenate([l1_w_in[:, :k0], _dup_head_cols(l1_w_in[:, k0:v0]), _dup_head_cols(l1_w_in[:, v0:])],
                           axis=1).astype(BF16)
    kd0, vd0 = k0, k0 + 2 * D_KV_WIDTH

    proj = _project(xs, g_pre, sc_m, sh_m, w_in, rope,
                    blocks((C_IN, PLAIN), (D_Q_WIDTH, (Q1_SCALE, 0.0)), (2 * D_KV_WIDTH, (1.0, 0.0)),
                           (2 * D_KV_WIDTH, PLAIN)), tm=512, tn=2560)
    pc = _project(cs, g_pre, csc_m, csh_m, w_in[:, kd0:], rope, blocks((4 * D_KV_WIDTH, PLAIN)), tm=nc, tn=1024)
    c_lat = _short_conv(proj, l1_sconv_w, t=512)
    d_lat = _window_attention(l1_sink, proj, q0, kd0, vd0, pc, tq=256)
    x1, hf = _out_project(c_lat, d_lat, xs, w_out, g_post, gt_m, gf_pre, sc_f, sh_f, tm=512)
    xs = _mlp(hf, w1, w2, x1, gf_post, gt_f, tm=512, tf=1024)
    return xs[None]
```

```python
import functools
import math

import jax
import jax.numpy as jnp
from jax import lax
from jax.experimental import pallas as pl
from jax.experimental.pallas import tpu as pltpu

F32 = jnp.float32
BF16 = jnp.bfloat16

D_MODEL = 2048
HEAD_DIM = 64
HALF = HEAD_DIM // 2
GRID_W = 64
ROPE_BASE = 10000.0
NORM_EPS = 1e-6
LN_EPS = 1e-5
ATTN_SCALE = HEAD_DIM ** -0.5
LOG2E = math.log2(math.e)
NEG_INF = -1e30
N_MOD = 6
WINDOW = 128
LANES = 128
HALO = 16

A_WIDTH = 1024
A_IN = 2 * A_WIDTH
CONV_A_WIDTH = 31
B_HEADS = 8
B_WIDTH = 1024
C_WIDTH = 1024
C_IN = 3 * C_WIDTH
D_HEADS = 16
D_KV_HEADS = 4
D_GROUP = 4
D_Q_WIDTH = 1024
D_KV_WIDTH = 256
D_FF = 4 * D_MODEL

VMEM_LIMIT = 56 * 1024 * 1024


def _params(sem):
    return pltpu.CompilerParams(dimension_semantics=sem, vmem_limit_bytes=VMEM_LIMIT)


def _rms(x, eps=NORM_EPS):
    return x * lax.rsqrt(jnp.mean(x * x, axis=-1, keepdims=True) + eps)


def _dot_nt(a, b):
    return lax.dot_general(a, b, (((1,), (1,)), ((), ())), preferred_element_type=F32)


def _mod_kernel(c_ref, w_ref, b_ref, o_ref):
    c = c_ref[...]
    s = c * jax.nn.sigmoid(c)
    o_ref[...] = jnp.dot(s, w_ref[...], preferred_element_type=F32) + b_ref[...]


def _modulation(cvecs, w, b):
    tn = 1024
    n_out = w.shape[1]
    return pl.pallas_call(
        _mod_kernel,
        grid=(n_out // tn,),
        in_specs=[pl.BlockSpec((8, D_MODEL), lambda j: (0, 0)),
                  pl.BlockSpec((D_MODEL, tn), lambda j: (0, j)),
                  pl.BlockSpec((1, tn), lambda j: (0, j))],
        out_specs=pl.BlockSpec((8, tn), lambda j: (0, j)),
        out_shape=jax.ShapeDtypeStruct((8, n_out), F32),
        compiler_params=_params(("arbitrary",)),
        name="modulation",
    )(cvecs, w, b.reshape(1, n_out))


PROJ_SUB = 512
PROJ_ROWS = 256


def _proj_kernel(coef_ref, x_ref, g_ref, sc_ref, sh_ref, tab_ref, w_ref, *rest, tn, fill):
    outs, h_scr = (rest[1:3], rest[3]) if fill else (rest[:1], rest[1])
    nsub = tn // PROJ_SUB
    j = pl.program_id(1)

    @pl.when(j == 0)
    def _():
        y = _rms(x_ref[...]) * g_ref[...]
        h_scr[...] = (y * (1.0 + sc_ref[...]) + sh_ref[...]).astype(BF16)

    lane = lax.broadcasted_iota(jnp.int32, (PROJ_ROWS, LANES), 1)
    first_half = (lane & (HEAD_DIM - 1)) < HALF
    for r in range(x_ref.shape[0] // PROJ_ROWS):
        rows = slice(r * PROJ_ROWS, (r + 1) * PROJ_ROWS)
        acc = jnp.dot(h_scr[rows, :], w_ref[...], preferred_element_type=F32)
        cosf = tab_ref[0, rows, :]
        sinf = tab_ref[1, rows, :]
        for s in range(nsub):
            alpha = coef_ref[2 * (j * nsub + s)]
            beta = coef_ref[2 * (j * nsub + s) + 1]
            a = alpha * cosf + beta
            b = alpha * sinf
            for c in range(s * PROJ_SUB // LANES, (s + 1) * PROJ_SUB // LANES):
                y = acc[:, c * LANES:(c + 1) * LANES]
                swapped = jnp.where(first_half, pltpu.roll(y, LANES - HALF, 1), pltpu.roll(y, HALF, 1))
                res = (y * a + swapped * b).astype(BF16)
                for o_ref in outs:
                    o_ref[rows, c * LANES:(c + 1) * LANES] = res


def _project(x, g, scale, shift, w, rope_tab, coefs, *, tm, tn, out_rows=None, fill=None):
    n, d = x.shape
    p = w.shape[1]
    assert len(coefs) == p // PROJ_SUB
    vec = pl.BlockSpec((1, d), lambda i, j: (0, 0))
    in_specs = [pl.BlockSpec(memory_space=pltpu.SMEM),
                pl.BlockSpec((tm, d), lambda i, j: (i, 0)), vec, vec, vec,
                pl.BlockSpec((2, tm, LANES), lambda i, j: (0, i, 0)),
                pl.BlockSpec((d, tn), lambda i, j: (0, j))]
    args = [jnp.asarray(coefs, F32).reshape(-1), x, g, scale, shift, rope_tab, w]
    out_specs = [pl.BlockSpec((tm, tn), lambda i, j: (i, j))]
    out_shape = [jax.ShapeDtypeStruct((out_rows or n, p), BF16)]
    aliases = {}
    if fill is not None:
        buf, row0 = fill
        blk0 = row0 // tm
        in_specs.append(pl.BlockSpec(memory_space=pl.ANY))
        args.append(buf)
        out_specs.append(pl.BlockSpec((tm, tn), lambda i, j: (blk0 + i, j)))
        out_shape.append(jax.ShapeDtypeStruct(buf.shape, BF16))
        aliases = {len(args) - 1: 1}
    res = pl.pallas_call(
        functools.partial(_proj_kernel, tn=tn, fill=fill is not None),
        grid=(n // tm, p // tn),
        in_specs=in_specs,
        out_specs=out_specs,
        out_shape=out_shape,
        input_output_aliases=aliases,
        scratch_shapes=[pltpu.VMEM((tm, d), BF16)],
        compiler_params=_params(("parallel", "arbitrary")),
        name="norm_project",
    )(*args)
    return res if fill is not None else res[0]


CONV_A_ROWS = 32


def _conv_a_kernel(main_ref, left_ref, right_ref, w_ref, cb_ref, lg_ref, lb_ref, o_ref, ext_scr, sh_scr, *, t):
    i = pl.program_id(0)
    last = pl.num_programs(0) - 1

    def glu(ref):
        v = ref[...].astype(F32)
        return v[:, :A_WIDTH] * jax.nn.sigmoid(v[:, A_WIDTH:])

    ext_scr[0:HALO, :] = jnp.where(i > 0, glu(left_ref), 0.0)
    ext_scr[HALO:HALO + t, :] = glu(main_ref)
    ext_scr[HALO + t:, :] = jnp.where(i < last, glu(right_ref), 0.0)
    ext = ext_scr[...]
    rows = t + 2 * HALO
    srows = t + 3 * 8
    sh_scr[0] = ext[:srows]
    for b in range(1, 8):
        sh_scr[b] = pltpu.roll(ext, rows - b, 0)[:srows]

    def chunk(c, carry):
        r0 = pl.multiple_of(c * CONV_A_ROWS, CONV_A_ROWS)
        acc = jnp.zeros((CONV_A_ROWS, A_WIDTH), F32)
        for k in range(CONV_A_WIDTH):
            a, b = divmod(k + 1, 8)
            acc = acc + sh_scr[b, pl.ds(r0 + 8 * a, CONV_A_ROWS), :] * w_ref[k:k + 1, :]
        v = acc + cb_ref[...]
        mu = jnp.mean(v, axis=-1, keepdims=True)
        vc = v - mu
        var = jnp.mean(vc * vc, axis=-1, keepdims=True)
        y = vc * lax.rsqrt(var + LN_EPS) * lg_ref[...] + lb_ref[...]
        o_ref[pl.ds(r0, CONV_A_ROWS), :] = (y * jax.nn.sigmoid(y)).astype(BF16)
        return carry

    lax.fori_loop(0, t // CONV_A_ROWS, chunk, 0, unroll=2)


def _conformer_conv(proj, conv_w, conv_b, ln_g, ln_b, *, n, t):
    per = t // HALO
    nh = n // HALO
    vec = pl.BlockSpec((1, A_WIDTH), lambda i: (0, 0))
    return pl.pallas_call(
        functools.partial(_conv_a_kernel, t=t),
        grid=(n // t,),
        in_specs=[pl.BlockSpec((t, A_IN), lambda i: (i, 0)),
                  pl.BlockSpec((HALO, A_IN), lambda i: (jnp.maximum(i * per - 1, 0), 0)),
                  pl.BlockSpec((HALO, A_IN), lambda i: (jnp.minimum((i + 1) * per, nh - 1), 0)),
                  pl.BlockSpec((CONV_A_WIDTH, A_WIDTH), lambda i: (0, 0)),
                  vec, vec, vec],
        out_specs=pl.BlockSpec((t, A_WIDTH), lambda i: (i, 0)),
        out_shape=jax.ShapeDtypeStruct((n, A_WIDTH), BF16),
        scratch_shapes=[pltpu.VMEM((t + 2 * HALO, A_WIDTH), F32),
                        pltpu.VMEM((8, t + 24, A_WIDTH), F32)],
        compiler_params=_params(("parallel",)),
        name="conformer_conv",
    )(proj, proj, proj, conv_w, conv_b.reshape(1, -1), ln_g.reshape(1, -1), ln_b.reshape(1, -1))


def _tile_lanes(x, reps):
    return x if reps == 1 else jnp.concatenate([x] * reps, axis=1)


def _diff_attn_kernel(lam_ref, sub_ref, q_ref, k_ref, v_ref, o_ref, qm_scr, m_scr, acc_scr, *, lam_init, tk):
    q = q_ref[...]
    lane = lax.broadcasted_iota(jnp.int32, q.shape, 1)
    zero = jnp.zeros_like(q)
    qm_scr[0] = jnp.where(lane < HEAD_DIM, q, zero)
    qm_scr[1] = jnp.where(lane >= HEAD_DIM, q, zero)
    m_scr[...] = jnp.full(m_scr.shape, -jnp.inf, F32)
    acc_scr[...] = jnp.zeros(acc_scr.shape, F32)
    ones = jnp.ones((tk, LANES), BF16)

    def chunk(c, carry):
        r0 = pl.multiple_of(c * tk, tk)
        k = k_ref[pl.ds(r0, tk), :]
        v_ext = jnp.concatenate([v_ref[pl.ds(r0, tk), :], ones], axis=1)
        scores = [_dot_nt(qm_scr[m], k) for m in range(2)]
        for m in range(2):
            s = scores[m]
            m_prev = m_scr[m]
            m_new = jnp.maximum(m_prev, jnp.max(s, axis=1, keepdims=True))
            alpha = jnp.exp2(m_prev - m_new)
            p = jnp.exp2(s - _tile_lanes(m_new, tk // LANES))
            acc_scr[m] = (_tile_lanes(alpha, 2) * acc_scr[m]
                          + jnp.dot(p.astype(BF16), v_ext, preferred_element_type=F32))
            m_scr[m] = m_new
        return carry

    lax.fori_loop(0, k_ref.shape[0] // tk, chunk, 0)

    lp = lam_ref[...]
    lam = (jnp.exp(jnp.sum(lp[0:1] * lp[1:2], axis=1, keepdims=True))
           - jnp.exp(jnp.sum(lp[2:3] * lp[3:4], axis=1, keepdims=True)) + lam_init)
    o = (acc_scr[0, :, :LANES] / acc_scr[0, :, LANES:]
         - lam * (acc_scr[1, :, :LANES] / acc_scr[1, :, LANES:]))
    o = _rms(o) * sub_ref[...] * (1.0 - lam_init)
    o_ref[...] = o.astype(BF16)


def _diff_attention(lam_rows, subln_g, q_src, q_col0, kv_src, k_col0, v_col0, *, n_q, tq, tk, lam_init):
    nk = kv_src.shape[0]
    qb, kb, vb = q_col0 // LANES, k_col0 // LANES, v_col0 // LANES
    return pl.pallas_call(
        functools.partial(_diff_attn_kernel, lam_init=lam_init, tk=tk),
        grid=(B_HEADS, n_q // tq),
        in_specs=[pl.BlockSpec((8, LANES), lambda h, i: (0, 0)),
                  pl.BlockSpec((1, LANES), lambda h, i: (0, 0)),
                  pl.BlockSpec((tq, LANES), lambda h, i: (i, qb + h)),
                  pl.BlockSpec((nk, LANES), lambda h, i: (0, kb + h), pipeline_mode=pl.Buffered(1)),
                  pl.BlockSpec((nk, LANES), lambda h, i: (0, vb + h), pipeline_mode=pl.Buffered(1))],
        out_specs=pl.BlockSpec((tq, LANES), lambda h, i: (i, h)),
        out_shape=jax.ShapeDtypeStruct((n_q, B_WIDTH), BF16),
        scratch_shapes=[pltpu.VMEM((2, tq, LANES), BF16),
                        pltpu.VMEM((2, tq, LANES), F32),
                        pltpu.VMEM((2, tq, 2 * LANES), F32)],
        compiler_params=_params(("parallel", "arbitrary")),
        name="diff_attention",
    )(lam_rows, subln_g.reshape(1, LANES), q_src, kv_src, kv_src)


def _conv_c_kernel(main_ref, left_ref, right_ref, w_ref, o_ref, ext_scr, *, t):
    i = pl.program_id(0)
    last = pl.num_programs(0) - 1

    def gated(ref):
        v = ref[...].astype(F32)
        return v[:, C_WIDTH:2 * C_WIDTH] * v[:, 2 * C_WIDTH:]

    ext_scr[0:HALO, :] = jnp.where(i > 0, gated(left_ref), 0.0)
    ext_scr[HALO:HALO + t, :] = gated(main_ref)
    ext_scr[HALO + t:, :] = jnp.where(i < last, gated(right_ref), 0.0)
    ext = ext_scr[...]
    rows = t + 2 * HALO
    prev = pltpu.roll(ext, 1, 0)[HALO:HALO + t]
    nxt = pltpu.roll(ext, rows - 1, 0)[HALO:HALO + t]
    conv = prev * w_ref[0:1, :] + ext[HALO:HALO + t] * w_ref[1:2, :] + nxt * w_ref[2:3, :]
    o_ref[...] = (main_ref[:, :C_WIDTH].astype(F32) * conv).astype(BF16)


def _short_conv(proj, w, *, t):
    n = proj.shape[0]
    per = t // HALO
    nh = n // HALO
    return pl.pallas_call(
        functools.partial(_conv_c_kernel, t=t),
        grid=(n // t,),
        in_specs=[pl.BlockSpec((t, C_IN), lambda i: (i, 0)),
                  pl.BlockSpec((HALO, C_IN), lambda i: (jnp.maximum(i * per - 1, 0), 0)),
                  pl.BlockSpec((HALO, C_IN), lambda i: (jnp.minimum((i + 1) * per, nh - 1), 0)),
                  pl.BlockSpec((3, C_WIDTH), lambda i: (0, 0))],
        out_specs=pl.BlockSpec((t, C_WIDTH), lambda i: (i, 0)),
        out_shape=jax.ShapeDtypeStruct((n, C_WIDTH), BF16),
        scratch_shapes=[pltpu.VMEM((t + 2 * HALO, C_WIDTH), F32)],
        compiler_params=_params(("parallel",)),
        name="short_conv",
    )(proj, proj, proj, w)


def _win_attn_kernel(sink_ref, q_ref, kp_ref, km_ref, kn_ref, vp_ref, vm_ref, vn_ref, c_ref, o_ref, *, tq, n):
    i = pl.program_id(0)
    lane = lax.broadcasted_iota(jnp.int32, (tq, LANES), 1)
    low = lane < HEAD_DIM

    def with_ones(v, keep_low):
        col = lax.broadcasted_iota(jnp.int32, v.shape, 1)
        return jnp.where((col < HEAD_DIM) if keep_low else (col >= HEAD_DIM), v, jnp.ones_like(v))

    nb = tq // WINDOW
    span = 3 * WINDOW
    row = lax.broadcasted_iota(jnp.int32, (WINDOW, span), 0)
    col = lax.broadcasted_iota(jnp.int32, (WINDOW, span), 1)
    band = (col >= row) & (col <= row + 2 * WINDOW)
    masks = []
    for blk in range(nb):
        pos = i * tq + (blk - 1) * WINDOW + col
        masks.append(band & (pos >= 0) & (pos < n))
    zeros = jnp.zeros((WINDOW, WINDOW), BF16)

    def scores(kv):
        tile = slice(kv * LANES, (kv + 1) * LANES)
        kw = jnp.concatenate([kp_ref[:, tile], km_ref[:, tile], kn_ref[:, tile]], axis=0)
        kc = c_ref[:, tile]
        qms = []
        for g in range(D_GROUP):
            c0 = (kv * D_GROUP + g) // 2 * LANES
            qa = q_ref[:, c0:c0 + LANES]
            qms.append(jnp.where(low if g % 2 == 0 else (lane >= HEAD_DIM), qa, jnp.zeros_like(qa)))
        q_all = jnp.concatenate(qms, axis=0)
        return _dot_nt(q_all, kc), _dot_nt(q_all, kw)

    def softmax(kv, s_c_all, s_w_all):
        probs = []
        for g in range(D_GROUP):
            sk = sink_ref[kv * D_GROUP + g] * LOG2E
            p_cs, p_ws, sinks = [], [], []
            for blk in range(nb):
                rows = slice(g * tq + blk * WINDOW, g * tq + (blk + 1) * WINDOW)
                s_c = s_c_all[rows]
                s_w = jnp.where(masks[blk], s_w_all[rows, blk * WINDOW:blk * WINDOW + span], NEG_INF)
                mx = jnp.maximum(jnp.maximum(jnp.max(s_c, axis=1, keepdims=True),
                                             jnp.max(s_w, axis=1, keepdims=True)), sk)
                p_cs.append(jnp.exp2(s_c - mx).astype(BF16))
                p_ws.append(jnp.concatenate([zeros] * blk + [jnp.exp2(s_w - mx).astype(BF16)]
                                            + [zeros] * (nb - 1 - blk), axis=1))
                sinks.append(jnp.exp2(sk - mx))
            probs.append((jnp.concatenate(p_cs, axis=0), jnp.concatenate(p_ws, axis=0),
                          jnp.concatenate(sinks, axis=0)))
        return probs

    def outputs(kv, probs):
        tile = slice(kv * LANES, (kv + 1) * LANES)
        vw = jnp.concatenate([vp_ref[:, tile], vm_ref[:, tile], vn_ref[:, tile]], axis=0)
        vc = c_ref[:, D_KV_HEADS * LANES + kv * LANES:D_KV_HEADS * LANES + (kv + 1) * LANES]
        outs = [None] * D_GROUP
        for parity in range(2):
            heads = list(range(parity, D_GROUP, 2))
            p_c = jnp.concatenate([probs[g][0] for g in heads], axis=0)
            p_w = jnp.concatenate([probs[g][1] for g in heads], axis=0)
            p_sink = jnp.concatenate([probs[g][2] for g in heads], axis=0)
            o = (jnp.dot(p_c, with_ones(vc, parity == 0), preferred_element_type=F32)
                 + jnp.dot(p_w, with_ones(vw, parity == 0), preferred_element_type=F32))
            o = o / (pltpu.roll(o, HEAD_DIM, 1) + p_sink)
            for idx, g in enumerate(heads):
                outs[g] = o[idx * tq:(idx + 1) * tq]
        for a in range(D_GROUP // 2):
            c0 = (kv * D_GROUP // 2 + a) * LANES
            o_ref[:, c0:c0 + LANES] = jnp.where(low, outs[2 * a], outs[2 * a + 1]).astype(BF16)

    for kv in range(D_KV_HEADS):
        outputs(kv, softmax(kv, *scores(kv)))


def _window_attention(sink, proj, q_col0, k_col0, v_col0, pc, *, tq):
    n = proj.shape[0]
    nc = pc.shape[0]
    kvw = D_KV_HEADS * LANES
    per = tq // WINDOW
    nw = n // WINDOW

    def window(col0):
        cb = col0 // kvw
        return [pl.BlockSpec((WINDOW, kvw), lambda i: (jnp.maximum(i * per - 1, 0), cb)),
                pl.BlockSpec((tq, kvw), lambda i: (i, cb)),
                pl.BlockSpec((WINDOW, kvw), lambda i: (jnp.minimum((i + 1) * per, nw - 1), cb))]

    return pl.pallas_call(
        functools.partial(_win_attn_kernel, tq=tq, n=n),
        grid=(n // tq,),
        in_specs=[pl.BlockSpec(memory_space=pltpu.SMEM),
                  pl.BlockSpec((tq, D_Q_WIDTH), lambda i: (i, q_col0 // D_Q_WIDTH))]
                 + window(k_col0) + window(v_col0)
                 + [pl.BlockSpec((nc, 2 * kvw), lambda i: (0, 0))],
        out_specs=pl.BlockSpec((tq, D_Q_WIDTH), lambda i: (i, 0)),
        out_shape=jax.ShapeDtypeStruct((n, D_Q_WIDTH), BF16),
        compiler_params=_params(("parallel",)),
        name="window_attention",
    )(sink, proj, proj, proj, proj, proj, proj, proj, pc)


def _dup_head_cols(w):
    d = w.shape[0]
    w = w.reshape(d, D_KV_HEADS, 1, HEAD_DIM)
    return jnp.broadcast_to(w, (d, D_KV_HEADS, 2, HEAD_DIM)).reshape(d, D_KV_HEADS * LANES)


OUT_ROWS = 256


def _out_kernel(a_ref, b_ref, x_ref, w_ref, gpost_ref, gate_ref, gpre_ref, sc_ref, sh_ref, x1_ref, hf_ref):
    half = a_ref.shape[1]
    for r in range(x_ref.shape[0] // OUT_ROWS):
        rows = slice(r * OUT_ROWS, (r + 1) * OUT_ROWS)
        y = (jnp.dot(a_ref[rows, :], w_ref[:half, :], preferred_element_type=F32)
             + jnp.dot(b_ref[rows, :], w_ref[half:, :], preferred_element_type=F32))
        x1 = x_ref[rows, :] + gate_ref[...] * (_rms(y) * gpost_ref[...])
        x1_ref[rows, :] = x1
        hf_ref[rows, :] = ((_rms(x1) * gpre_ref[...]) * (1.0 + sc_ref[...]) + sh_ref[...]).astype(BF16)


def _out_project(a, b, x, w, g_post, gate, g_pre, scale, shift, *, tm):
    n, d = x.shape
    half = a.shape[1]
    vec = pl.BlockSpec((1, d), lambda i: (0, 0))
    return pl.pallas_call(
        _out_kernel,
        grid=(n // tm,),
        in_specs=[pl.BlockSpec((tm, half), lambda i: (i, 0)),
                  pl.BlockSpec((tm, half), lambda i: (i, 0)),
                  pl.BlockSpec((tm, d), lambda i: (i, 0)),
                  pl.BlockSpec((2 * half, d), lambda i: (0, 0), pipeline_mode=pl.Buffered(1)),
                  vec, vec, vec, vec, vec],
        out_specs=[pl.BlockSpec((tm, d), lambda i: (i, 0)), pl.BlockSpec((tm, d), lambda i: (i, 0))],
        out_shape=[jax.ShapeDtypeStruct((n, d), F32), jax.ShapeDtypeStruct((n, d), BF16)],
        compiler_params=_params(("parallel",)),
        name="out_project",
    )(a, b, x, w, g_post, gate, g_pre, scale, shift)


MLP_ROWS = 256


def _mlp_kernel(hf_ref, w1_ref, w2_ref, x1_ref, gpost_ref, gate_ref, o_ref, acc_scr):
    f = pl.program_id(1)

    @pl.when(f == 0)
    def _():
        acc_scr[...] = jnp.zeros(acc_scr.shape, F32)

    for r in range(hf_ref.shape[0] // MLP_ROWS):
        rows = slice(r * MLP_ROWS, (r + 1) * MLP_ROWS)
        h = jnp.dot(hf_ref[rows, :], w1_ref[...], preferred_element_type=F32)
        h = jnp.square(jnp.maximum(h, 0.0)).astype(BF16)
        acc_scr[rows, :] += jnp.dot(h, w2_ref[...], preferred_element_type=F32)

    @pl.when(f == pl.num_programs(1) - 1)
    def _():
        o_ref[...] = x1_ref[...] + gate_ref[...] * (_rms(acc_scr[...]) * gpost_ref[...])


def _mlp(hf, w1, w2, x1, g_post, gate, *, tm, tf):
    n, d = x1.shape
    ff = w1.shape[1]
    vec = pl.BlockSpec((1, d), lambda i, f: (0, 0))
    return pl.pallas_call(
        _mlp_kernel,
        grid=(n // tm, ff // tf),
        in_specs=[pl.BlockSpec((tm, d), lambda i, f: (i, 0)),
                  pl.BlockSpec((d, tf), lambda i, f: (0, f)),
                  pl.BlockSpec((tf, d), lambda i, f: (f, 0)),
                  pl.BlockSpec((tm, d), lambda i, f: (i, 0)),
                  vec, vec],
        out_specs=pl.BlockSpec((tm, d), lambda i, f: (i, 0)),
        out_shape=jax.ShapeDtypeStruct((n, d), F32),
        scratch_shapes=[pltpu.VMEM((tm, d), F32)],
        compiler_params=_params(("parallel", "arbitrary")),
        name="mlp",
    )(hf, w1, w2, x1, g_post, gate)


Q0_SCALE = ATTN_SCALE * LOG2E
Q1_SCALE = ATTN_SCALE * LOG2E
PLAIN = (0.0, 1.0)


def _rope_table(n):
    rows = n // GRID_W
    pairs = HEAD_DIM // 4
    inv = jnp.power(ROPE_BASE, -jnp.arange(pairs, dtype=F32) / pairs)
    ang_r = jnp.arange(rows, dtype=F32)[:, None] * inv
    ang_c = jnp.arange(GRID_W, dtype=F32)[:, None] * inv

    def grid(f):
        r = jnp.broadcast_to(f(ang_r)[:, None, :], (rows, GRID_W, pairs))
        c = jnp.broadcast_to(f(ang_c)[None, :, :], (rows, GRID_W, pairs))
        return jnp.concatenate([r, c], axis=-1).reshape(n, 2 * pairs)

    cos, sin = grid(jnp.cos), grid(jnp.sin)
    return jnp.stack([jnp.tile(cos, (1, 4)), jnp.tile(jnp.concatenate([-sin, sin], axis=-1), (1, 2))])


def _row(v):
    return v.reshape(1, -1)


def _split_mod(m):
    return [_row(t) for t in jnp.split(m, N_MOD)]


def kernel(x, c, ctx, c_ctx, l0_mod_w, l0_mod_b, l0_norm_mix_pre, l0_norm_mix_post, l0_norm_mlp_pre, l0_norm_mlp_post, l0_w_in, l0_conv_w, l0_conv_b, l0_ln_g, l0_ln_b, l0_lambda_q1, l0_lambda_k1, l0_lambda_q2, l0_lambda_k2, l0_subln_g, l0_w_out, l0_mlp_w1, l0_mlp_w2, l1_mod_w, l1_mod_b, l1_norm_mix_pre, l1_norm_mix_post, l1_norm_mlp_pre, l1_norm_mlp_post, l1_w_in, l1_sconv_w, l1_sink, l1_w_out, l1_mlp_w1, l1_mlp_w2):
    n = x.shape[1]
    nc = ctx.shape[1]
    xs = x[0]
    cs = ctx[0]
    rope = _rope_table(n)
    cvecs = jnp.zeros((8, D_MODEL), F32).at[0].set(c[0]).at[1].set(c_ctx)

    mod = _modulation(cvecs, l0_mod_w, l0_mod_b)
    sh_m, sc_m, gt_m, sh_f, sc_f, gt_f = _split_mod(mod[0])
    csh_m, csc_m, cgt_m, csh_f, csc_f, cgt_f = _split_mod(mod[1])
    w_in = l0_w_in.astype(BF16)
    w_out = l0_w_out.astype(BF16)
    w1 = l0_mlp_w1.astype(BF16)
    w2 = l0_mlp_w2.astype(BF16)
    g_pre, g_post = _row(l0_norm_mix_pre), _row(l0_norm_mix_post)
    gf_pre, gf_post = _row(l0_norm_mlp_pre), _row(l0_norm_mlp_post)
    q0, k0, v0 = A_IN, A_IN + B_WIDTH, A_IN + 2 * B_WIDTH

    blocks = lambda *widths_coefs: [c for w, c in widths_coefs for _ in range(w // PROJ_SUB)]
    proj = _project(xs, g_pre, sc_m, sh_m, w_in, rope,
                    blocks((A_IN, PLAIN), (B_WIDTH, (Q0_SCALE, 0.0)), (B_WIDTH, (1.0, 0.0)), (B_WIDTH, PLAIN)),
                    tm=512, tn=2560, out_rows=n + nc)
    pc, proj = _project(cs, g_pre, csc_m, csh_m, w_in, rope,
                        blocks((A_IN, PLAIN), (B_WIDTH, (0.0, Q0_SCALE)), (2 * B_WIDTH, PLAIN)),
                        tm=nc, tn=1024, fill=(proj, n))
    lam_init = 0.8 - 0.6 * math.exp(-0.3 * 0)
    lam_rows = jnp.zeros((8, LANES), F32)
    for r, lv in enumerate((l0_lambda_q1, l0_lambda_k1, l0_lambda_q2, l0_lambda_k2)):
        lam_rows = lam_rows.at[r, :HEAD_DIM].set(lv)

    a_lat = _conformer_conv(proj, l0_conv_w, l0_conv_b, l0_ln_g, l0_ln_b, n=n, t=512)
    a_ctx = _conformer_conv(pc, l0_conv_w, l0_conv_b, l0_ln_g, l0_ln_b, n=nc, t=nc)
    b_lat = _diff_attention(lam_rows, l0_subln_g, proj, q0, proj, k0, v0, n_q=n, tq=1024, tk=3328,
                            lam_init=lam_init)
    b_ctx = _diff_attention(lam_rows, l0_subln_g, pc, q0, pc, k0, v0, n_q=nc, tq=nc, tk=nc, lam_init=lam_init)

    x1, hf = _out_project(a_lat, b_lat, xs, w_out, g_post, gt_m, gf_pre, sc_f, sh_f, tm=512)
    xs = _mlp(hf, w1, w2, x1, gf_post, gt_f, tm=512, tf=1024)
    c1, hcf = _out_project(a_ctx, b_ctx, cs, w_out, g_post, cgt_m, gf_pre, csc_f, csh_f, tm=nc)
    cs = _mlp(hcf, w1, w2, c1, gf_post, cgt_f, tm=nc, tf=512)

    mod = _modulation(cvecs, l1_mod_w, l1_mod_b)
    sh_m, sc_m, gt_m, sh_f, sc_f, gt_f = _split_mod(mod[0])
    csh_m, csc_m = _split_mod(mod[1])[:2]
    w_out = l1_w_out.astype(BF16)
    w1 = l1_mlp_w1.astype(BF16)
    w2 = l1_mlp_w2.astype(BF16)
    g_pre, g_post = _row(l1_norm_mix_pre), _row(l1_norm_mix_post)
    gf_pre, gf_post = _row(l1_norm_mlp_pre), _row(l1_norm_mlp_post)
    q0, k0, v0 = C_IN, C_IN + D_Q_WIDTH, C_IN + D_Q_WIDTH + D_KV_WIDTH
    w_in = jnp.concatenate([l1_w_in[:, :k0], _dup_head_cols(l1_w_in[:, k0:v0]), _dup_head_cols(l1_w_in[:, v0:])],
                           axis=1).astype(BF16)
    kd0, vd0 = k0, k0 + 2 * D_KV_WIDTH

    proj = _project(xs, g_pre, sc_m, sh_m, w_in, rope,
                    blocks((C_IN, PLAIN), (D_Q_WIDTH, (Q1_SCALE, 0.0)), (2 * D_KV_WIDTH, (1.0, 0.0)),
                           (2 * D_KV_WIDTH, PLAIN)), tm=512, tn=2560)
    pc = _project(cs, g_pre, csc_m, csh_m, w_in[:, kd0:], rope, blocks((4 * D_KV_WIDTH, PLAIN)), tm=nc, tn=1024)
    c_lat = _short_conv(proj, l1_sconv_w, t=512)
    d_lat = _window_attention(l1_sink, proj, q0, kd0, vd0, pc, tq=256)
    x1, hf = _out_project(c_lat, d_lat, xs, w_out, g_post, gt_m, gf_pre, sc_f, sh_f, tm=512)
    xs = _mlp(hf, w1, w2, x1, gf_post, gt_f, tm=512, tf=1024)
    return xs[None]
```

```python
import functools
import math

import jax
import jax.numpy as jnp
from jax import lax
from jax.experimental import pallas as pl
from jax.experimental.pallas import tpu as pltpu

F32 = jnp.float32
BF16 = jnp.bfloat16

D_MODEL = 2048
HEAD_DIM = 64
HALF = HEAD_DIM // 2
GRID_W = 64
ROPE_BASE = 10000.0
NORM_EPS = 1e-6
LN_EPS = 1e-5
ATTN_SCALE = HEAD_DIM ** -0.5
LOG2E = math.log2(math.e)
NEG_INF = -1e30
N_MOD = 6
WINDOW = 128
LANES = 128
HALO = 16

A_WIDTH = 1024
A_IN = 2 * A_WIDTH
CONV_A_WIDTH = 31
B_HEADS = 8
B_WIDTH = 1024
C_WIDTH = 1024
C_IN = 3 * C_WIDTH
D_HEADS = 16
D_KV_HEADS = 4
D_GROUP = 4
D_Q_WIDTH = 1024
D_KV_WIDTH = 256
D_FF = 4 * D_MODEL

VMEM_LIMIT = 56 * 1024 * 1024


def _params(sem):
    return pltpu.CompilerParams(dimension_semantics=sem, vmem_limit_bytes=VMEM_LIMIT)


def _rms(x, eps=NORM_EPS):
    return x * lax.rsqrt(jnp.mean(x * x, axis=-1, keepdims=True) + eps)


def _dot_nt(a, b):
    return lax.dot_general(a, b, (((1,), (1,)), ((), ())), preferred_element_type=F32)


def _mod_kernel(c_ref, w_ref, b_ref, o_ref):
    c = c_ref[...]
    s = c * jax.nn.sigmoid(c)
    o_ref[...] = jnp.dot(s, w_ref[...], preferred_element_type=F32) + b_ref[...]


def _modulation(cvecs, w, b):
    tn = 1024
    n_out = w.shape[1]
    return pl.pallas_call(
        _mod_kernel,
        grid=(n_out // tn,),
        in_specs=[pl.BlockSpec((8, D_MODEL), lambda j: (0, 0)),
                  pl.BlockSpec((D_MODEL, tn), lambda j: (0, j)),
                  pl.BlockSpec((1, tn), lambda j: (0, j))],
        out_specs=pl.BlockSpec((8, tn), lambda j: (0, j)),
        out_shape=jax.ShapeDtypeStruct((8, n_out), F32),
        compiler_params=_params(("arbitrary",)),
        name="modulation",
    )(cvecs, w, b.reshape(1, n_out))


PROJ_SUB = 512
PROJ_ROWS = 256


def _proj_kernel(coef_ref, x_ref, g_ref, sc_ref, sh_ref, tab_ref, w_ref, *rest, tn, fill):
    outs, h_scr = (rest[1:3], rest[3]) if fill else (rest[:1], rest[1])
    nsub = tn // PROJ_SUB
    j = pl.program_id(1)

    @pl.when(j == 0)
    def _():
        y = _rms(x_ref[...]) * g_ref[...]
        h_scr[...] = (y * (1.0 + sc_ref[...]) + sh_ref[...]).astype(BF16)

    lane = lax.broadcasted_iota(jnp.int32, (PROJ_ROWS, LANES), 1)
    first_half = (lane & (HEAD_DIM - 1)) < HALF
    for r in range(x_ref.shape[0] // PROJ_ROWS):
        rows = slice(r * PROJ_ROWS, (r + 1) * PROJ_ROWS)
        acc = jnp.dot(h_scr[rows, :], w_ref[...], preferred_element_type=F32)
        cosf = tab_ref[0, rows, :]
        sinf = tab_ref[1, rows, :]
        for s in range(nsub):
            alpha = coef_ref[2 * (j * nsub + s)]
            beta = coef_ref[2 * (j * nsub + s) + 1]
            a = alpha * cosf + beta
            b = alpha * sinf
            for c in range(s * PROJ_SUB // LANES, (s + 1) * PROJ_SUB // LANES):
                y = acc[:, c * LANES:(c + 1) * LANES]
                swapped = jnp.where(first_half, pltpu.roll(y, LANES - HALF, 1), pltpu.roll(y, HALF, 1))
                res = (y * a + swapped * b).astype(BF16)
                for o_ref in outs:
                    o_ref[rows, c * LANES:(c + 1) * LANES] = res


def _project(x, g, scale, shift, w, rope_tab, coefs, *, tm, tn, out_rows=None, fill=None):
    n, d = x.shape
    p = w.shape[1]
    assert len(coefs) == p // PROJ_SUB
    vec = pl.BlockSpec((1, d), lambda i, j: (0, 0))
    in_specs = [pl.BlockSpec(memory_space=pltpu.SMEM),
                pl.BlockSpec((tm, d), lambda i, j: (i, 0)), vec, vec, vec,
                pl.BlockSpec((2, tm, LANES), lambda i, j: (0, i, 0)),
                pl.BlockSpec((d, tn), lambda i, j: (0, j))]
    args = [jnp.asarray(coefs, F32).reshape(-1), x, g, scale, shift, rope_tab, w]
    out_specs = [pl.BlockSpec((tm, tn), lambda i, j: (i, j))]
    out_shape = [jax.ShapeDtypeStruct((out_rows or n, p), BF16)]
    aliases = {}
    if fill is not None:
        buf, row0 = fill
        blk0 = row0 // tm
        in_specs.append(pl.BlockSpec(memory_space=pl.ANY))
        args.append(buf)
        out_specs.append(pl.BlockSpec((tm, tn), lambda i, j: (blk0 + i, j)))
        out_shape.append(jax.ShapeDtypeStruct(buf.shape, BF16))
        aliases = {len(args) - 1: 1}
    res = pl.pallas_call(
        functools.partial(_proj_kernel, tn=tn, fill=fill is not None),
        grid=(n // tm, p // tn),
        in_specs=in_specs,
        out_specs=out_specs,
        out_shape=out_shape,
        input_output_aliases=aliases,
        scratch_shapes=[pltpu.VMEM((tm, d), BF16)],
        compiler_params=_params(("parallel", "arbitrary")),
        name="norm_project",
    )(*args)
    return res if fill is not None else res[0]


CONV_A_ROWS = 32


def _conv_a_kernel(main_ref, left_ref, right_ref, w_ref, cb_ref, lg_ref, lb_ref, o_ref, ext_scr, sh_scr, *, t):
    i = pl.program_id(0)
    last = pl.num_programs(0) - 1

    def glu(ref):
        v = ref[...].astype(F32)
        return v[:, :A_WIDTH] * jax.nn.sigmoid(v[:, A_WIDTH:])

    ext_scr[0:HALO, :] = jnp.where(i > 0, glu(left_ref), 0.0)
    ext_scr[HALO:HALO + t, :] = glu(main_ref)
    ext_scr[HALO + t:, :] = jnp.where(i < last, glu(right_ref), 0.0)
    ext = ext_scr[...]
    rows = t + 2 * HALO
    srows = t + 3 * 8
    sh_scr[0] = ext[:srows]
    for b in range(1, 8):
        sh_scr[b] = pltpu.roll(ext, rows - b, 0)[:srows]

    def chunk(c, carry):
        r0 = pl.multiple_of(c * CONV_A_ROWS, CONV_A_ROWS)
        acc = jnp.zeros((CONV_A_ROWS, A_WIDTH), F32)
        for k in range(CONV_A_WIDTH):
            a, b = divmod(k + 1, 8)
            acc = acc + sh_scr[b, pl.ds(r0 + 8 * a, CONV_A_ROWS), :] * w_ref[k:k + 1, :]
        v = acc + cb_ref[...]
        mu = jnp.mean(v, axis=-1, keepdims=True)
        vc = v - mu
        var = jnp.mean(vc * vc, axis=-1, keepdims=True)
        y = vc * lax.rsqrt(var + LN_EPS) * lg_ref[...] + lb_ref[...]
        o_ref[pl.ds(r0, CONV_A_ROWS), :] = (y * jax.nn.sigmoid(y)).astype(BF16)
        return carry

    lax.fori_loop(0, t // CONV_A_ROWS, chunk, 0, unroll=2)


def _conformer_conv(proj, conv_w, conv_b, ln_g, ln_b, *, n, t):
    per = t // HALO
    nh = n // HALO
    vec = pl.BlockSpec((1, A_WIDTH), lambda i: (0, 0))
    return pl.pallas_call(
        functools.partial(_conv_a_kernel, t=t),
        grid=(n // t,),
        in_specs=[pl.BlockSpec((t, A_IN), lambda i: (i, 0)),
                  pl.BlockSpec((HALO, A_IN), lambda i: (jnp.maximum(i * per - 1, 0), 0)),
                  pl.BlockSpec((HALO, A_IN), lambda i: (jnp.minimum((i + 1) * per, nh - 1), 0)),
                  pl.BlockSpec((CONV_A_WIDTH, A_WIDTH), lambda i: (0, 0)),
                  vec, vec, vec],
        out_specs=pl.BlockSpec((t, A_WIDTH), lambda i: (i, 0)),
        out_shape=jax.ShapeDtypeStruct((n, A_WIDTH), BF16),
        scratch_shapes=[pltpu.VMEM((t + 2 * HALO, A_WIDTH), F32),
                        pltpu.VMEM((8, t + 24, A_WIDTH), F32)],
        compiler_params=_params(("parallel",)),
        name="conformer_conv",
    )(proj, proj, proj, conv_w, conv_b.reshape(1, -1), ln_g.reshape(1, -1), ln_b.reshape(1, -1))


def _tile_lanes(x, reps):
    return x if reps == 1 else jnp.concatenate([x] * reps, axis=1)


def _diff_attn_kernel(lam_ref, sub_ref, q_ref, k_ref, v_ref, o_ref, qm_scr, m_scr, acc_scr, *, lam_init, tk):
    q = q_ref[...]
    lane = lax.broadcasted_iota(jnp.int32, q.shape, 1)
    zero = jnp.zeros_like(q)
    qm_scr[0] = jnp.where(lane < HEAD_DIM, q, zero)
    qm_scr[1] = jnp.where(lane >= HEAD_DIM, q, zero)
    m_scr[...] = jnp.full(m_scr.shape, -jnp.inf, F32)
    acc_scr[...] = jnp.zeros(acc_scr.shape, F32)
    ones = jnp.ones((tk, LANES), BF16)

    def chunk(c, carry):
        r0 = pl.multiple_of(c * tk, tk)
        k = k_ref[pl.ds(r0, tk), :]
        v_ext = jnp.concatenate([v_ref[pl.ds(r0, tk), :], ones], axis=1)
        scores = [_dot_nt(qm_scr[m], k) for m in range(2)]
        for m in range(2):
            s = scores[m]
            m_prev = m_scr[m]
            m_new = jnp.maximum(m_prev, jnp.max(s, axis=1, keepdims=True))
            alpha = jnp.exp2(m_prev - m_new)
            p = jnp.exp2(s - _tile_lanes(m_new, tk // LANES))
            acc_scr[m] = (_tile_lanes(alpha, 2) * acc_scr[m]
                          + jnp.dot(p.astype(BF16), v_ext, preferred_element_type=F32))
            m_scr[m] = m_new
        return carry

    lax.fori_loop(0, k_ref.shape[0] // tk, chunk, 0)

    lp = lam_ref[...]
    lam = (jnp.exp(jnp.sum(lp[0:1] * lp[1:2], axis=1, keepdims=True))
           - jnp.exp(jnp.sum(lp[2:3] * lp[3:4], axis=1, keepdims=True)) + lam_init)
    o = (acc_scr[0, :, :LANES] / acc_scr[0, :, LANES:]
         - lam * (acc_scr[1, :, :LANES] / acc_scr[1, :, LANES:]))
    o = _rms(o) * sub_ref[...] * (1.0 - lam_init)
    o_ref[...] = o.astype(BF16)


def _diff_attention(lam_rows, subln_g, q_src, q_col0, kv_src, k_col0, v_col0, *, n_q, tq, tk, lam_init):
    nk = kv_src.shape[0]
    qb, kb, vb = q_col0 // LANES, k_col0 // LANES, v_col0 // LANES
    return pl.pallas_call(
        functools.partial(_diff_attn_kernel, lam_init=lam_init, tk=tk),
        grid=(B_HEADS, n_q // tq),
        in_specs=[pl.BlockSpec((8, LANES), lambda h, i: (0, 0)),
                  pl.BlockSpec((1, LANES), lambda h, i: (0, 0)),
                  pl.BlockSpec((tq, LANES), lambda h, i: (i, qb + h)),
                  pl.BlockSpec((nk, LANES), lambda h, i: (0, kb + h)),
                  pl.BlockSpec((nk, LANES), lambda h, i: (0, vb + h))],
        out_specs=pl.BlockSpec((tq, LANES), lambda h, i: (i, h)),
        out_shape=jax.ShapeDtypeStruct((n_q, B_WIDTH), BF16),
        scratch_shapes=[pltpu.VMEM((2, tq, LANES), BF16),
                        pltpu.VMEM((2, tq, LANES), F32),
                        pltpu.VMEM((2, tq, 2 * LANES), F32)],
        compiler_params=_params(("parallel", "arbitrary")),
        name="diff_attention",
    )(lam_rows, subln_g.reshape(1, LANES), q_src, kv_src, kv_src)


def _conv_c_kernel(main_ref, left_ref, right_ref, w_ref, o_ref, ext_scr, *, t):
    i = pl.program_id(0)
    last = pl.num_programs(0) - 1

    def gated(ref):
        v = ref[...].astype(F32)
        return v[:, C_WIDTH:2 * C_WIDTH] * v[:, 2 * C_WIDTH:]

    ext_scr[0:HALO, :] = jnp.where(i > 0, gated(left_ref), 0.0)
    ext_scr[HALO:HALO + t, :] = gated(main_ref)
    ext_scr[HALO + t:, :] = jnp.where(i < last, gated(right_ref), 0.0)
    ext = ext_scr[...]
    rows = t + 2 * HALO
    prev = pltpu.roll(ext, 1, 0)[HALO:HALO + t]
    nxt = pltpu.roll(ext, rows - 1, 0)[HALO:HALO + t]
    conv = prev * w_ref[0:1, :] + ext[HALO:HALO + t] * w_ref[1:2, :] + nxt * w_ref[2:3, :]
    o_ref[...] = (main_ref[:, :C_WIDTH].astype(F32) * conv).astype(BF16)


def _short_conv(proj, w, *, t):
    n = proj.shape[0]
    per = t // HALO
    nh = n // HALO
    return pl.pallas_call(
        functools.partial(_conv_c_kernel, t=t),
        grid=(n // t,),
        in_specs=[pl.BlockSpec((t, C_IN), lambda i: (i, 0)),
                  pl.BlockSpec((HALO, C_IN), lambda i: (jnp.maximum(i * per - 1, 0), 0)),
                  pl.BlockSpec((HALO, C_IN), lambda i: (jnp.minimum((i + 1) * per, nh - 1), 0)),
                  pl.BlockSpec((3, C_WIDTH), lambda i: (0, 0))],
        out_specs=pl.BlockSpec((t, C_WIDTH), lambda i: (i, 0)),
        out_shape=jax.ShapeDtypeStruct((n, C_WIDTH), BF16),
        scratch_shapes=[pltpu.VMEM((t + 2 * HALO, C_WIDTH), F32)],
        compiler_params=_params(("parallel",)),
        name="short_conv",
    )(proj, proj, proj, w)


def _win_attn_kernel(sink_ref, q_ref, kp_ref, km_ref, kn_ref, vp_ref, vm_ref, vn_ref, c_ref, o_ref, *, tq, n):
    i = pl.program_id(0)
    lane = lax.broadcasted_iota(jnp.int32, (tq, LANES), 1)
    low = lane < HEAD_DIM

    def with_ones(v, keep_low):
        col = lax.broadcasted_iota(jnp.int32, v.shape, 1)
        return jnp.where((col < HEAD_DIM) if keep_low else (col >= HEAD_DIM), v, jnp.ones_like(v))

    nb = tq // WINDOW
    span = 3 * WINDOW
    row = lax.broadcasted_iota(jnp.int32, (WINDOW, span), 0)
    col = lax.broadcasted_iota(jnp.int32, (WINDOW, span), 1)
    band = (col >= row) & (col <= row + 2 * WINDOW)
    masks = []
    for blk in range(nb):
        pos = i * tq + (blk - 1) * WINDOW + col
        masks.append(band & (pos >= 0) & (pos < n))
    zeros = jnp.zeros((WINDOW, WINDOW), BF16)

    def scores(kv):
        tile = slice(kv * LANES, (kv + 1) * LANES)
        kw = jnp.concatenate([kp_ref[:, tile], km_ref[:, tile], kn_ref[:, tile]], axis=0)
        kc = c_ref[:, tile]
        qms = []
        for g in range(D_GROUP):
            c0 = (kv * D_GROUP + g) // 2 * LANES
            qa = q_ref[:, c0:c0 + LANES]
            qms.append(jnp.where(low if g % 2 == 0 else (lane >= HEAD_DIM), qa, jnp.zeros_like(qa)))
        q_all = jnp.concatenate(qms, axis=0)
        return _dot_nt(q_all, kc), _dot_nt(q_all, kw)

    def softmax(kv, s_c_all, s_w_all):
        probs = []
        for g in range(D_GROUP):
            sk = sink_ref[kv * D_GROUP + g] * LOG2E
            p_cs, p_ws, sinks = [], [], []
            for blk in range(nb):
                rows = slice(g * tq + blk * WINDOW, g * tq + (blk + 1) * WINDOW)
                s_c = s_c_all[rows]
                s_w = jnp.where(masks[blk], s_w_all[rows, blk * WINDOW:blk * WINDOW + span], NEG_INF)
                mx = jnp.maximum(jnp.maximum(jnp.max(s_c, axis=1, keepdims=True),
                                             jnp.max(s_w, axis=1, keepdims=True)), sk)
                p_cs.append(jnp.exp2(s_c - mx).astype(BF16))
                p_ws.append(jnp.concatenate([zeros] * blk + [jnp.exp2(s_w - mx).astype(BF16)]
                                            + [zeros] * (nb - 1 - blk), axis=1))
                sinks.append(jnp.exp2(sk - mx))
            probs.append((jnp.concatenate(p_cs, axis=0), jnp.concatenate(p_ws, axis=0),
                          jnp.concatenate(sinks, axis=0)))
        return probs

    def outputs(kv, probs):
        tile = slice(kv * LANES, (kv + 1) * LANES)
        vw = jnp.concatenate([vp_ref[:, tile], vm_ref[:, tile], vn_ref[:, tile]], axis=0)
        vc = c_ref[:, D_KV_HEADS * LANES + kv * LANES:D_KV_HEADS * LANES + (kv + 1) * LANES]
        outs = [None] * D_GROUP
        for parity in range(2):
            heads = list(range(parity, D_GROUP, 2))
            p_c = jnp.concatenate([probs[g][0] for g in heads], axis=0)
            p_w = jnp.concatenate([probs[g][1] for g in heads], axis=0)
            p_sink = jnp.concatenate([probs[g][2] for g in heads], axis=0)
            o = (jnp.dot(p_c, with_ones(vc, parity == 0), preferred_element_type=F32)
                 + jnp.dot(p_w, with_ones(vw, parity == 0), preferred_element_type=F32))
            o = o / (pltpu.roll(o, HEAD_DIM, 1) + p_sink)
            for idx, g in enumerate(heads):
                outs[g] = o[idx * tq:(idx + 1) * tq]
        for a in range(D_GROUP // 2):
            c0 = (kv * D_GROUP // 2 + a) * LANES
            o_ref[:, c0:c0 + LANES] = jnp.where(low, outs[2 * a], outs[2 * a + 1]).astype(BF16)

    for kv in range(D_KV_HEADS):
        outputs(kv, softmax(kv, *scores(kv)))


def _window_attention(sink, proj, q_col0, k_col0, v_col0, pc, *, tq):
    n = proj.shape[0]
    nc = pc.shape[0]
    kvw = D_KV_HEADS * LANES
    per = tq // WINDOW
    nw = n // WINDOW

    def window(col0):
        cb = col0 // kvw
        return [pl.BlockSpec((WINDOW, kvw), lambda i: (jnp.maximum(i * per - 1, 0), cb)),
                pl.BlockSpec((tq, kvw), lambda i: (i, cb)),
                pl.BlockSpec((WINDOW, kvw), lambda i: (jnp.minimum((i + 1) * per, nw - 1), cb))]

    return pl.pallas_call(
        functools.partial(_win_attn_kernel, tq=tq, n=n),
        grid=(n // tq,),
        in_specs=[pl.BlockSpec(memory_space=pltpu.SMEM),
                  pl.BlockSpec((tq, D_Q_WIDTH), lambda i: (i, q_col0 // D_Q_WIDTH))]
                 + window(k_col0) + window(v_col0)
                 + [pl.BlockSpec((nc, 2 * kvw), lambda i: (0, 0))],
        out_specs=pl.BlockSpec((tq, D_Q_WIDTH), lambda i: (i, 0)),
        out_shape=jax.ShapeDtypeStruct((n, D_Q_WIDTH), BF16),
        compiler_params=_params(("parallel",)),
        name="window_attention",
    )(sink, proj, proj, proj, proj, proj, proj, proj, pc)


def _dup_head_cols(w):
    d = w.shape[0]
    w = w.reshape(d, D_KV_HEADS, 1, HEAD_DIM)
    return jnp.broadcast_to(w, (d, D_KV_HEADS, 2, HEAD_DIM)).reshape(d, D_KV_HEADS * LANES)


OUT_ROWS = 256


def _out_kernel(a_ref, b_ref, x_ref, w_ref, gpost_ref, gate_ref, gpre_ref, sc_ref, sh_ref, x1_ref, hf_ref):
    half = a_ref.shape[1]
    for r in range(x_ref.shape[0] // OUT_ROWS):
        rows = slice(r * OUT_ROWS, (r + 1) * OUT_ROWS)
        y = (jnp.dot(a_ref[rows, :], w_ref[:half, :], preferred_element_type=F32)
             + jnp.dot(b_ref[rows, :], w_ref[half:, :], preferred_element_type=F32))
        x1 = x_ref[rows, :] + gate_ref[...] * (_rms(y) * gpost_ref[...])
        x1_ref[rows, :] = x1
        hf_ref[rows, :] = ((_rms(x1) * gpre_ref[...]) * (1.0 + sc_ref[...]) + sh_ref[...]).astype(BF16)


def _out_project(a, b, x, w, g_post, gate, g_pre, scale, shift, *, tm):
    n, d = x.shape
    half = a.shape[1]
    vec = pl.BlockSpec((1, d), lambda i: (0, 0))
    return pl.pallas_call(
        _out_kernel,
        grid=(n // tm,),
        in_specs=[pl.BlockSpec((tm, half), lambda i: (i, 0)),
                  pl.BlockSpec((tm, half), lambda i: (i, 0)),
                  pl.BlockSpec((tm, d), lambda i: (i, 0)),
                  pl.BlockSpec((2 * half, d), lambda i: (0, 0), pipeline_mode=pl.Buffered(1)),
                  vec, vec, vec, vec, vec],
        out_specs=[pl.BlockSpec((tm, d), lambda i: (i, 0)), pl.BlockSpec((tm, d), lambda i: (i, 0))],
        out_shape=[jax.ShapeDtypeStruct((n, d), F32), jax.ShapeDtypeStruct((n, d), BF16)],
        compiler_params=_params(("parallel",)),
        name="out_project",
    )(a, b, x, w, g_post, gate, g_pre, scale, shift)


def _mlp_kernel(hf_ref, w1_ref, w2_ref, x1_ref, gpost_ref, gate_ref, o_ref, acc_scr):
    f = pl.program_id(1)

    @pl.when(f == 0)
    def _():
        acc_scr[...] = jnp.zeros(acc_scr.shape, F32)

    h = jnp.dot(hf_ref[...], w1_ref[...], preferred_element_type=F32)
    h = jnp.square(jnp.maximum(h, 0.0)).astype(BF16)
    acc_scr[...] += jnp.dot(h, w2_ref[...], preferred_element_type=F32)

    @pl.when(f == pl.num_programs(1) - 1)
    def _():
        o_ref[...] = x1_ref[...] + gate_ref[...] * (_rms(acc_scr[...]) * gpost_ref[...])


def _mlp(hf, w1, w2, x1, g_post, gate, *, tm, tf):
    n, d = x1.shape
    ff = w1.shape[1]
    vec = pl.BlockSpec((1, d), lambda i, f: (0, 0))
    return pl.pallas_call(
        _mlp_kernel,
        grid=(n // tm, ff // tf),
        in_specs=[pl.BlockSpec((tm, d), lambda i, f: (i, 0)),
                  pl.BlockSpec((d, tf), lambda i, f: (0, f)),
                  pl.BlockSpec((tf, d), lambda i, f: (f, 0)),
                  pl.BlockSpec((tm, d), lambda i, f: (i, 0)),
                  vec, vec],
        out_specs=pl.BlockSpec((tm, d), lambda i, f: (i, 0)),
        out_shape=jax.ShapeDtypeStruct((n, d), F32),
        scratch_shapes=[pltpu.VMEM((tm, d), F32)],
        compiler_params=_params(("parallel", "arbitrary")),
        name="mlp",
    )(hf, w1, w2, x1, g_post, gate)


Q0_SCALE = ATTN_SCALE * LOG2E
Q1_SCALE = ATTN_SCALE * LOG2E
PLAIN = (0.0, 1.0)


def _rope_table(n):
    rows = n // GRID_W
    pairs = HEAD_DIM // 4
    inv = jnp.power(ROPE_BASE, -jnp.arange(pairs, dtype=F32) / pairs)
    ang_r = jnp.arange(rows, dtype=F32)[:, None] * inv
    ang_c = jnp.arange(GRID_W, dtype=F32)[:, None] * inv

    def grid(f):
        r = jnp.broadcast_to(f(ang_r)[:, None, :], (rows, GRID_W, pairs))
        c = jnp.broadcast_to(f(ang_c)[None, :, :], (rows, GRID_W, pairs))
        return jnp.concatenate([r, c], axis=-1).reshape(n, 2 * pairs)

    cos, sin = grid(jnp.cos), grid(jnp.sin)
    return jnp.stack([jnp.tile(cos, (1, 4)), jnp.tile(jnp.concatenate([-sin, sin], axis=-1), (1, 2))])


def _row(v):
    return v.reshape(1, -1)


def _split_mod(m):
    return [_row(t) for t in jnp.split(m, N_MOD)]


def kernel(x, c, ctx, c_ctx, l0_mod_w, l0_mod_b, l0_norm_mix_pre, l0_norm_mix_post, l0_norm_mlp_pre, l0_norm_mlp_post, l0_w_in, l0_conv_w, l0_conv_b, l0_ln_g, l0_ln_b, l0_lambda_q1, l0_lambda_k1, l0_lambda_q2, l0_lambda_k2, l0_subln_g, l0_w_out, l0_mlp_w1, l0_mlp_w2, l1_mod_w, l1_mod_b, l1_norm_mix_pre, l1_norm_mix_post, l1_norm_mlp_pre, l1_norm_mlp_post, l1_w_in, l1_sconv_w, l1_sink, l1_w_out, l1_mlp_w1, l1_mlp_w2):
    n = x.shape[1]
    nc = ctx.shape[1]
    xs = x[0]
    cs = ctx[0]
    rope = _rope_table(n)
    cvecs = jnp.zeros((8, D_MODEL), F32).at[0].set(c[0]).at[1].set(c_ctx)

    mod = _modulation(cvecs, l0_mod_w, l0_mod_b)
    sh_m, sc_m, gt_m, sh_f, sc_f, gt_f = _split_mod(mod[0])
    csh_m, csc_m, cgt_m, csh_f, csc_f, cgt_f = _split_mod(mod[1])
    w_in = l0_w_in.astype(BF16)
    w_out = l0_w_out.astype(BF16)
    w1 = l0_mlp_w1.astype(BF16)
    w2 = l0_mlp_w2.astype(BF16)
    g_pre, g_post = _row(l0_norm_mix_pre), _row(l0_norm_mix_post)
    gf_pre, gf_post = _row(l0_norm_mlp_pre), _row(l0_norm_mlp_post)
    q0, k0, v0 = A_IN, A_IN + B_WIDTH, A_IN + 2 * B_WIDTH

    blocks = lambda *widths_coefs: [c for w, c in widths_coefs for _ in range(w // PROJ_SUB)]
    proj = _project(xs, g_pre, sc_m, sh_m, w_in, rope,
                    blocks((A_IN, PLAIN), (B_WIDTH, (Q0_SCALE, 0.0)), (B_WIDTH, (1.0, 0.0)), (B_WIDTH, PLAIN)),
                    tm=512, tn=2560, out_rows=n + nc)
    pc, proj = _project(cs, g_pre, csc_m, csh_m, w_in, rope,
                        blocks((A_IN, PLAIN), (B_WIDTH, (0.0, Q0_SCALE)), (2 * B_WIDTH, PLAIN)),
                        tm=nc, tn=1024, fill=(proj, n))
    lam_init = 0.8 - 0.6 * math.exp(-0.3 * 0)
    lam_rows = jnp.zeros((8, LANES), F32)
    for r, lv in enumerate((l0_lambda_q1, l0_lambda_k1, l0_lambda_q2, l0_lambda_k2)):
        lam_rows = lam_rows.at[r, :HEAD_DIM].set(lv)

    a_lat = _conformer_conv(proj, l0_conv_w, l0_conv_b, l0_ln_g, l0_ln_b, n=n, t=512)
    a_ctx = _conformer_conv(pc, l0_conv_w, l0_conv_b, l0_ln_g, l0_ln_b, n=nc, t=nc)
    b_lat = _diff_attention(lam_rows, l0_subln_g, proj, q0, proj, k0, v0, n_q=n, tq=1024, tk=3328,
                            lam_init=lam_init)
    b_ctx = _diff_attention(lam_rows, l0_subln_g, pc, q0, pc, k0, v0, n_q=nc, tq=nc, tk=nc, lam_init=lam_init)

    x1, hf = _out_project(a_lat, b_lat, xs, w_out, g_post, gt_m, gf_pre, sc_f, sh_f, tm=512)
    xs = _mlp(hf, w1, w2, x1, gf_post, gt_f, tm=512, tf=1024)
    c1, hcf = _out_project(a_ctx, b_ctx, cs, w_out, g_post, cgt_m, gf_pre, csc_f, csh_f, tm=nc)
    cs = _mlp(hcf, w1, w2, c1, gf_post, cgt_f, tm=nc, tf=512)

    mod = _modulation(cvecs, l1_mod_w, l1_mod_b)
    sh_m, sc_m, gt_m, sh_f, sc_f, gt_f = _split_mod(mod[0])
    csh_m, csc_m = _split_mod(mod[1])[:2]
    w_out = l1_w_out.astype(BF16)
    w1 = l1_mlp_w1.astype(BF16)
    w2 = l1_mlp_w2.astype(BF16)
    g_pre, g_post = _row(l1_norm_mix_pre), _row(l1_norm_mix_post)
    gf_pre, gf_post = _row(l1_norm_mlp_pre), _row(l1_norm_mlp_post)
    q0, k0, v0 = C_IN, C_IN + D_Q_WIDTH, C_IN + D_Q_WIDTH + D_KV_WIDTH
    w_in = jnp.concatenate([l1_w_in[:, :k0], _dup_head_cols(l1_w_in[:, k0:v0]), _dup_head_cols(l1_w_in[:, v0:])],
                           axis=1).astype(BF16)
    kd0, vd0 = k0, k0 + 2 * D_KV_WIDTH

    proj = _project(xs, g_pre, sc_m, sh_m, w_in, rope,
                    blocks((C_IN, PLAIN), (D_Q_WIDTH, (Q1_SCALE, 0.0)), (2 * D_KV_WIDTH, (1.0, 0.0)),
                           (2 * D_KV_WIDTH, PLAIN)), tm=512, tn=2560)
    pc = _project(cs, g_pre, csc_m, csh_m, w_in[:, kd0:], rope, blocks((4 * D_KV_WIDTH, PLAIN)), tm=nc, tn=1024)
    c_lat = _short_conv(proj, l1_sconv_w, t=512)
    d_lat = _window_attention(l1_sink, proj, q0, kd0, vd0, pc, tq=256)
    x1, hf = _out_project(c_lat, d_lat, xs, w_out, g_post, gt_m, gf_pre, sc_f, sh_f, tm=512)
    xs = _mlp(hf, w1, w2, x1, gf_post, gt_f, tm=512, tf=1024)
    return xs[None]
```

```python
import functools
import math

import jax
import jax.numpy as jnp
from jax import lax
from jax.experimental import pallas as pl
from jax.experimental.pallas import tpu as pltpu

F32 = jnp.float32
BF16 = jnp.bfloat16

D_MODEL = 2048
HEAD_DIM = 64
HALF = HEAD_DIM // 2
GRID_W = 64
ROPE_BASE = 10000.0
NORM_EPS = 1e-6
LN_EPS = 1e-5
ATTN_SCALE = HEAD_DIM ** -0.5
LOG2E = math.log2(math.e)
NEG_INF = -1e30
N_MOD = 6
WINDOW = 128
LANES = 128
HALO = 16

A_WIDTH = 1024
A_IN = 2 * A_WIDTH
CONV_A_WIDTH = 31
B_HEADS = 8
B_WIDTH = 1024
C_WIDTH = 1024
C_IN = 3 * C_WIDTH
D_HEADS = 16
D_KV_HEADS = 4
D_GROUP = 4
D_Q_WIDTH = 1024
D_KV_WIDTH = 256
D_FF = 4 * D_MODEL

VMEM_LIMIT = 56 * 1024 * 1024


def _params(sem):
    return pltpu.CompilerParams(dimension_semantics=sem, vmem_limit_bytes=VMEM_LIMIT)


def _rms(x, eps=NORM_EPS):
    return x * lax.rsqrt(jnp.mean(x * x, axis=-1, keepdims=True) + eps)


def _dot_nt(a, b):
    return lax.dot_general(a, b, (((1,), (1,)), ((), ())), preferred_element_type=F32)


def _mod_kernel(c_ref, w_ref, b_ref, o_ref):
    c = c_ref[...]
    s = c * jax.nn.sigmoid(c)
    o_ref[...] = jnp.dot(s, w_ref[...], preferred_element_type=F32) + b_ref[...]


def _modulation(cvecs, w, b):
    tn = 1024
    n_out = w.shape[1]
    return pl.pallas_call(
        _mod_kernel,
        grid=(n_out // tn,),
        in_specs=[pl.BlockSpec((8, D_MODEL), lambda j: (0, 0)),
                  pl.BlockSpec((D_MODEL, tn), lambda j: (0, j)),
                  pl.BlockSpec((1, tn), lambda j: (0, j))],
        out_specs=pl.BlockSpec((8, tn), lambda j: (0, j)),
        out_shape=jax.ShapeDtypeStruct((8, n_out), F32),
        compiler_params=_params(("arbitrary",)),
        name="modulation",
    )(cvecs, w, b.reshape(1, n_out))


PROJ_SUB = 512
PROJ_ROWS = 256


def _proj_kernel(coef_ref, x_ref, g_ref, sc_ref, sh_ref, tab_ref, w_ref, *rest, tn, fill):
    outs, h_scr = (rest[1:3], rest[3]) if fill else (rest[:1], rest[1])
    nsub = tn // PROJ_SUB
    j = pl.program_id(1)

    @pl.when(j == 0)
    def _():
        y = _rms(x_ref[...]) * g_ref[...]
        h_scr[...] = (y * (1.0 + sc_ref[...]) + sh_ref[...]).astype(BF16)

    lane = lax.broadcasted_iota(jnp.int32, (PROJ_ROWS, LANES), 1)
    first_half = (lane & (HEAD_DIM - 1)) < HALF
    for r in range(x_ref.shape[0] // PROJ_ROWS):
        rows = slice(r * PROJ_ROWS, (r + 1) * PROJ_ROWS)
        acc = jnp.dot(h_scr[rows, :], w_ref[...], preferred_element_type=F32)
        cosf = tab_ref[0, rows, :]
        sinf = tab_ref[1, rows, :]
        for s in range(nsub):
            alpha = coef_ref[2 * (j * nsub + s)]
            beta = coef_ref[2 * (j * nsub + s) + 1]
            a = alpha * cosf + beta
            b = alpha * sinf
            for c in range(s * PROJ_SUB // LANES, (s + 1) * PROJ_SUB // LANES):
                y = acc[:, c * LANES:(c + 1) * LANES]
                swapped = jnp.where(first_half, pltpu.roll(y, LANES - HALF, 1), pltpu.roll(y, HALF, 1))
                res = (y * a + swapped * b).astype(BF16)
                for o_ref in outs:
                    o_ref[rows, c * LANES:(c + 1) * LANES] = res


def _project(x, g, scale, shift, w, rope_tab, coefs, *, tm, tn, out_rows=None, fill=None):
    n, d = x.shape
    p = w.shape[1]
    assert len(coefs) == p // PROJ_SUB
    vec = pl.BlockSpec((1, d), lambda i, j: (0, 0))
    row_blocks = pl.cdiv(out_rows or n, tm)
    src = lambda i: jnp.minimum(i, n // tm - 1)
    in_specs = [pl.BlockSpec(memory_space=pltpu.SMEM),
                pl.BlockSpec((tm, d), lambda i, j: (src(i), 0)), vec, vec, vec,
                pl.BlockSpec((2, tm, LANES), lambda i, j: (0, src(i), 0)),
                pl.BlockSpec((d, tn), lambda i, j: (0, j))]
    args = [jnp.asarray(coefs, F32).reshape(-1), x, g, scale, shift, rope_tab, w]
    out_specs = [pl.BlockSpec((tm, tn), lambda i, j: (i, j))]
    out_shape = [jax.ShapeDtypeStruct((out_rows or n, p), BF16)]
    aliases = {}
    if fill is not None:
        buf, row0 = fill
        blk0 = row0 // tm
        in_specs.append(pl.BlockSpec(memory_space=pl.ANY))
        args.append(buf)
        out_specs.append(pl.BlockSpec((tm, tn), lambda i, j: (blk0 + i, j)))
        out_shape.append(jax.ShapeDtypeStruct(buf.shape, BF16))
        aliases = {len(args) - 1: 1}
    res = pl.pallas_call(
        functools.partial(_proj_kernel, tn=tn, fill=fill is not None),
        grid=(row_blocks, p // tn),
        in_specs=in_specs,
        out_specs=out_specs,
        out_shape=out_shape,
        input_output_aliases=aliases,
        scratch_shapes=[pltpu.VMEM((tm, d), BF16)],
        compiler_params=_params(("parallel", "arbitrary")),
        name="norm_project",
    )(*args)
    return res if fill is not None else res[0]


CONV_A_ROWS = 32


def _conv_a_kernel(main_ref, left_ref, right_ref, w_ref, cb_ref, lg_ref, lb_ref, o_ref, ext_scr, sh_scr, *, t):
    i = pl.program_id(0)
    last = pl.num_programs(0) - 1

    def glu(ref):
        v = ref[...].astype(F32)
        return v[:, :A_WIDTH] * jax.nn.sigmoid(v[:, A_WIDTH:])

    ext_scr[0:HALO, :] = jnp.where(i > 0, glu(left_ref), 0.0)
    ext_scr[HALO:HALO + t, :] = glu(main_ref)
    ext_scr[HALO + t:, :] = jnp.where(i < last, glu(right_ref), 0.0)
    ext = ext_scr[...]
    rows = t + 2 * HALO
    srows = t + 3 * 8
    sh_scr[0] = ext[:srows]
    for b in range(1, 8):
        sh_scr[b] = pltpu.roll(ext, rows - b, 0)[:srows]

    def chunk(c, carry):
        r0 = pl.multiple_of(c * CONV_A_ROWS, CONV_A_ROWS)
        acc = jnp.zeros((CONV_A_ROWS, A_WIDTH), F32)
        for k in range(CONV_A_WIDTH):
            a, b = divmod(k + 1, 8)
            acc = acc + sh_scr[b, pl.ds(r0 + 8 * a, CONV_A_ROWS), :] * w_ref[k:k + 1, :]
        v = acc + cb_ref[...]
        mu = jnp.mean(v, axis=-1, keepdims=True)
        vc = v - mu
        var = jnp.mean(vc * vc, axis=-1, keepdims=True)
        y = vc * lax.rsqrt(var + LN_EPS) * lg_ref[...] + lb_ref[...]
        o_ref[pl.ds(r0, CONV_A_ROWS), :] = (y * jax.nn.sigmoid(y)).astype(BF16)
        return carry

    lax.fori_loop(0, t // CONV_A_ROWS, chunk, 0, unroll=2)


def _conformer_conv(proj, conv_w, conv_b, ln_g, ln_b, *, n, t):
    per = t // HALO
    nh = n // HALO
    vec = pl.BlockSpec((1, A_WIDTH), lambda i: (0, 0))
    return pl.pallas_call(
        functools.partial(_conv_a_kernel, t=t),
        grid=(n // t,),
        in_specs=[pl.BlockSpec((t, A_IN), lambda i: (i, 0)),
                  pl.BlockSpec((HALO, A_IN), lambda i: (jnp.maximum(i * per - 1, 0), 0)),
                  pl.BlockSpec((HALO, A_IN), lambda i: (jnp.minimum((i + 1) * per, nh - 1), 0)),
                  pl.BlockSpec((CONV_A_WIDTH, A_WIDTH), lambda i: (0, 0)),
                  vec, vec, vec],
        out_specs=pl.BlockSpec((t, A_WIDTH), lambda i: (i, 0)),
        out_shape=jax.ShapeDtypeStruct((n, A_WIDTH), BF16),
        scratch_shapes=[pltpu.VMEM((t + 2 * HALO, A_WIDTH), F32),
                        pltpu.VMEM((8, t + 24, A_WIDTH), F32)],
        compiler_params=_params(("parallel",)),
        name="conformer_conv",
    )(proj, proj, proj, conv_w, conv_b.reshape(1, -1), ln_g.reshape(1, -1), ln_b.reshape(1, -1))


def _tile_lanes(x, reps):
    return x if reps == 1 else jnp.concatenate([x] * reps, axis=1)


def _diff_attn_kernel(lam_ref, sub_ref, q_ref, k_ref, v_ref, o_ref, qm_scr, m_scr, acc_scr, *, lam_init, tk):
    q = q_ref[...]
    lane = lax.broadcasted_iota(jnp.int32, q.shape, 1)
    zero = jnp.zeros_like(q)
    qm_scr[0] = jnp.where(lane < HEAD_DIM, q, zero)
    qm_scr[1] = jnp.where(lane >= HEAD_DIM, q, zero)
    m_scr[...] = jnp.full(m_scr.shape, -jnp.inf, F32)
    acc_scr[...] = jnp.zeros(acc_scr.shape, F32)
    ones = jnp.ones((tk, LANES), BF16)

    def chunk(c, carry):
        r0 = pl.multiple_of(c * tk, tk)
        k = k_ref[pl.ds(r0, tk), :]
        v_ext = jnp.concatenate([v_ref[pl.ds(r0, tk), :], ones], axis=1)
        scores = [_dot_nt(qm_scr[m], k) for m in range(2)]
        for m in range(2):
            s = scores[m]
            m_prev = m_scr[m]
            m_new = jnp.maximum(m_prev, jnp.max(s, axis=1, keepdims=True))
            alpha = jnp.exp2(m_prev - m_new)
            p = jnp.exp2(s - _tile_lanes(m_new, tk // LANES))
            acc_scr[m] = (_tile_lanes(alpha, 2) * acc_scr[m]
                          + jnp.dot(p.astype(BF16), v_ext, preferred_element_type=F32))
            m_scr[m] = m_new
        return carry

    lax.fori_loop(0, k_ref.shape[0] // tk, chunk, 0)

    lp = lam_ref[...]
    lam = (jnp.exp(jnp.sum(lp[0:1] * lp[1:2], axis=1, keepdims=True))
           - jnp.exp(jnp.sum(lp[2:3] * lp[3:4], axis=1, keepdims=True)) + lam_init)
    o = (acc_scr[0, :, :LANES] / acc_scr[0, :, LANES:]
         - lam * (acc_scr[1, :, :LANES] / acc_scr[1, :, LANES:]))
    o = _rms(o) * sub_ref[...] * (1.0 - lam_init)
    o_ref[...] = o.astype(BF16)


def _diff_attention(lam_rows, subln_g, q_src, q_col0, kv_src, k_col0, v_col0, *, n_q, tq, tk, lam_init):
    nk = kv_src.shape[0]
    qb, kb, vb = q_col0 // LANES, k_col0 // LANES, v_col0 // LANES
    return pl.pallas_call(
        functools.partial(_diff_attn_kernel, lam_init=lam_init, tk=tk),
        grid=(B_HEADS, n_q // tq),
        in_specs=[pl.BlockSpec((8, LANES), lambda h, i: (0, 0)),
                  pl.BlockSpec((1, LANES), lambda h, i: (0, 0)),
                  pl.BlockSpec((tq, LANES), lambda h, i: (i, qb + h)),
                  pl.BlockSpec((nk, LANES), lambda h, i: (0, kb + h)),
                  pl.BlockSpec((nk, LANES), lambda h, i: (0, vb + h))],
        out_specs=pl.BlockSpec((tq, LANES), lambda h, i: (i, h)),
        out_shape=jax.ShapeDtypeStruct((n_q, B_WIDTH), BF16),
        scratch_shapes=[pltpu.VMEM((2, tq, LANES), BF16),
                        pltpu.VMEM((2, tq, LANES), F32),
                        pltpu.VMEM((2, tq, 2 * LANES), F32)],
        compiler_params=_params(("parallel", "arbitrary")),
        name="diff_attention",
    )(lam_rows, subln_g.reshape(1, LANES), q_src, kv_src, kv_src)


def _conv_c_kernel(main_ref, left_ref, right_ref, w_ref, o_ref, ext_scr, *, t):
    i = pl.program_id(0)
    last = pl.num_programs(0) - 1

    def gated(ref):
        v = ref[...].astype(F32)
        return v[:, C_WIDTH:2 * C_WIDTH] * v[:, 2 * C_WIDTH:]

    ext_scr[0:HALO, :] = jnp.where(i > 0, gated(left_ref), 0.0)
    ext_scr[HALO:HALO + t, :] = gated(main_ref)
    ext_scr[HALO + t:, :] = jnp.where(i < last, gated(right_ref), 0.0)
    ext = ext_scr[...]
    rows = t + 2 * HALO
    prev = pltpu.roll(ext, 1, 0)[HALO:HALO + t]
    nxt = pltpu.roll(ext, rows - 1, 0)[HALO:HALO + t]
    conv = prev * w_ref[0:1, :] + ext[HALO:HALO + t] * w_ref[1:2, :] + nxt * w_ref[2:3, :]
    o_ref[...] = (main_ref[:, :C_WIDTH].astype(F32) * conv).astype(BF16)


def _short_conv(proj, w, *, t):
    n = proj.shape[0]
    per = t // HALO
    nh = n // HALO
    return pl.pallas_call(
        functools.partial(_conv_c_kernel, t=t),
        grid=(n // t,),
        in_specs=[pl.BlockSpec((t, C_IN), lambda i: (i, 0)),
                  pl.BlockSpec((HALO, C_IN), lambda i: (jnp.maximum(i * per - 1, 0), 0)),
                  pl.BlockSpec((HALO, C_IN), lambda i: (jnp.minimum((i + 1) * per, nh - 1), 0)),
                  pl.BlockSpec((3, C_WIDTH), lambda i: (0, 0))],
        out_specs=pl.BlockSpec((t, C_WIDTH), lambda i: (i, 0)),
        out_shape=jax.ShapeDtypeStruct((n, C_WIDTH), BF16),
        scratch_shapes=[pltpu.VMEM((t + 2 * HALO, C_WIDTH), F32)],
        compiler_params=_params(("parallel",)),
        name="short_conv",
    )(proj, proj, proj, w)


def _win_attn_kernel(sink_ref, q_ref, kp_ref, km_ref, kn_ref, vp_ref, vm_ref, vn_ref, c_ref, o_ref, *, tq, n):
    i = pl.program_id(0)
    lane = lax.broadcasted_iota(jnp.int32, (tq, LANES), 1)
    low = lane < HEAD_DIM

    def with_ones(v, keep_low):
        col = lax.broadcasted_iota(jnp.int32, v.shape, 1)
        return jnp.where((col < HEAD_DIM) if keep_low else (col >= HEAD_DIM), v, jnp.ones_like(v))

    nb = tq // WINDOW
    span = 3 * WINDOW
    row = lax.broadcasted_iota(jnp.int32, (WINDOW, span), 0)
    col = lax.broadcasted_iota(jnp.int32, (WINDOW, span), 1)
    band = (col >= row) & (col <= row + 2 * WINDOW)
    masks = []
    for blk in range(nb):
        pos = i * tq + (blk - 1) * WINDOW + col
        masks.append(band & (pos >= 0) & (pos < n))
    zeros = jnp.zeros((WINDOW, WINDOW), BF16)

    def scores(kv):
        tile = slice(kv * LANES, (kv + 1) * LANES)
        kw = jnp.concatenate([kp_ref[:, tile], km_ref[:, tile], kn_ref[:, tile]], axis=0)
        kc = c_ref[:, tile]
        qms = []
        for g in range(D_GROUP):
            c0 = (kv * D_GROUP + g) // 2 * LANES
            qa = q_ref[:, c0:c0 + LANES]
            qms.append(jnp.where(low if g % 2 == 0 else (lane >= HEAD_DIM), qa, jnp.zeros_like(qa)))
        q_all = jnp.concatenate(qms, axis=0)
        return _dot_nt(q_all, kc), _dot_nt(q_all, kw)

    def softmax(kv, s_c_all, s_w_all):
        probs = []
        for g in range(D_GROUP):
            sk = sink_ref[kv * D_GROUP + g] * LOG2E
            p_cs, p_ws, sinks = [], [], []
            for blk in range(nb):
                rows = slice(g * tq + blk * WINDOW, g * tq + (blk + 1) * WINDOW)
                s_c = s_c_all[rows]
                s_w = jnp.where(masks[blk], s_w_all[rows, blk * WINDOW:blk * WINDOW + span], NEG_INF)
                mx = jnp.maximum(jnp.maximum(jnp.max(s_c, axis=1, keepdims=True),
                                             jnp.max(s_w, axis=1, keepdims=True)), sk)
                p_cs.append(jnp.exp2(s_c - mx).astype(BF16))
                p_ws.append(jnp.concatenate([zeros] * blk + [jnp.exp2(s_w - mx).astype(BF16)]
                                            + [zeros] * (nb - 1 - blk), axis=1))
                sinks.append(jnp.exp2(sk - mx))
            probs.append((jnp.concatenate(p_cs, axis=0), jnp.concatenate(p_ws, axis=0),
                          jnp.concatenate(sinks, axis=0)))
        return probs

    def outputs(kv, probs):
        tile = slice(kv * LANES, (kv + 1) * LANES)
        vw = jnp.concatenate([vp_ref[:, tile], vm_ref[:, tile], vn_ref[:, tile]], axis=0)
        vc = c_ref[:, D_KV_HEADS * LANES + kv * LANES:D_KV_HEADS * LANES + (kv + 1) * LANES]
        outs = [None] * D_GROUP
        for parity in range(2):
            heads = list(range(parity, D_GROUP, 2))
            p_c = jnp.concatenate([probs[g][0] for g in heads], axis=0)
            p_w = jnp.concatenate([probs[g][1] for g in heads], axis=0)
            p_sink = jnp.concatenate([probs[g][2] for g in heads], axis=0)
            o = (jnp.dot(p_c, with_ones(vc, parity == 0), preferred_element_type=F32)
                 + jnp.dot(p_w, with_ones(vw, parity == 0), preferred_element_type=F32))
            o = o / (pltpu.roll(o, HEAD_DIM, 1) + p_sink)
            for idx, g in enumerate(heads):
                outs[g] = o[idx * tq:(idx + 1) * tq]
        for a in range(D_GROUP // 2):
            c0 = (kv * D_GROUP // 2 + a) * LANES
            o_ref[:, c0:c0 + LANES] = jnp.where(low, outs[2 * a], outs[2 * a + 1]).astype(BF16)

    for kv in range(D_KV_HEADS):
        outputs(kv, softmax(kv, *scores(kv)))


def _window_attention(sink, proj, q_col0, k_col0, v_col0, pc, *, tq):
    n = proj.shape[0]
    nc = pc.shape[0]
    kvw = D_KV_HEADS * LANES
    per = tq // WINDOW
    nw = n // WINDOW

    def window(col0):
        cb = col0 // kvw
        return [pl.BlockSpec((WINDOW, kvw), lambda i: (jnp.maximum(i * per - 1, 0), cb)),
                pl.BlockSpec((tq, kvw), lambda i: (i, cb)),
                pl.BlockSpec((WINDOW, kvw), lambda i: (jnp.minimum((i + 1) * per, nw - 1), cb))]

    return pl.pallas_call(
        functools.partial(_win_attn_kernel, tq=tq, n=n),
        grid=(n // tq,),
        in_specs=[pl.BlockSpec(memory_space=pltpu.SMEM),
                  pl.BlockSpec((tq, D_Q_WIDTH), lambda i: (i, q_col0 // D_Q_WIDTH))]
                 + window(k_col0) + window(v_col0)
                 + [pl.BlockSpec((nc, 2 * kvw), lambda i: (0, 0))],
        out_specs=pl.BlockSpec((tq, D_Q_WIDTH), lambda i: (i, 0)),
        out_shape=jax.ShapeDtypeStruct((n, D_Q_WIDTH), BF16),
        compiler_params=_params(("parallel",)),
        name="window_attention",
    )(sink, proj, proj, proj, proj, proj, proj, proj, pc)


def _dup_head_cols(w):
    d = w.shape[0]
    w = w.reshape(d, D_KV_HEADS, 1, HEAD_DIM)
    return jnp.broadcast_to(w, (d, D_KV_HEADS, 2, HEAD_DIM)).reshape(d, D_KV_HEADS * LANES)


OUT_ROWS = 256


def _out_kernel(a_ref, b_ref, x_ref, w_ref, gpost_ref, gate_ref, gpre_ref, sc_ref, sh_ref, x1_ref, hf_ref):
    half = a_ref.shape[1]
    for r in range(x_ref.shape[0] // OUT_ROWS):
        rows = slice(r * OUT_ROWS, (r + 1) * OUT_ROWS)
        y = (jnp.dot(a_ref[rows, :], w_ref[:half, :], preferred_element_type=F32)
             + jnp.dot(b_ref[rows, :], w_ref[half:, :], preferred_element_type=F32))
        x1 = x_ref[rows, :] + gate_ref[...] * (_rms(y) * gpost_ref[...])
        x1_ref[rows, :] = x1
        hf_ref[rows, :] = ((_rms(x1) * gpre_ref[...]) * (1.0 + sc_ref[...]) + sh_ref[...]).astype(BF16)


def _out_project(a, b, x, w, g_post, gate, g_pre, scale, shift, *, tm):
    n, d = x.shape
    half = a.shape[1]
    vec = pl.BlockSpec((1, d), lambda i: (0, 0))
    return pl.pallas_call(
        _out_kernel,
        grid=(n // tm,),
        in_specs=[pl.BlockSpec((tm, half), lambda i: (i, 0)),
                  pl.BlockSpec((tm, half), lambda i: (i, 0)),
                  pl.BlockSpec((tm, d), lambda i: (i, 0)),
                  pl.BlockSpec((2 * half, d), lambda i: (0, 0), pipeline_mode=pl.Buffered(1)),
                  vec, vec, vec, vec, vec],
        out_specs=[pl.BlockSpec((tm, d), lambda i: (i, 0)), pl.BlockSpec((tm, d), lambda i: (i, 0))],
        out_shape=[jax.ShapeDtypeStruct((n, d), F32), jax.ShapeDtypeStruct((n, d), BF16)],
        compiler_params=_params(("parallel",)),
        name="out_project",
    )(a, b, x, w, g_post, gate, g_pre, scale, shift)


def _mlp_kernel(hf_ref, w1_ref, w2_ref, x1_ref, gpost_ref, gate_ref, o_ref, acc_scr):
    f = pl.program_id(1)

    @pl.when(f == 0)
    def _():
        acc_scr[...] = jnp.zeros(acc_scr.shape, F32)

    h = jnp.dot(hf_ref[...], w1_ref[...], preferred_element_type=F32)
    h = jnp.square(jnp.maximum(h, 0.0)).astype(BF16)
    acc_scr[...] += jnp.dot(h, w2_ref[...], preferred_element_type=F32)

    @pl.when(f == pl.num_programs(1) - 1)
    def _():
        o_ref[...] = x1_ref[...] + gate_ref[...] * (_rms(acc_scr[...]) * gpost_ref[...])


def _mlp(hf, w1, w2, x1, g_post, gate, *, tm, tf):
    n, d = x1.shape
    ff = w1.shape[1]
    vec = pl.BlockSpec((1, d), lambda i, f: (0, 0))
    return pl.pallas_call(
        _mlp_kernel,
        grid=(n // tm, ff // tf),
        in_specs=[pl.BlockSpec((tm, d), lambda i, f: (i, 0)),
                  pl.BlockSpec((d, tf), lambda i, f: (0, f)),
                  pl.BlockSpec((tf, d), lambda i, f: (f, 0)),
                  pl.BlockSpec((tm, d), lambda i, f: (i, 0)),
                  vec, vec],
        out_specs=pl.BlockSpec((tm, d), lambda i, f: (i, 0)),
        out_shape=jax.ShapeDtypeStruct((n, d), F32),
        scratch_shapes=[pltpu.VMEM((tm, d), F32)],
        compiler_params=_params(("parallel", "arbitrary")),
        name="mlp",
    )(hf, w1, w2, x1, g_post, gate)


Q0_SCALE = ATTN_SCALE * LOG2E
Q1_SCALE = ATTN_SCALE * LOG2E
PLAIN = (0.0, 1.0)


def _rope_table(n):
    rows = n // GRID_W
    pairs = HEAD_DIM // 4
    inv = jnp.power(ROPE_BASE, -jnp.arange(pairs, dtype=F32) / pairs)
    ang_r = jnp.arange(rows, dtype=F32)[:, None] * inv
    ang_c = jnp.arange(GRID_W, dtype=F32)[:, None] * inv

    def grid(f):
        r = jnp.broadcast_to(f(ang_r)[:, None, :], (rows, GRID_W, pairs))
        c = jnp.broadcast_to(f(ang_c)[None, :, :], (rows, GRID_W, pairs))
        return jnp.concatenate([r, c], axis=-1).reshape(n, 2 * pairs)

    cos, sin = grid(jnp.cos), grid(jnp.sin)
    return jnp.stack([jnp.tile(cos, (1, 4)), jnp.tile(jnp.concatenate([-sin, sin], axis=-1), (1, 2))])


def _row(v):
    return v.reshape(1, -1)


def _split_mod(m):
    return [_row(t) for t in jnp.split(m, N_MOD)]


def kernel(x, c, ctx, c_ctx, l0_mod_w, l0_mod_b, l0_norm_mix_pre, l0_norm_mix_post, l0_norm_mlp_pre, l0_norm_mlp_post, l0_w_in, l0_conv_w, l0_conv_b, l0_ln_g, l0_ln_b, l0_lambda_q1, l0_lambda_k1, l0_lambda_q2, l0_lambda_k2, l0_subln_g, l0_w_out, l0_mlp_w1, l0_mlp_w2, l1_mod_w, l1_mod_b, l1_norm_mix_pre, l1_norm_mix_post, l1_norm_mlp_pre, l1_norm_mlp_post, l1_w_in, l1_sconv_w, l1_sink, l1_w_out, l1_mlp_w1, l1_mlp_w2):
    n = x.shape[1]
    nc = ctx.shape[1]
    xs = x[0]
    cs = ctx[0]
    rope = _rope_table(n)
    cvecs = jnp.zeros((8, D_MODEL), F32).at[0].set(c[0]).at[1].set(c_ctx)

    mod = _modulation(cvecs, l0_mod_w, l0_mod_b)
    sh_m, sc_m, gt_m, sh_f, sc_f, gt_f = _split_mod(mod[0])
    csh_m, csc_m, cgt_m, csh_f, csc_f, cgt_f = _split_mod(mod[1])
    w_in = l0_w_in.astype(BF16)
    w_out = l0_w_out.astype(BF16)
    w1 = l0_mlp_w1.astype(BF16)
    w2 = l0_mlp_w2.astype(BF16)
    g_pre, g_post = _row(l0_norm_mix_pre), _row(l0_norm_mix_post)
    gf_pre, gf_post = _row(l0_norm_mlp_pre), _row(l0_norm_mlp_post)
    q0, k0, v0 = A_IN, A_IN + B_WIDTH, A_IN + 2 * B_WIDTH

    blocks = lambda *widths_coefs: [c for w, c in widths_coefs for _ in range(w // PROJ_SUB)]
    proj = _project(xs, g_pre, sc_m, sh_m, w_in, rope,
                    blocks((A_IN, PLAIN), (B_WIDTH, (Q0_SCALE, 0.0)), (B_WIDTH, (1.0, 0.0)), (B_WIDTH, PLAIN)),
                    tm=512, tn=2560, out_rows=n + nc)
    pc, proj = _project(cs, g_pre, csc_m, csh_m, w_in, rope,
                        blocks((A_IN, PLAIN), (B_WIDTH, (0.0, Q0_SCALE)), (2 * B_WIDTH, PLAIN)),
                        tm=nc, tn=1024, fill=(proj, n))
    lam_init = 0.8 - 0.6 * math.exp(-0.3 * 0)
    lam_rows = jnp.zeros((8, LANES), F32)
    for r, lv in enumerate((l0_lambda_q1, l0_lambda_k1, l0_lambda_q2, l0_lambda_k2)):
        lam_rows = lam_rows.at[r, :HEAD_DIM].set(lv)

    a_lat = _conformer_conv(proj, l0_conv_w, l0_conv_b, l0_ln_g, l0_ln_b, n=n, t=512)
    a_ctx = _conformer_conv(pc, l0_conv_w, l0_conv_b, l0_ln_g, l0_ln_b, n=nc, t=nc)
    b_lat = _diff_attention(lam_rows, l0_subln_g, proj, q0, proj, k0, v0, n_q=n, tq=1024, tk=3328,
                            lam_init=lam_init)
    b_ctx = _diff_attention(lam_rows, l0_subln_g, pc, q0, pc, k0, v0, n_q=nc, tq=nc, tk=nc, lam_init=lam_init)

    x1, hf = _out_project(a_lat, b_lat, xs, w_out, g_post, gt_m, gf_pre, sc_f, sh_f, tm=512)
    xs = _mlp(hf, w1, w2, x1, gf_post, gt_f, tm=512, tf=1024)
    c1, hcf = _out_project(a_ctx, b_ctx, cs, w_out, g_post, cgt_m, gf_pre, csc_f, csh_f, tm=nc)
    cs = _mlp(hcf, w1, w2, c1, gf_post, cgt_f, tm=nc, tf=512)

    mod = _modulation(cvecs, l1_mod_w, l1_mod_b)
    sh_m, sc_m, gt_m, sh_f, sc_f, gt_f = _split_mod(mod[0])
    csh_m, csc_m = _split_mod(mod[1])[:2]
    w_out = l1_w_out.astype(BF16)
    w1 = l1_mlp_w1.astype(BF16)
    w2 = l1_mlp_w2.astype(BF16)
    g_pre, g_post = _row(l1_norm_mix_pre), _row(l1_norm_mix_post)
    gf_pre, gf_post = _row(l1_norm_mlp_pre), _row(l1_norm_mlp_post)
    q0, k0, v0 = C_IN, C_IN + D_Q_WIDTH, C_IN + D_Q_WIDTH + D_KV_WIDTH
    w_in = jnp.concatenate([l1_w_in[:, :k0], _dup_head_cols(l1_w_in[:, k0:v0]), _dup_head_cols(l1_w_in[:, v0:])],
                           axis=1).astype(BF16)
    kd0, vd0 = k0, k0 + 2 * D_KV_WIDTH

    proj = _project(xs, g_pre, sc_m, sh_m, w_in, rope,
                    blocks((C_IN, PLAIN), (D_Q_WIDTH, (Q1_SCALE, 0.0)), (2 * D_KV_WIDTH, (1.0, 0.0)),
                           (2 * D_KV_WIDTH, PLAIN)), tm=512, tn=2560)
    pc = _project(cs, g_pre, csc_m, csh_m, w_in[:, kd0:], rope, blocks((4 * D_KV_WIDTH, PLAIN)), tm=nc, tn=1024)
    c_lat = _short_conv(proj, l1_sconv_w, t=512)
    d_lat = _window_attention(l1_sink, proj, q0, kd0, vd0, pc, tq=256)
    x1, hf = _out_project(c_lat, d_lat, xs, w_out, g_post, gt_m, gf_pre, sc_f, sh_f, tm=512)
    xs = _mlp(hf, w1, w2, x1, gf_post, gt_f, tm=512, tf=1024)
    return xs[None]
```

```python
import functools
import math

import jax
import jax.numpy as jnp
from jax import lax
from jax.experimental import pallas as pl
from jax.experimental.pallas import tpu as pltpu

F32 = jnp.float32
BF16 = jnp.bfloat16

D_MODEL = 2048
HEAD_DIM = 64
HALF = HEAD_DIM // 2
GRID_W = 64
ROPE_BASE = 10000.0
NORM_EPS = 1e-6
LN_EPS = 1e-5
ATTN_SCALE = HEAD_DIM ** -0.5
LOG2E = math.log2(math.e)
NEG_INF = -1e30
N_MOD = 6
WINDOW = 128
LANES = 128
HALO = 16

A_WIDTH = 1024
A_IN = 2 * A_WIDTH
CONV_A_WIDTH = 31
B_HEADS = 8
B_WIDTH = 1024
C_WIDTH = 1024
C_IN = 3 * C_WIDTH
D_HEADS = 16
D_KV_HEADS = 4
D_GROUP = 4
D_Q_WIDTH = 1024
D_KV_WIDTH = 256
D_FF = 4 * D_MODEL

VMEM_LIMIT = 56 * 1024 * 1024


def _params(sem):
    return pltpu.CompilerParams(dimension_semantics=sem, vmem_limit_bytes=VMEM_LIMIT)


def _rms(x, eps=NORM_EPS):
    return x * lax.rsqrt(jnp.mean(x * x, axis=-1, keepdims=True) + eps)


def _dot_nt(a, b):
    return lax.dot_general(a, b, (((1,), (1,)), ((), ())), preferred_element_type=F32)


def _mod_kernel(c_ref, w_ref, b_ref, o_ref):
    c = c_ref[...]
    s = c * jax.nn.sigmoid(c)
    o_ref[...] = jnp.dot(s, w_ref[...], preferred_element_type=F32) + b_ref[...]


def _modulation(cvecs, w, b):
    tn = 1024
    n_out = w.shape[1]
    return pl.pallas_call(
        _mod_kernel,
        grid=(n_out // tn,),
        in_specs=[pl.BlockSpec((8, D_MODEL), lambda j: (0, 0)),
                  pl.BlockSpec((D_MODEL, tn), lambda j: (0, j)),
                  pl.BlockSpec((1, tn), lambda j: (0, j))],
        out_specs=pl.BlockSpec((8, tn), lambda j: (0, j)),
        out_shape=jax.ShapeDtypeStruct((8, n_out), F32),
        compiler_params=_params(("arbitrary",)),
        name="modulation",
    )(cvecs, w, b.reshape(1, n_out))


PROJ_SUB = 512
PROJ_ROWS = 256


def _proj_kernel(coef_ref, x_ref, g_ref, sc_ref, sh_ref, tab_ref, w_ref, *rest, tn, fill):
    outs, h_scr = (rest[1:3], rest[3]) if fill else (rest[:1], rest[1])
    nsub = tn // PROJ_SUB
    j = pl.program_id(1)

    @pl.when(j == 0)
    def _():
        y = _rms(x_ref[...]) * g_ref[...]
        h_scr[...] = (y * (1.0 + sc_ref[...]) + sh_ref[...]).astype(BF16)

    lane = lax.broadcasted_iota(jnp.int32, (PROJ_ROWS, LANES), 1)
    first_half = (lane & (HEAD_DIM - 1)) < HALF
    for r in range(x_ref.shape[0] // PROJ_ROWS):
        rows = slice(r * PROJ_ROWS, (r + 1) * PROJ_ROWS)
        acc = jnp.dot(h_scr[rows, :], w_ref[...], preferred_element_type=F32)
        cosf = tab_ref[0, rows, :]
        sinf = tab_ref[1, rows, :]
        for s in range(nsub):
            alpha = coef_ref[2 * (j * nsub + s)]
            beta = coef_ref[2 * (j * nsub + s) + 1]
            a = alpha * cosf + beta
            b = alpha * sinf
            for c in range(s * PROJ_SUB // LANES, (s + 1) * PROJ_SUB // LANES):
                y = acc[:, c * LANES:(c + 1) * LANES]
                swapped = jnp.where(first_half, pltpu.roll(y, LANES - HALF, 1), pltpu.roll(y, HALF, 1))
                res = (y * a + swapped * b).astype(BF16)
                for o_ref in outs:
                    o_ref[rows, c * LANES:(c + 1) * LANES] = res


def _project(x, g, scale, shift, w, rope_tab, coefs, *, tm, tn, out_rows=None, fill=None):
    n, d = x.shape
    p = w.shape[1]
    assert len(coefs) == p // PROJ_SUB
    vec = pl.BlockSpec((1, d), lambda i, j: (0, 0))
    row_blocks = pl.cdiv(out_rows or n, tm)
    src = lambda i: jnp.minimum(i, n // tm - 1)
    in_specs = [pl.BlockSpec(memory_space=pltpu.SMEM),
                pl.BlockSpec((tm, d), lambda i, j: (src(i), 0)), vec, vec, vec,
                pl.BlockSpec((2, tm, LANES), lambda i, j: (0, src(i), 0)),
                pl.BlockSpec((d, tn), lambda i, j: (0, j))]
    args = [jnp.asarray(coefs, F32).reshape(-1), x, g, scale, shift, rope_tab, w]
    out_specs = [pl.BlockSpec((tm, tn), lambda i, j: (i, j))]
    out_shape = [jax.ShapeDtypeStruct((out_rows or n, p), BF16)]
    aliases = {}
    if fill is not None:
        buf, row0 = fill
        blk0 = row0 // tm
        in_specs.append(pl.BlockSpec(memory_space=pl.ANY))
        args.append(buf)
        out_specs.append(pl.BlockSpec((tm, tn), lambda i, j: (blk0 + i, j)))
        out_shape.append(jax.ShapeDtypeStruct(buf.shape, BF16))
        aliases = {len(args) - 1: 1}
    res = pl.pallas_call(
        functools.partial(_proj_kernel, tn=tn, fill=fill is not None),
        grid=(row_blocks, p // tn),
        in_specs=in_specs,
        out_specs=out_specs,
        out_shape=out_shape,
        input_output_aliases=aliases,
        scratch_shapes=[pltpu.VMEM((tm, d), BF16)],
        compiler_params=_params(("parallel", "arbitrary")),
        name="norm_project",
    )(*args)
    return res if fill is not None else res[0]


CONV_A_ROWS = 32


def _conv_a_kernel(main_ref, left_ref, right_ref, w_ref, cb_ref, lg_ref, lb_ref, o_ref, ext_scr, sh_scr, *, t):
    i = pl.program_id(0)
    last = pl.num_programs(0) - 1

    def glu(ref):
        v = ref[...].astype(F32)
        return v[:, :A_WIDTH] * jax.nn.sigmoid(v[:, A_WIDTH:])

    ext_scr[0:HALO, :] = jnp.where(i > 0, glu(left_ref), 0.0)
    ext_scr[HALO:HALO + t, :] = glu(main_ref)
    ext_scr[HALO + t:, :] = jnp.where(i < last, glu(right_ref), 0.0)
    ext = ext_scr[...]
    rows = t + 2 * HALO
    srows = t + 3 * 8
    sh_scr[0] = ext[:srows]
    for b in range(1, 8):
        sh_scr[b] = pltpu.roll(ext, rows - b, 0)[:srows]

    def chunk(c, carry):
        r0 = pl.multiple_of(c * CONV_A_ROWS, CONV_A_ROWS)
        acc = jnp.zeros((CONV_A_ROWS, A_WIDTH), F32)
        for k in range(CONV_A_WIDTH):
            a, b = divmod(k + 1, 8)
            acc = acc + sh_scr[b, pl.ds(r0 + 8 * a, CONV_A_ROWS), :] * w_ref[k:k + 1, :]
        v = acc + cb_ref[...]
        mu = jnp.mean(v, axis=-1, keepdims=True)
        vc = v - mu
        var = jnp.mean(vc * vc, axis=-1, keepdims=True)
        y = vc * lax.rsqrt(var + LN_EPS) * lg_ref[...] + lb_ref[...]
        o_ref[pl.ds(r0, CONV_A_ROWS), :] = (y * jax.nn.sigmoid(y)).astype(BF16)
        return carry

    lax.fori_loop(0, t // CONV_A_ROWS, chunk, 0, unroll=2)


def _conformer_conv(proj, conv_w, conv_b, ln_g, ln_b, *, n, t):
    per = t // HALO
    nh = n // HALO
    vec = pl.BlockSpec((1, A_WIDTH), lambda i: (0, 0))
    return pl.pallas_call(
        functools.partial(_conv_a_kernel, t=t),
        grid=(n // t,),
        in_specs=[pl.BlockSpec((t, A_IN), lambda i: (i, 0)),
                  pl.BlockSpec((HALO, A_IN), lambda i: (jnp.maximum(i * per - 1, 0), 0)),
                  pl.BlockSpec((HALO, A_IN), lambda i: (jnp.minimum((i + 1) * per, nh - 1), 0)),
                  pl.BlockSpec((CONV_A_WIDTH, A_WIDTH), lambda i: (0, 0)),
                  vec, vec, vec],
        out_specs=pl.BlockSpec((t, A_WIDTH), lambda i: (i, 0)),
        out_shape=jax.ShapeDtypeStruct((n, A_WIDTH), BF16),
        scratch_shapes=[pltpu.VMEM((t + 2 * HALO, A_WIDTH), F32),
                        pltpu.VMEM((8, t + 24, A_WIDTH), F32)],
        compiler_params=_params(("parallel",)),
        name="conformer_conv",
    )(proj, proj, proj, conv_w, conv_b.reshape(1, -1), ln_g.reshape(1, -1), ln_b.reshape(1, -1))


def _tile_lanes(x, reps):
    return x if reps == 1 else jnp.concatenate([x] * reps, axis=1)


def _diff_attn_kernel(lam_ref, sub_ref, q_ref, k_ref, v_ref, o_ref, qm_scr, m_scr, acc_scr, *, lam_init, tk):
    q = q_ref[...]
    lane = lax.broadcasted_iota(jnp.int32, q.shape, 1)
    zero = jnp.zeros_like(q)
    qm_scr[0] = jnp.where(lane < HEAD_DIM, q, zero)
    qm_scr[1] = jnp.where(lane >= HEAD_DIM, q, zero)
    m_scr[...] = jnp.full(m_scr.shape, -jnp.inf, F32)
    acc_scr[...] = jnp.zeros(acc_scr.shape, F32)
    ones = jnp.ones((tk, LANES), BF16)

    def chunk(c, carry):
        r0 = pl.multiple_of(c * tk, tk)
        k = k_ref[pl.ds(r0, tk), :]
        v_ext = jnp.concatenate([v_ref[pl.ds(r0, tk), :], ones], axis=1)
        scores = [_dot_nt(qm_scr[m], k) for m in range(2)]
        for m in range(2):
            s = scores[m]
            m_prev = m_scr[m]
            m_new = jnp.maximum(m_prev, jnp.max(s, axis=1, keepdims=True))
            alpha = jnp.exp2(m_prev - m_new)
            p = jnp.exp2(s - _tile_lanes(m_new, tk // LANES))
            acc_scr[m] = (_tile_lanes(alpha, 2) * acc_scr[m]
                          + jnp.dot(p.astype(BF16), v_ext, preferred_element_type=F32))
            m_scr[m] = m_new
        return carry

    lax.fori_loop(0, k_ref.shape[0] // tk, chunk, 0)

    lp = lam_ref[...]
    lam = (jnp.exp(jnp.sum(lp[0:1] * lp[1:2], axis=1, keepdims=True))
           - jnp.exp(jnp.sum(lp[2:3] * lp[3:4], axis=1, keepdims=True)) + lam_init)
    o = (acc_scr[0, :, :LANES] / acc_scr[0, :, LANES:]
         - lam * (acc_scr[1, :, :LANES] / acc_scr[1, :, LANES:]))
    o = _rms(o) * sub_ref[...] * (1.0 - lam_init)
    o_ref[...] = o.astype(BF16)


def _diff_attention(lam_rows, subln_g, q_src, q_col0, kv_src, k_col0, v_col0, *, n_q, tq, tk, lam_init):
    nk = kv_src.shape[0]
    qb, kb, vb = q_col0 // LANES, k_col0 // LANES, v_col0 // LANES
    return pl.pallas_call(
        functools.partial(_diff_attn_kernel, lam_init=lam_init, tk=tk),
        grid=(B_HEADS, n_q // tq),
        in_specs=[pl.BlockSpec((8, LANES), lambda h, i: (0, 0)),
                  pl.BlockSpec((1, LANES), lambda h, i: (0, 0)),
                  pl.BlockSpec((tq, LANES), lambda h, i: (i, qb + h)),
                  pl.BlockSpec((nk, LANES), lambda h, i: (0, kb + h)),
                  pl.BlockSpec((nk, LANES), lambda h, i: (0, vb + h))],
        out_specs=pl.BlockSpec((tq, LANES), lambda h, i: (i, h)),
        out_shape=jax.ShapeDtypeStruct((n_q, B_WIDTH), BF16),
        scratch_shapes=[pltpu.VMEM((2, tq, LANES), BF16),
                        pltpu.VMEM((2, tq, LANES), F32),
                        pltpu.VMEM((2, tq, 2 * LANES), F32)],
        compiler_params=_params(("parallel", "arbitrary")),
        name="diff_attention",
    )(lam_rows, subln_g.reshape(1, LANES), q_src, kv_src, kv_src)


def _conv_c_kernel(main_ref, left_ref, right_ref, w_ref, o_ref, ext_scr, *, t):
    i = pl.program_id(0)
    last = pl.num_programs(0) - 1

    def gated(ref):
        v = ref[...].astype(F32)
        return v[:, C_WIDTH:2 * C_WIDTH] * v[:, 2 * C_WIDTH:]

    ext_scr[0:HALO, :] = jnp.where(i > 0, gated(left_ref), 0.0)
    ext_scr[HALO:HALO + t, :] = gated(main_ref)
    ext_scr[HALO + t:, :] = jnp.where(i < last, gated(right_ref), 0.0)
    ext = ext_scr[...]
    rows = t + 2 * HALO
    prev = pltpu.roll(ext, 1, 0)[HALO:HALO + t]
    nxt = pltpu.roll(ext, rows - 1, 0)[HALO:HALO + t]
    conv = prev * w_ref[0:1, :] + ext[HALO:HALO + t] * w_ref[1:2, :] + nxt * w_ref[2:3, :]
    o_ref[...] = (main_ref[:, :C_WIDTH].astype(F32) * conv).astype(BF16)


def _short_conv(proj, w, *, t):
    n = proj.shape[0]
    per = t // HALO
    nh = n // HALO
    return pl.pallas_call(
        functools.partial(_conv_c_kernel, t=t),
        grid=(n // t,),
        in_specs=[pl.BlockSpec((t, C_IN), lambda i: (i, 0)),
                  pl.BlockSpec((HALO, C_IN), lambda i: (jnp.maximum(i * per - 1, 0), 0)),
                  pl.BlockSpec((HALO, C_IN), lambda i: (jnp.minimum((i + 1) * per, nh - 1), 0)),
                  pl.BlockSpec((3, C_WIDTH), lambda i: (0, 0))],
        out_specs=pl.BlockSpec((t, C_WIDTH), lambda i: (i, 0)),
        out_shape=jax.ShapeDtypeStruct((n, C_WIDTH), BF16),
        scratch_shapes=[pltpu.VMEM((t + 2 * HALO, C_WIDTH), F32)],
        compiler_params=_params(("parallel",)),
        name="short_conv",
    )(proj, proj, proj, w)


def _win_attn_kernel(sink_ref, q_ref, kp_ref, km_ref, kn_ref, vp_ref, vm_ref, vn_ref, c_ref, o_ref, *, tq, n):
    i = pl.program_id(0)
    lane = lax.broadcasted_iota(jnp.int32, (tq, LANES), 1)
    low = lane < HEAD_DIM

    def with_ones(v, keep_low):
        col = lax.broadcasted_iota(jnp.int32, v.shape, 1)
        return jnp.where((col < HEAD_DIM) if keep_low else (col >= HEAD_DIM), v, jnp.ones_like(v))

    nb = tq // WINDOW
    span = 3 * WINDOW
    row = lax.broadcasted_iota(jnp.int32, (WINDOW, span), 0)
    col = lax.broadcasted_iota(jnp.int32, (WINDOW, span), 1)
    band = (col >= row) & (col <= row + 2 * WINDOW)
    masks = []
    for blk in range(nb):
        pos = i * tq + (blk - 1) * WINDOW + col
        masks.append(band & (pos >= 0) & (pos < n))
    zeros = jnp.zeros((WINDOW, WINDOW), BF16)

    def scores(kv):
        tile = slice(kv * LANES, (kv + 1) * LANES)
        kw = jnp.concatenate([kp_ref[:, tile], km_ref[:, tile], kn_ref[:, tile]], axis=0)
        kc = c_ref[:, tile]
        qms = []
        for g in range(D_GROUP):
            c0 = (kv * D_GROUP + g) // 2 * LANES
            qa = q_ref[:, c0:c0 + LANES]
            qms.append(jnp.where(low if g % 2 == 0 else (lane >= HEAD_DIM), qa, jnp.zeros_like(qa)))
        q_all = jnp.concatenate(qms, axis=0)
        return _dot_nt(q_all, kc), _dot_nt(q_all, kw)

    def softmax(kv, s_c_all, s_w_all):
        probs = []
        for g in range(D_GROUP):
            sk = sink_ref[kv * D_GROUP + g] * LOG2E
            p_cs, p_ws, sinks = [], [], []
            for blk in range(nb):
                rows = slice(g * tq + blk * WINDOW, g * tq + (blk + 1) * WINDOW)
                s_c = s_c_all[rows]
                s_w = jnp.where(masks[blk], s_w_all[rows, blk * WINDOW:blk * WINDOW + span], NEG_INF)
                mx = jnp.maximum(jnp.maximum(jnp.max(s_c, axis=1, keepdims=True),
                                             jnp.max(s_w, axis=1, keepdims=True)), sk)
                p_cs.append(jnp.exp2(s_c - mx).astype(BF16))
                p_ws.append(jnp.concatenate([zeros] * blk + [jnp.exp2(s_w - mx).astype(BF16)]
                                            + [zeros] * (nb - 1 - blk), axis=1))
                sinks.append(jnp.exp2(sk - mx))
            probs.append((jnp.concatenate(p_cs, axis=0), jnp.concatenate(p_ws, axis=0),
                          jnp.concatenate(sinks, axis=0)))
        return probs

    def outputs(kv, probs):
        tile = slice(kv * LANES, (kv + 1) * LANES)
        vw = jnp.concatenate([vp_ref[:, tile], vm_ref[:, tile], vn_ref[:, tile]], axis=0)
        vc = c_ref[:, D_KV_HEADS * LANES + kv * LANES:D_KV_HEADS * LANES + (kv + 1) * LANES]
        outs = [None] * D_GROUP
        for parity in range(2):
            heads = list(range(parity, D_GROUP, 2))
            p_c = jnp.concatenate([probs[g][0] for g in heads], axis=0)
            p_w = jnp.concatenate([probs[g][1] for g in heads], axis=0)
            p_sink = jnp.concatenate([probs[g][2] for g in heads], axis=0)
            o = (jnp.dot(p_c, with_ones(vc, parity == 0), preferred_element_type=F32)
                 + jnp.dot(p_w, with_ones(vw, parity == 0), preferred_element_type=F32))
            o = o / (pltpu.roll(o, HEAD_DIM, 1) + p_sink)
            for idx, g in enumerate(heads):
                outs[g] = o[idx * tq:(idx + 1) * tq]
        for a in range(D_GROUP // 2):
            c0 = (kv * D_GROUP // 2 + a) * LANES
            o_ref[:, c0:c0 + LANES] = jnp.where(low, outs[2 * a], outs[2 * a + 1]).astype(BF16)

    for kv in range(D_KV_HEADS):
        outputs(kv, softmax(kv, *scores(kv)))


def _window_attention(sink, proj, q_col0, k_col0, v_col0, pc, *, tq):
    n = proj.shape[0]
    nc = pc.shape[0]
    kvw = D_KV_HEADS * LANES
    per = tq // WINDOW
    nw = n // WINDOW

    def window(col0):
        cb = col0 // kvw
        return [pl.BlockSpec((WINDOW, kvw), lambda i: (jnp.maximum(i * per - 1, 0), cb)),
                pl.BlockSpec((tq, kvw), lambda i: (i, cb)),
                pl.BlockSpec((WINDOW, kvw), lambda i: (jnp.minimum((i + 1) * per, nw - 1), cb))]

    return pl.pallas_call(
        functools.partial(_win_attn_kernel, tq=tq, n=n),
        grid=(n // tq,),
        in_specs=[pl.BlockSpec(memory_space=pltpu.SMEM),
                  pl.BlockSpec((tq, D_Q_WIDTH), lambda i: (i, q_col0 // D_Q_WIDTH))]
                 + window(k_col0) + window(v_col0)
                 + [pl.BlockSpec((nc, 2 * kvw), lambda i: (0, 0))],
        out_specs=pl.BlockSpec((tq, D_Q_WIDTH), lambda i: (i, 0)),
        out_shape=jax.ShapeDtypeStruct((n, D_Q_WIDTH), BF16),
        compiler_params=_params(("parallel",)),
        name="window_attention",
    )(sink, proj, proj, proj, proj, proj, proj, proj, pc)


def _dup_head_cols(w):
    d = w.shape[0]
    w = w.reshape(d, D_KV_HEADS, 1, HEAD_DIM)
    return jnp.broadcast_to(w, (d, D_KV_HEADS, 2, HEAD_DIM)).reshape(d, D_KV_HEADS * LANES)


OUT_ROWS = 256


def _out_kernel(a_ref, b_ref, x_ref, w_ref, gpost_ref, gate_ref, gpre_ref, sc_ref, sh_ref, x1_ref, hf_ref):
    half = a_ref.shape[1]
    for r in range(x_ref.shape[0] // OUT_ROWS):
        rows = slice(r * OUT_ROWS, (r + 1) * OUT_ROWS)
        y = (jnp.dot(a_ref[rows, :], w_ref[:half, :], preferred_element_type=F32)
             + jnp.dot(b_ref[rows, :], w_ref[half:, :], preferred_element_type=F32))
        x1 = x_ref[rows, :] + gate_ref[...] * (_rms(y) * gpost_ref[...])
        x1_ref[rows, :] = x1
        hf_ref[rows, :] = ((_rms(x1) * gpre_ref[...]) * (1.0 + sc_ref[...]) + sh_ref[...]).astype(BF16)


def _out_project(a, b, x, w, g_post, gate, g_pre, scale, shift, *, tm):
    n, d = x.shape
    half = a.shape[1]
    vec = pl.BlockSpec((1, d), lambda i: (0, 0))
    return pl.pallas_call(
        _out_kernel,
        grid=(n // tm,),
        in_specs=[pl.BlockSpec((tm, half), lambda i: (i, 0)),
                  pl.BlockSpec((tm, half), lambda i: (i, 0)),
                  pl.BlockSpec((tm, d), lambda i: (i, 0)),
                  pl.BlockSpec((2 * half, d), lambda i: (0, 0), pipeline_mode=pl.Buffered(1)),
                  vec, vec, vec, vec, vec],
        out_specs=[pl.BlockSpec((tm, d), lambda i: (i, 0)), pl.BlockSpec((tm, d), lambda i: (i, 0))],
        out_shape=[jax.ShapeDtypeStruct((n, d), F32), jax.ShapeDtypeStruct((n, d), BF16)],
        compiler_params=_params(("parallel",)),
        name="out_project",
    )(a, b, x, w, g_post, gate, g_pre, scale, shift)


def _mlp_kernel(hf_ref, w1_ref, w2_ref, x1_ref, gpost_ref, gate_ref, o_ref, acc_scr):
    f = pl.program_id(1)

    @pl.when(f == 0)
    def _():
        acc_scr[...] = jnp.zeros(acc_scr.shape, F32)

    h = jnp.dot(hf_ref[...], w1_ref[...], preferred_element_type=F32)
    h = jnp.square(jnp.maximum(h, 0.0)).astype(BF16)
    acc_scr[...] += jnp.dot(h, w2_ref[...], preferred_element_type=F32)

    @pl.when(f == pl.num_programs(1) - 1)
    def _():
        o_ref[...] = x1_ref[...] + gate_ref[...] * (_rms(acc_scr[...]) * gpost_ref[...])


def _mlp(hf, w1, w2, x1, g_post, gate, *, tm, tf):
    n, d = x1.shape
    ff = w1.shape[1]
    vec = pl.BlockSpec((1, d), lambda i, f: (0, 0))
    return pl.pallas_call(
        _mlp_kernel,
        grid=(n // tm, ff // tf),
        in_specs=[pl.BlockSpec((tm, d), lambda i, f: (i, 0)),
                  pl.BlockSpec((d, tf), lambda i, f: (0, f)),
                  pl.BlockSpec((tf, d), lambda i, f: (f, 0)),
                  pl.BlockSpec((tm, d), lambda i, f: (i, 0)),
                  vec, vec],
        out_specs=pl.BlockSpec((tm, d), lambda i, f: (i, 0)),
        out_shape=jax.ShapeDtypeStruct((n, d), F32),
        scratch_shapes=[pltpu.VMEM((tm, d), F32)],
        compiler_params=_params(("parallel", "arbitrary")),
        name="mlp",
    )(hf, w1, w2, x1, g_post, gate)


Q0_SCALE = ATTN_SCALE * LOG2E
Q1_SCALE = ATTN_SCALE * LOG2E
PLAIN = (0.0, 1.0)


def _rope_table(n):
    rows = n // GRID_W
    pairs = HEAD_DIM // 4
    inv = jnp.power(ROPE_BASE, -jnp.arange(pairs, dtype=F32) / pairs)
    ang_r = jnp.arange(rows, dtype=F32)[:, None] * inv
    ang_c = jnp.arange(GRID_W, dtype=F32)[:, None] * inv

    def grid(f):
        r = jnp.broadcast_to(f(ang_r)[:, None, :], (rows, GRID_W, pairs))
        c = jnp.broadcast_to(f(ang_c)[None, :, :], (rows, GRID_W, pairs))
        return jnp.concatenate([r, c], axis=-1).reshape(n, 2 * pairs)

    cos, sin = grid(jnp.cos), grid(jnp.sin)
    return jnp.stack([jnp.tile(cos, (1, 4)), jnp.tile(jnp.concatenate([-sin, sin], axis=-1), (1, 2))])


def _row(v):
    return v.reshape(1, -1)


def _split_mod(m):
    return [_row(t) for t in jnp.split(m, N_MOD)]


def kernel(x, c, ctx, c_ctx, l0_mod_w, l0_mod_b, l0_norm_mix_pre, l0_norm_mix_post, l0_norm_mlp_pre, l0_norm_mlp_post, l0_w_in, l0_conv_w, l0_conv_b, l0_ln_g, l0_ln_b, l0_lambda_q1, l0_lambda_k1, l0_lambda_q2, l0_lambda_k2, l0_subln_g, l0_w_out, l0_mlp_w1, l0_mlp_w2, l1_mod_w, l1_mod_b, l1_norm_mix_pre, l1_norm_mix_post, l1_norm_mlp_pre, l1_norm_mlp_post, l1_w_in, l1_sconv_w, l1_sink, l1_w_out, l1_mlp_w1, l1_mlp_w2):
    n = x.shape[1]
    nc = ctx.shape[1]
    xs = x[0]
    cs = ctx[0]
    rope = _rope_table(n)
    cvecs = jnp.zeros((8, D_MODEL), F32).at[0].set(c[0]).at[1].set(c_ctx)

    mod = _modulation(cvecs, l0_mod_w, l0_mod_b)
    sh_m, sc_m, gt_m, sh_f, sc_f, gt_f = _split_mod(mod[0])
    csh_m, csc_m, cgt_m, csh_f, csc_f, cgt_f = _split_mod(mod[1])
    w_in = l0_w_in.astype(BF16)
    w_out = l0_w_out.astype(BF16)
    w1 = l0_mlp_w1.astype(BF16)
    w2 = l0_mlp_w2.astype(BF16)
    g_pre, g_post = _row(l0_norm_mix_pre), _row(l0_norm_mix_post)
    gf_pre, gf_post = _row(l0_norm_mlp_pre), _row(l0_norm_mlp_post)
    q0, k0, v0 = A_IN, A_IN + B_WIDTH, A_IN + 2 * B_WIDTH

    blocks = lambda *widths_coefs: [c for w, c in widths_coefs for _ in range(w // PROJ_SUB)]
    proj = _project(xs, g_pre, sc_m, sh_m, w_in, rope,
                    blocks((A_IN, PLAIN), (B_WIDTH, (Q0_SCALE, 0.0)), (B_WIDTH, (1.0, 0.0)), (B_WIDTH, PLAIN)),
                    tm=512, tn=2560, out_rows=n + nc)
    pc, proj = _project(cs, g_pre, csc_m, csh_m, w_in, rope,
                        blocks((A_IN, PLAIN), (B_WIDTH, (0.0, Q0_SCALE)), (2 * B_WIDTH, PLAIN)),
                        tm=nc, tn=1024, fill=(proj, n))
    lam_init = 0.8 - 0.6 * math.exp(-0.3 * 0)
    lam_rows = jnp.zeros((8, LANES), F32)
    for r, lv in enumerate((l0_lambda_q1, l0_lambda_k1, l0_lambda_q2, l0_lambda_k2)):
        lam_rows = lam_rows.at[r, :HEAD_DIM].set(lv)

    a_lat = _conformer_conv(proj, l0_conv_w, l0_conv_b, l0_ln_g, l0_ln_b, n=n, t=512)
    a_ctx = _conformer_conv(pc, l0_conv_w, l0_conv_b, l0_ln_g, l0_ln_b, n=nc, t=nc)
    b_lat = _diff_attention(lam_rows, l0_subln_g, proj, q0, proj, k0, v0, n_q=n, tq=1024, tk=3328,
                            lam_init=lam_init)
    b_ctx = _diff_attention(lam_rows, l0_subln_g, pc, q0, pc, k0, v0, n_q=nc, tq=nc, tk=nc, lam_init=lam_init)

    x1, hf = _out_project(a_lat, b_lat, xs, w_out, g_post, gt_m, gf_pre, sc_f, sh_f, tm=512)
    xs = _mlp(hf, w1, w2, x1, gf_post, gt_f, tm=512, tf=1024)
    c1, hcf = _out_project(a_ctx, b_ctx, cs, w_out, g_post, cgt_m, gf_pre, csc_f, csh_f, tm=nc)
    cs = _mlp(hcf, w1, w2, c1, gf_post, cgt_f, tm=nc, tf=1024)

    mod = _modulation(cvecs, l1_mod_w, l1_mod_b)
    sh_m, sc_m, gt_m, sh_f, sc_f, gt_f = _split_mod(mod[0])
    csh_m, csc_m = _split_mod(mod[1])[:2]
    w_out = l1_w_out.astype(BF16)
    w1 = l1_mlp_w1.astype(BF16)
    w2 = l1_mlp_w2.astype(BF16)
    g_pre, g_post = _row(l1_norm_mix_pre), _row(l1_norm_mix_post)
    gf_pre, gf_post = _row(l1_norm_mlp_pre), _row(l1_norm_mlp_post)
    q0, k0, v0 = C_IN, C_IN + D_Q_WIDTH, C_IN + D_Q_WIDTH + D_KV_WIDTH
    w_in = jnp.concatenate([l1_w_in[:, :k0], _dup_head_cols(l1_w_in[:, k0:v0]), _dup_head_cols(l1_w_in[:, v0:])],
                           axis=1).astype(BF16)
    kd0, vd0 = k0, k0 + 2 * D_KV_WIDTH

    proj = _project(xs, g_pre, sc_m, sh_m, w_in, rope,
                    blocks((C_IN, PLAIN), (D_Q_WIDTH, (Q1_SCALE, 0.0)), (2 * D_KV_WIDTH, (1.0, 0.0)),
                           (2 * D_KV_WIDTH, PLAIN)), tm=512, tn=2560)
    pc = _project(cs, g_pre, csc_m, csh_m, w_in[:, kd0:], rope, blocks((4 * D_KV_WIDTH, PLAIN)), tm=nc, tn=1024)
    c_lat = _short_conv(proj, l1_sconv_w, t=512)
    d_lat = _window_attention(l1_sink, proj, q0, kd0, vd0, pc, tq=512)
    x1, hf = _out_project(c_lat, d_lat, xs, w_out, g_post, gt_m, gf_pre, sc_f, sh_f, tm=512)
    xs = _mlp(hf, w1, w2, x1, gf_post, gt_f, tm=512, tf=1024)
    return xs[None]
```

```python
import functools
import math

import jax
import jax.numpy as jnp
from jax import lax
from jax.experimental import pallas as pl
from jax.experimental.pallas import tpu as pltpu

F32 = jnp.float32
BF16 = jnp.bfloat16

D_MODEL = 2048
HEAD_DIM = 64
HALF = HEAD_DIM // 2
GRID_W = 64
ROPE_BASE = 10000.0
NORM_EPS = 1e-6
LN_EPS = 1e-5
ATTN_SCALE = HEAD_DIM ** -0.5
LOG2E = math.log2(math.e)
NEG_INF = -1e30
N_MOD = 6
WINDOW = 128
LANES = 128
HALO = 16

A_WIDTH = 1024
A_IN = 2 * A_WIDTH
CONV_A_WIDTH = 31
B_HEADS = 8
B_WIDTH = 1024
C_WIDTH = 1024
C_IN = 3 * C_WIDTH
D_HEADS = 16
D_KV_HEADS = 4
D_GROUP = 4
D_Q_WIDTH = 1024
D_KV_WIDTH = 256
D_FF = 4 * D_MODEL

VMEM_LIMIT = 56 * 1024 * 1024


def _params(sem):
    return pltpu.CompilerParams(dimension_semantics=sem, vmem_limit_bytes=VMEM_LIMIT)


def _rms(x, eps=NORM_EPS):
    return x * lax.rsqrt(jnp.mean(x * x, axis=-1, keepdims=True) + eps)


def _dot_nt(a, b):
    return lax.dot_general(a, b, (((1,), (1,)), ((), ())), preferred_element_type=F32)


def _mod_kernel(c_ref, w_ref, b_ref, o_ref):
    c = c_ref[...]
    s = c * jax.nn.sigmoid(c)
    o_ref[...] = jnp.dot(s, w_ref[...], preferred_element_type=F32) + b_ref[...]


def _modulation(cvecs, w, b):
    tn = 1024
    n_out = w.shape[1]
    return pl.pallas_call(
        _mod_kernel,
        grid=(n_out // tn,),
        in_specs=[pl.BlockSpec((8, D_MODEL), lambda j: (0, 0)),
                  pl.BlockSpec((D_MODEL, tn), lambda j: (0, j)),
                  pl.BlockSpec((1, tn), lambda j: (0, j))],
        out_specs=pl.BlockSpec((8, tn), lambda j: (0, j)),
        out_shape=jax.ShapeDtypeStruct((8, n_out), F32),
        compiler_params=_params(("arbitrary",)),
        name="modulation",
    )(cvecs, w, b.reshape(1, n_out))


PROJ_SUB = 512
PROJ_ROWS = 256


def _proj_kernel(coef_ref, x_ref, g_ref, sc_ref, sh_ref, tab_ref, w_ref, *rest, tn, fill):
    outs, h_scr = (rest[1:3], rest[3]) if fill else (rest[:1], rest[1])
    nsub = tn // PROJ_SUB
    j = pl.program_id(1)

    @pl.when(j == 0)
    def _():
        y = _rms(x_ref[...]) * g_ref[...]
        h_scr[...] = (y * (1.0 + sc_ref[...]) + sh_ref[...]).astype(BF16)

    lane = lax.broadcasted_iota(jnp.int32, (PROJ_ROWS, LANES), 1)
    first_half = (lane & (HEAD_DIM - 1)) < HALF
    for r in range(x_ref.shape[0] // PROJ_ROWS):
        rows = slice(r * PROJ_ROWS, (r + 1) * PROJ_ROWS)
        acc = jnp.dot(h_scr[rows, :], w_ref[...], preferred_element_type=F32)
        cosf = tab_ref[0, rows, :]
        sinf = tab_ref[1, rows, :]
        for s in range(nsub):
            alpha = coef_ref[2 * (j * nsub + s)]
            beta = coef_ref[2 * (j * nsub + s) + 1]
            a = alpha * cosf + beta
            b = alpha * sinf
            for c in range(s * PROJ_SUB // LANES, (s + 1) * PROJ_SUB // LANES):
                y = acc[:, c * LANES:(c + 1) * LANES]
                swapped = jnp.where(first_half, pltpu.roll(y, LANES - HALF, 1), pltpu.roll(y, HALF, 1))
                res = (y * a + swapped * b).astype(BF16)
                for o_ref in outs:
                    o_ref[rows, c * LANES:(c + 1) * LANES] = res


def _project(x, g, scale, shift, w, rope_tab, coefs, *, tm, tn, out_rows=None, fill=None):
    n, d = x.shape
    p = w.shape[1]
    assert len(coefs) == p // PROJ_SUB
    vec = pl.BlockSpec((1, d), lambda i, j: (0, 0))
    row_blocks = pl.cdiv(out_rows or n, tm)
    src = lambda i: jnp.minimum(i, n // tm - 1)
    in_specs = [pl.BlockSpec(memory_space=pltpu.SMEM),
                pl.BlockSpec((tm, d), lambda i, j: (src(i), 0)), vec, vec, vec,
                pl.BlockSpec((2, tm, LANES), lambda i, j: (0, src(i), 0)),
                pl.BlockSpec((d, tn), lambda i, j: (0, j))]
    args = [jnp.asarray(coefs, F32).reshape(-1), x, g, scale, shift, rope_tab, w]
    out_specs = [pl.BlockSpec((tm, tn), lambda i, j: (i, j))]
    out_shape = [jax.ShapeDtypeStruct((out_rows or n, p), BF16)]
    aliases = {}
    if fill is not None:
        buf, row0 = fill
        blk0 = row0 // tm
        in_specs.append(pl.BlockSpec(memory_space=pl.ANY))
        args.append(buf)
        out_specs.append(pl.BlockSpec((tm, tn), lambda i, j: (blk0 + i, j)))
        out_shape.append(jax.ShapeDtypeStruct(buf.shape, BF16))
        aliases = {len(args) - 1: 1}
    res = pl.pallas_call(
        functools.partial(_proj_kernel, tn=tn, fill=fill is not None),
        grid=(row_blocks, p // tn),
        in_specs=in_specs,
        out_specs=out_specs,
        out_shape=out_shape,
        input_output_aliases=aliases,
        scratch_shapes=[pltpu.VMEM((tm, d), BF16)],
        compiler_params=_params(("parallel", "arbitrary")),
        name="norm_project",
    )(*args)
    return res if fill is not None else res[0]


CONV_A_ROWS = 32


def _conv_a_kernel(main_ref, left_ref, right_ref, w_ref, cb_ref, lg_ref, lb_ref, o_ref, ext_scr, sh_scr, *, t):
    i = pl.program_id(0)
    last = pl.num_programs(0) - 1

    def glu(ref):
        v = ref[...].astype(F32)
        return v[:, :A_WIDTH] * jax.nn.sigmoid(v[:, A_WIDTH:])

    ext_scr[0:HALO, :] = jnp.where(i > 0, glu(left_ref), 0.0)
    ext_scr[HALO:HALO + t, :] = glu(main_ref)
    ext_scr[HALO + t:, :] = jnp.where(i < last, glu(right_ref), 0.0)
    ext = ext_scr[...]
    rows = t + 2 * HALO
    srows = t + 3 * 8
    sh_scr[0] = ext[:srows]
    for b in range(1, 8):
        sh_scr[b] = pltpu.roll(ext, rows - b, 0)[:srows]

    def chunk(c, carry):
        r0 = pl.multiple_of(c * CONV_A_ROWS, CONV_A_ROWS)
        acc = jnp.zeros((CONV_A_ROWS, A_WIDTH), F32)
        for k in range(CONV_A_WIDTH):
            a, b = divmod(k + 1, 8)
            acc = acc + sh_scr[b, pl.ds(r0 + 8 * a, CONV_A_ROWS), :] * w_ref[k:k + 1, :]
        v = acc + cb_ref[...]
        mu = jnp.mean(v, axis=-1, keepdims=True)
        vc = v - mu
        var = jnp.mean(vc * vc, axis=-1, keepdims=True)
        y = vc * lax.rsqrt(var + LN_EPS) * lg_ref[...] + lb_ref[...]
        o_ref[pl.ds(r0, CONV_A_ROWS), :] = (y * jax.nn.sigmoid(y)).astype(BF16)
        return carry

    lax.fori_loop(0, t // CONV_A_ROWS, chunk, 0, unroll=2)


def _conformer_conv(proj, conv_w, conv_b, ln_g, ln_b, *, n, t):
    per = t // HALO
    nh = n // HALO
    vec = pl.BlockSpec((1, A_WIDTH), lambda i: (0, 0))
    return pl.pallas_call(
        functools.partial(_conv_a_kernel, t=t),
        grid=(n // t,),
        in_specs=[pl.BlockSpec((t, A_IN), lambda i: (i, 0)),
                  pl.BlockSpec((HALO, A_IN), lambda i: (jnp.maximum(i * per - 1, 0), 0)),
                  pl.BlockSpec((HALO, A_IN), lambda i: (jnp.minimum((i + 1) * per, nh - 1), 0)),
                  pl.BlockSpec((CONV_A_WIDTH, A_WIDTH), lambda i: (0, 0)),
                  vec, vec, vec],
        out_specs=pl.BlockSpec((t, A_WIDTH), lambda i: (i, 0)),
        out_shape=jax.ShapeDtypeStruct((n, A_WIDTH), BF16),
        scratch_shapes=[pltpu.VMEM((t + 2 * HALO, A_WIDTH), F32),
                        pltpu.VMEM((8, t + 24, A_WIDTH), F32)],
        compiler_params=_params(("parallel",)),
        name="conformer_conv",
    )(proj, proj, proj, conv_w, conv_b.reshape(1, -1), ln_g.reshape(1, -1), ln_b.reshape(1, -1))


def _tile_lanes(x, reps):
    return x if reps == 1 else jnp.concatenate([x] * reps, axis=1)


def _diff_attn_kernel(lam_ref, sub_ref, q_ref, k_ref, v_ref, *rest, lam_init, tk, ncast):
    cast_in, o_ref, cast_out = rest[:ncast], rest[ncast], rest[ncast + 1:2 * ncast + 1]
    qm_scr, m_scr, acc_scr = rest[2 * ncast + 1:]
    for w_ref, wb_ref in zip(cast_in, cast_out):
        wb_ref[...] = w_ref[...].astype(BF16)
    q = q_ref[...]
    lane = lax.broadcasted_iota(jnp.int32, q.shape, 1)
    zero = jnp.zeros_like(q)
    qm_scr[0] = jnp.where(lane < HEAD_DIM, q, zero)
    qm_scr[1] = jnp.where(lane >= HEAD_DIM, q, zero)
    m_scr[...] = jnp.full(m_scr.shape, -jnp.inf, F32)
    acc_scr[...] = jnp.zeros(acc_scr.shape, F32)
    ones = jnp.ones((tk, LANES), BF16)

    def chunk(c, carry):
        r0 = pl.multiple_of(c * tk, tk)
        k = k_ref[pl.ds(r0, tk), :]
        v_ext = jnp.concatenate([v_ref[pl.ds(r0, tk), :], ones], axis=1)
        scores = [_dot_nt(qm_scr[m], k) for m in range(2)]
        for m in range(2):
            s = scores[m]
            m_prev = m_scr[m]
            m_new = jnp.maximum(m_prev, jnp.max(s, axis=1, keepdims=True))
            alpha = jnp.exp2(m_prev - m_new)
            p = jnp.exp2(s - _tile_lanes(m_new, tk // LANES))
            acc_scr[m] = (_tile_lanes(alpha, 2) * acc_scr[m]
                          + jnp.dot(p.astype(BF16), v_ext, preferred_element_type=F32))
            m_scr[m] = m_new
        return carry

    lax.fori_loop(0, k_ref.shape[0] // tk, chunk, 0)

    lp = lam_ref[...]
    lam = (jnp.exp(jnp.sum(lp[0:1] * lp[1:2], axis=1, keepdims=True))
           - jnp.exp(jnp.sum(lp[2:3] * lp[3:4], axis=1, keepdims=True)) + lam_init)
    o = (acc_scr[0, :, :LANES] / acc_scr[0, :, LANES:]
         - lam * (acc_scr[1, :, :LANES] / acc_scr[1, :, LANES:]))
    o = _rms(o) * sub_ref[...] * (1.0 - lam_init)
    o_ref[...] = o.astype(BF16)


def _diff_attention(lam_rows, subln_g, q_src, q_col0, kv_src, k_col0, v_col0, *, n_q, tq, tk, lam_init, cast=()):
    nk = kv_src.shape[0]
    qb, kb, vb = q_col0 // LANES, k_col0 // LANES, v_col0 // LANES
    nq = n_q // tq
    steps = B_HEADS * nq
    slab = lambda w: pl.BlockSpec((w.shape[0] // steps, w.shape[1]), lambda h, i: (h * nq + i, 0))
    res = pl.pallas_call(
        functools.partial(_diff_attn_kernel, lam_init=lam_init, tk=tk, ncast=len(cast)),
        grid=(B_HEADS, nq),
        in_specs=[pl.BlockSpec((8, LANES), lambda h, i: (0, 0)),
                  pl.BlockSpec((1, LANES), lambda h, i: (0, 0)),
                  pl.BlockSpec((tq, LANES), lambda h, i: (i, qb + h)),
                  pl.BlockSpec((nk, LANES), lambda h, i: (0, kb + h)),
                  pl.BlockSpec((nk, LANES), lambda h, i: (0, vb + h))] + [slab(w) for w in cast],
        out_specs=[pl.BlockSpec((tq, LANES), lambda h, i: (i, h))] + [slab(w) for w in cast],
        out_shape=[jax.ShapeDtypeStruct((n_q, B_WIDTH), BF16)]
                  + [jax.ShapeDtypeStruct(w.shape, BF16) for w in cast],
        scratch_shapes=[pltpu.VMEM((2, tq, LANES), BF16),
                        pltpu.VMEM((2, tq, LANES), F32),
                        pltpu.VMEM((2, tq, 2 * LANES), F32)],
        compiler_params=_params(("parallel", "arbitrary")),
        name="diff_attention",
    )(lam_rows, subln_g.reshape(1, LANES), q_src, kv_src, kv_src, *cast)
    return res[0], res[1:]


def _conv_c_kernel(main_ref, left_ref, right_ref, w_ref, o_ref, ext_scr, *, t):
    i = pl.program_id(0)
    last = pl.num_programs(0) - 1

    def gated(ref):
        v = ref[...].astype(F32)
        return v[:, C_WIDTH:2 * C_WIDTH] * v[:, 2 * C_WIDTH:]

    ext_scr[0:HALO, :] = jnp.where(i > 0, gated(left_ref), 0.0)
    ext_scr[HALO:HALO + t, :] = gated(main_ref)
    ext_scr[HALO + t:, :] = jnp.where(i < last, gated(right_ref), 0.0)
    ext = ext_scr[...]
    rows = t + 2 * HALO
    prev = pltpu.roll(ext, 1, 0)[HALO:HALO + t]
    nxt = pltpu.roll(ext, rows - 1, 0)[HALO:HALO + t]
    conv = prev * w_ref[0:1, :] + ext[HALO:HALO + t] * w_ref[1:2, :] + nxt * w_ref[2:3, :]
    o_ref[...] = (main_ref[:, :C_WIDTH].astype(F32) * conv).astype(BF16)


def _short_conv(proj, w, *, t):
    n = proj.shape[0]
    per = t // HALO
    nh = n // HALO
    return pl.pallas_call(
        functools.partial(_conv_c_kernel, t=t),
        grid=(n // t,),
        in_specs=[pl.BlockSpec((t, C_IN), lambda i: (i, 0)),
                  pl.BlockSpec((HALO, C_IN), lambda i: (jnp.maximum(i * per - 1, 0), 0)),
                  pl.BlockSpec((HALO, C_IN), lambda i: (jnp.minimum((i + 1) * per, nh - 1), 0)),
                  pl.BlockSpec((3, C_WIDTH), lambda i: (0, 0))],
        out_specs=pl.BlockSpec((t, C_WIDTH), lambda i: (i, 0)),
        out_shape=jax.ShapeDtypeStruct((n, C_WIDTH), BF16),
        scratch_shapes=[pltpu.VMEM((t + 2 * HALO, C_WIDTH), F32)],
        compiler_params=_params(("parallel",)),
        name="short_conv",
    )(proj, proj, proj, w)


def _win_attn_kernel(sink_ref, q_ref, kp_ref, km_ref, kn_ref, vp_ref, vm_ref, vn_ref, c_ref, o_ref, *, tq, n):
    i = pl.program_id(0)
    lane = lax.broadcasted_iota(jnp.int32, (tq, LANES), 1)
    low = lane < HEAD_DIM

    def with_ones(v, keep_low):
        col = lax.broadcasted_iota(jnp.int32, v.shape, 1)
        return jnp.where((col < HEAD_DIM) if keep_low else (col >= HEAD_DIM), v, jnp.ones_like(v))

    nb = tq // WINDOW
    span = 3 * WINDOW
    row = lax.broadcasted_iota(jnp.int32, (WINDOW, span), 0)
    col = lax.broadcasted_iota(jnp.int32, (WINDOW, span), 1)
    band = (col >= row) & (col <= row + 2 * WINDOW)
    masks = []
    for blk in range(nb):
        pos = i * tq + (blk - 1) * WINDOW + col
        masks.append(band & (pos >= 0) & (pos < n))
    zeros = jnp.zeros((WINDOW, WINDOW), BF16)

    def scores(kv):
        tile = slice(kv * LANES, (kv + 1) * LANES)
        kw = jnp.concatenate([kp_ref[:, tile], km_ref[:, tile], kn_ref[:, tile]], axis=0)
        kc = c_ref[:, tile]
        qms = []
        for g in range(D_GROUP):
            c0 = (kv * D_GROUP + g) // 2 * LANES
            qa = q_ref[:, c0:c0 + LANES]
            qms.append(jnp.where(low if g % 2 == 0 else (lane >= HEAD_DIM), qa, jnp.zeros_like(qa)))
        q_all = jnp.concatenate(qms, axis=0)
        return _dot_nt(q_all, kc), _dot_nt(q_all, kw)

    def softmax(kv, s_c_all, s_w_all):
        probs = []
        for g in range(D_GROUP):
            sk = sink_ref[kv * D_GROUP + g] * LOG2E
            p_cs, p_ws, sinks = [], [], []
            for blk in range(nb):
                rows = slice(g * tq + blk * WINDOW, g * tq + (blk + 1) * WINDOW)
                s_c = s_c_all[rows]
                s_w = jnp.where(masks[blk], s_w_all[rows, blk * WINDOW:blk * WINDOW + span], NEG_INF)
                mx = jnp.maximum(jnp.maximum(jnp.max(s_c, axis=1, keepdims=True),
                                             jnp.max(s_w, axis=1, keepdims=True)), sk)
                p_cs.append(jnp.exp2(s_c - mx).astype(BF16))
                p_ws.append(jnp.concatenate([zeros] * blk + [jnp.exp2(s_w - mx).astype(BF16)]
                                            + [zeros] * (nb - 1 - blk), axis=1))
                sinks.append(jnp.exp2(sk - mx))
            probs.append((jnp.concatenate(p_cs, axis=0), jnp.concatenate(p_ws, axis=0),
                          jnp.concatenate(sinks, axis=0)))
        return probs

    def outputs(kv, probs):
        tile = slice(kv * LANES, (kv + 1) * LANES)
        vw = jnp.concatenate([vp_ref[:, tile], vm_ref[:, tile], vn_ref[:, tile]], axis=0)
        vc = c_ref[:, D_KV_HEADS * LANES + kv * LANES:D_KV_HEADS * LANES + (kv + 1) * LANES]
        outs = [None] * D_GROUP
        for parity in range(2):
            heads = list(range(parity, D_GROUP, 2))
            p_c = jnp.concatenate([probs[g][0] for g in heads], axis=0)
            p_w = jnp.concatenate([probs[g][1] for g in heads], axis=0)
            p_sink = jnp.concatenate([probs[g][2] for g in heads], axis=0)
            o = (jnp.dot(p_c, with_ones(vc, parity == 0), preferred_element_type=F32)
                 + jnp.dot(p_w, with_ones(vw, parity == 0), preferred_element_type=F32))
            o = o / (pltpu.roll(o, HEAD_DIM, 1) + p_sink)
            for idx, g in enumerate(heads):
                outs[g] = o[idx * tq:(idx + 1) * tq]
        for a in range(D_GROUP // 2):
            c0 = (kv * D_GROUP // 2 + a) * LANES
            o_ref[:, c0:c0 + LANES] = jnp.where(low, outs[2 * a], outs[2 * a + 1]).astype(BF16)

    for kv in range(D_KV_HEADS):
        outputs(kv, softmax(kv, *scores(kv)))


def _window_attention(sink, proj, q_col0, k_col0, v_col0, pc, *, tq):
    n = proj.shape[0]
    nc = pc.shape[0]
    kvw = D_KV_HEADS * LANES
    per = tq // WINDOW
    nw = n // WINDOW

    def window(col0):
        cb = col0 // kvw
        return [pl.BlockSpec((WINDOW, kvw), lambda i: (jnp.maximum(i * per - 1, 0), cb)),
                pl.BlockSpec((tq, kvw), lambda i: (i, cb)),
                pl.BlockSpec((WINDOW, kvw), lambda i: (jnp.minimum((i + 1) * per, nw - 1), cb))]

    return pl.pallas_call(
        functools.partial(_win_attn_kernel, tq=tq, n=n),
        grid=(n // tq,),
        in_specs=[pl.BlockSpec(memory_space=pltpu.SMEM),
                  pl.BlockSpec((tq, D_Q_WIDTH), lambda i: (i, q_col0 // D_Q_WIDTH))]
                 + window(k_col0) + window(v_col0)
                 + [pl.BlockSpec((nc, 2 * kvw), lambda i: (0, 0))],
        out_specs=pl.BlockSpec((tq, D_Q_WIDTH), lambda i: (i, 0)),
        out_shape=jax.ShapeDtypeStruct((n, D_Q_WIDTH), BF16),
        compiler_params=_params(("parallel",)),
        name="window_attention",
    )(sink, proj, proj, proj, proj, proj, proj, proj, pc)


def _dup_head_cols(w):
    d = w.shape[0]
    w = w.reshape(d, D_KV_HEADS, 1, HEAD_DIM)
    return jnp.broadcast_to(w, (d, D_KV_HEADS, 2, HEAD_DIM)).reshape(d, D_KV_HEADS * LANES)


OUT_ROWS = 256


def _out_kernel(a_ref, b_ref, x_ref, w_ref, gpost_ref, gate_ref, gpre_ref, sc_ref, sh_ref, x1_ref, hf_ref):
    half = a_ref.shape[1]
    for r in range(x_ref.shape[0] // OUT_ROWS):
        rows = slice(r * OUT_ROWS, (r + 1) * OUT_ROWS)
        y = (jnp.dot(a_ref[rows, :], w_ref[:half, :], preferred_element_type=F32)
             + jnp.dot(b_ref[rows, :], w_ref[half:, :], preferred_element_type=F32))
        x1 = x_ref[rows, :] + gate_ref[...] * (_rms(y) * gpost_ref[...])
        x1_ref[rows, :] = x1
        hf_ref[rows, :] = ((_rms(x1) * gpre_ref[...]) * (1.0 + sc_ref[...]) + sh_ref[...]).astype(BF16)


def _out_project(a, b, x, w, g_post, gate, g_pre, scale, shift, *, tm):
    n, d = x.shape
    half = a.shape[1]
    vec = pl.BlockSpec((1, d), lambda i: (0, 0))
    return pl.pallas_call(
        _out_kernel,
        grid=(n // tm,),
        in_specs=[pl.BlockSpec((tm, half), lambda i: (i, 0)),
                  pl.BlockSpec((tm, half), lambda i: (i, 0)),
                  pl.BlockSpec((tm, d), lambda i: (i, 0)),
                  pl.BlockSpec((2 * half, d), lambda i: (0, 0), pipeline_mode=pl.Buffered(1)),
                  vec, vec, vec, vec, vec],
        out_specs=[pl.BlockSpec((tm, d), lambda i: (i, 0)), pl.BlockSpec((tm, d), lambda i: (i, 0))],
        out_shape=[jax.ShapeDtypeStruct((n, d), F32), jax.ShapeDtypeStruct((n, d), BF16)],
        compiler_params=_params(("parallel",)),
        name="out_project",
    )(a, b, x, w, g_post, gate, g_pre, scale, shift)


def _mlp_kernel(hf_ref, w1_ref, w2_ref, x1_ref, gpost_ref, gate_ref, o_ref, acc_scr):
    f = pl.program_id(1)

    @pl.when(f == 0)
    def _():
        acc_scr[...] = jnp.zeros(acc_scr.shape, F32)

    h = jnp.dot(hf_ref[...], w1_ref[...], preferred_element_type=F32)
    h = jnp.square(jnp.maximum(h, 0.0)).astype(BF16)
    acc_scr[...] += jnp.dot(h, w2_ref[...], preferred_element_type=F32)

    @pl.when(f == pl.num_programs(1) - 1)
    def _():
        o_ref[...] = x1_ref[...] + gate_ref[...] * (_rms(acc_scr[...]) * gpost_ref[...])


def _mlp(hf, w1, w2, x1, g_post, gate, *, tm, tf):
    n, d = x1.shape
    ff = w1.shape[1]
    vec = pl.BlockSpec((1, d), lambda i, f: (0, 0))
    return pl.pallas_call(
        _mlp_kernel,
        grid=(n // tm, ff // tf),
        in_specs=[pl.BlockSpec((tm, d), lambda i, f: (i, 0)),
                  pl.BlockSpec((d, tf), lambda i, f: (0, f)),
                  pl.BlockSpec((tf, d), lambda i, f: (f, 0)),
                  pl.BlockSpec((tm, d), lambda i, f: (i, 0)),
                  vec, vec],
        out_specs=pl.BlockSpec((tm, d), lambda i, f: (i, 0)),
        out_shape=jax.ShapeDtypeStruct((n, d), F32),
        scratch_shapes=[pltpu.VMEM((tm, d), F32)],
        compiler_params=_params(("parallel", "arbitrary")),
        name="mlp",
    )(hf, w1, w2, x1, g_post, gate)


Q0_SCALE = ATTN_SCALE * LOG2E
Q1_SCALE = ATTN_SCALE * LOG2E
PLAIN = (0.0, 1.0)


def _rope_table(n):
    rows = n // GRID_W
    pairs = HEAD_DIM // 4
    inv = jnp.power(ROPE_BASE, -jnp.arange(pairs, dtype=F32) / pairs)
    ang_r = jnp.arange(rows, dtype=F32)[:, None] * inv
    ang_c = jnp.arange(GRID_W, dtype=F32)[:, None] * inv

    def grid(f):
        r = jnp.broadcast_to(f(ang_r)[:, None, :], (rows, GRID_W, pairs))
        c = jnp.broadcast_to(f(ang_c)[None, :, :], (rows, GRID_W, pairs))
        return jnp.concatenate([r, c], axis=-1).reshape(n, 2 * pairs)

    cos, sin = grid(jnp.cos), grid(jnp.sin)
    return jnp.stack([jnp.tile(cos, (1, 4)), jnp.tile(jnp.concatenate([-sin, sin], axis=-1), (1, 2))])


def _row(v):
    return v.reshape(1, -1)


def _split_mod(m):
    return [_row(t) for t in jnp.split(m, N_MOD)]


def kernel(x, c, ctx, c_ctx, l0_mod_w, l0_mod_b, l0_norm_mix_pre, l0_norm_mix_post, l0_norm_mlp_pre, l0_norm_mlp_post, l0_w_in, l0_conv_w, l0_conv_b, l0_ln_g, l0_ln_b, l0_lambda_q1, l0_lambda_k1, l0_lambda_q2, l0_lambda_k2, l0_subln_g, l0_w_out, l0_mlp_w1, l0_mlp_w2, l1_mod_w, l1_mod_b, l1_norm_mix_pre, l1_norm_mix_post, l1_norm_mlp_pre, l1_norm_mlp_post, l1_w_in, l1_sconv_w, l1_sink, l1_w_out, l1_mlp_w1, l1_mlp_w2):
    n = x.shape[1]
    nc = ctx.shape[1]
    xs = x[0]
    cs = ctx[0]
    rope = _rope_table(n)
    cvecs = jnp.zeros((8, D_MODEL), F32).at[0].set(c[0]).at[1].set(c_ctx)

    mod = _modulation(cvecs, l0_mod_w, l0_mod_b)
    sh_m, sc_m, gt_m, sh_f, sc_f, gt_f = _split_mod(mod[0])
    csh_m, csc_m, cgt_m, csh_f, csc_f, cgt_f = _split_mod(mod[1])
    w_in = l0_w_in.astype(BF16)
    w_out = l0_w_out.astype(BF16)
    g_pre, g_post = _row(l0_norm_mix_pre), _row(l0_norm_mix_post)
    gf_pre, gf_post = _row(l0_norm_mlp_pre), _row(l0_norm_mlp_post)
    q0, k0, v0 = A_IN, A_IN + B_WIDTH, A_IN + 2 * B_WIDTH

    blocks = lambda *widths_coefs: [c for w, c in widths_coefs for _ in range(w // PROJ_SUB)]
    proj = _project(xs, g_pre, sc_m, sh_m, w_in, rope,
                    blocks((A_IN, PLAIN), (B_WIDTH, (Q0_SCALE, 0.0)), (B_WIDTH, (1.0, 0.0)), (B_WIDTH, PLAIN)),
                    tm=512, tn=2560, out_rows=n + nc)
    pc, proj = _project(cs, g_pre, csc_m, csh_m, w_in, rope,
                        blocks((A_IN, PLAIN), (B_WIDTH, (0.0, Q0_SCALE)), (2 * B_WIDTH, PLAIN)),
                        tm=nc, tn=1024, fill=(proj, n))
    lam_init = 0.8 - 0.6 * math.exp(-0.3 * 0)
    lam_rows = jnp.zeros((8, LANES), F32)
    for r, lv in enumerate((l0_lambda_q1, l0_lambda_k1, l0_lambda_q2, l0_lambda_k2)):
        lam_rows = lam_rows.at[r, :HEAD_DIM].set(lv)

    a_lat = _conformer_conv(proj, l0_conv_w, l0_conv_b, l0_ln_g, l0_ln_b, n=n, t=512)
    a_ctx = _conformer_conv(pc, l0_conv_w, l0_conv_b, l0_ln_g, l0_ln_b, n=nc, t=nc)
    b_lat, (w1, w2, w1_next, w2_next) = _diff_attention(
        lam_rows, l0_subln_g, proj, q0, proj, k0, v0, n_q=n, tq=1024, tk=3328, lam_init=lam_init,
        cast=(l0_mlp_w1, l0_mlp_w2, l1_mlp_w1, l1_mlp_w2))
    b_ctx, _ = _diff_attention(lam_rows, l0_subln_g, pc, q0, pc, k0, v0, n_q=nc, tq=nc, tk=nc, lam_init=lam_init)

    x1, hf = _out_project(a_lat, b_lat, xs, w_out, g_post, gt_m, gf_pre, sc_f, sh_f, tm=512)
    xs = _mlp(hf, w1, w2, x1, gf_post, gt_f, tm=512, tf=1024)
    c1, hcf = _out_project(a_ctx, b_ctx, cs, w_out, g_post, cgt_m, gf_pre, csc_f, csh_f, tm=nc)
    cs = _mlp(hcf, w1, w2, c1, gf_post, cgt_f, tm=nc, tf=1024)

    mod = _modulation(cvecs, l1_mod_w, l1_mod_b)
    sh_m, sc_m, gt_m, sh_f, sc_f, gt_f = _split_mod(mod[0])
    csh_m, csc_m = _split_mod(mod[1])[:2]
    w_out = l1_w_out.astype(BF16)
    w1, w2 = w1_next, w2_next
    g_pre, g_post = _row(l1_norm_mix_pre), _row(l1_norm_mix_post)
    gf_pre, gf_post = _row(l1_norm_mlp_pre), _row(l1_norm_mlp_post)
    q0, k0, v0 = C_IN, C_IN + D_Q_WIDTH, C_IN + D_Q_WIDTH + D_KV_WIDTH
    w_in = jnp.concatenate([l1_w_in[:, :k0], _dup_head_cols(l1_w_in[:, k0:v0]), _dup_head_cols(l1_w_in[:, v0:])],
                           axis=1).astype(BF16)
    kd0, vd0 = k0, k0 + 2 * D_KV_WIDTH

    proj = _project(xs, g_pre, sc_m, sh_m, w_in, rope,
                    blocks((C_IN, PLAIN), (D_Q_WIDTH, (Q1_SCALE, 0.0)), (2 * D_KV_WIDTH, (1.0, 0.0)),
                           (2 * D_KV_WIDTH, PLAIN)), tm=512, tn=2560)
    pc = _project(cs, g_pre, csc_m, csh_m, w_in[:, kd0:], rope, blocks((4 * D_KV_WIDTH, PLAIN)), tm=nc, tn=1024)
    c_lat = _short_conv(proj, l1_sconv_w, t=512)
    d_lat = _window_attention(l1_sink, proj, q0, kd0, vd0, pc, tq=512)
    x1, hf = _out_project(c_lat, d_lat, xs, w_out, g_post, gt_m, gf_pre, sc_f, sh_f, tm=512)
    xs = _mlp(hf, w1, w2, x1, gf_post, gt_f, tm=512, tf=1024)
    return xs[None]
```

```python
import functools
import math

import jax
import jax.numpy as jnp
from jax import lax
from jax.experimental import pallas as pl
from jax.experimental.pallas import tpu as pltpu

F32 = jnp.float32
BF16 = jnp.bfloat16

D_MODEL = 2048
HEAD_DIM = 64
HALF = HEAD_DIM // 2
GRID_W = 64
ROPE_BASE = 10000.0
NORM_EPS = 1e-6
LN_EPS = 1e-5
ATTN_SCALE = HEAD_DIM ** -0.5
LOG2E = math.log2(math.e)
NEG_INF = -1e30
N_MOD = 6
WINDOW = 128
LANES = 128
HALO = 16

A_WIDTH = 1024
A_IN = 2 * A_WIDTH
CONV_A_WIDTH = 31
B_HEADS = 8
B_WIDTH = 1024
C_WIDTH = 1024
C_IN = 3 * C_WIDTH
D_HEADS = 16
D_KV_HEADS = 4
D_GROUP = 4
D_Q_WIDTH = 1024
D_KV_WIDTH = 256
D_FF = 4 * D_MODEL

VMEM_LIMIT = 56 * 1024 * 1024


def _params(sem):
    return pltpu.CompilerParams(dimension_semantics=sem, vmem_limit_bytes=VMEM_LIMIT)


def _rms(x, eps=NORM_EPS):
    return x * lax.rsqrt(jnp.mean(x * x, axis=-1, keepdims=True) + eps)


def _dot_nt(a, b):
    return lax.dot_general(a, b, (((1,), (1,)), ((), ())), preferred_element_type=F32)


def _mod_kernel(c_ref, w_ref, b_ref, o_ref):
    c = c_ref[...]
    s = c * jax.nn.sigmoid(c)
    o_ref[...] = jnp.dot(s, w_ref[...], preferred_element_type=F32) + b_ref[...]


def _modulation(cvecs, w, b):
    tn = 1024
    n_out = w.shape[1]
    return pl.pallas_call(
        _mod_kernel,
        grid=(n_out // tn,),
        in_specs=[pl.BlockSpec((8, D_MODEL), lambda j: (0, 0)),
                  pl.BlockSpec((D_MODEL, tn), lambda j: (0, j)),
                  pl.BlockSpec((1, tn), lambda j: (0, j))],
        out_specs=pl.BlockSpec((8, tn), lambda j: (0, j)),
        out_shape=jax.ShapeDtypeStruct((8, n_out), F32),
        compiler_params=_params(("arbitrary",)),
        name="modulation",
    )(cvecs, w, b.reshape(1, n_out))


PROJ_SUB = 512
PROJ_ROWS = 256


def _proj_kernel(coef_ref, x_ref, g_ref, sc_ref, sh_ref, tab_ref, w_ref, *rest, tn, fill):
    outs, h_scr = (rest[1:3], rest[3]) if fill else (rest[:1], rest[1])
    nsub = tn // PROJ_SUB
    j = pl.program_id(1)

    @pl.when(j == 0)
    def _():
        y = _rms(x_ref[...]) * g_ref[...]
        h_scr[...] = (y * (1.0 + sc_ref[...]) + sh_ref[...]).astype(BF16)

    lane = lax.broadcasted_iota(jnp.int32, (PROJ_ROWS, LANES), 1)
    first_half = (lane & (HEAD_DIM - 1)) < HALF
    for r in range(x_ref.shape[0] // PROJ_ROWS):
        rows = slice(r * PROJ_ROWS, (r + 1) * PROJ_ROWS)
        acc = jnp.dot(h_scr[rows, :], w_ref[...], preferred_element_type=F32)
        cosf = tab_ref[0, rows, :]
        sinf = tab_ref[1, rows, :]
        for s in range(nsub):
            alpha = coef_ref[2 * (j * nsub + s)]
            beta = coef_ref[2 * (j * nsub + s) + 1]
            a = alpha * cosf + beta
            b = alpha * sinf
            for c in range(s * PROJ_SUB // LANES, (s + 1) * PROJ_SUB // LANES):
                y = acc[:, c * LANES:(c + 1) * LANES]
                swapped = jnp.where(first_half, pltpu.roll(y, LANES - HALF, 1), pltpu.roll(y, HALF, 1))
                res = (y * a + swapped * b).astype(BF16)
                for o_ref in outs:
                    o_ref[rows, c * LANES:(c + 1) * LANES] = res


def _project(x, g, scale, shift, w, rope_tab, coefs, *, tm, tn, out_rows=None, fill=None):
    n, d = x.shape
    p = w.shape[1]
    assert len(coefs) == p // PROJ_SUB
    vec = pl.BlockSpec((1, d), lambda i, j: (0, 0))
    row_blocks = pl.cdiv(out_rows or n, tm)
    src = lambda i: jnp.minimum(i, n // tm - 1)
    in_specs = [pl.BlockSpec(memory_space=pltpu.SMEM),
                pl.BlockSpec((tm, d), lambda i, j: (src(i), 0)), vec, vec, vec,
                pl.BlockSpec((2, tm, LANES), lambda i, j: (0, src(i), 0)),
                pl.BlockSpec((d, tn), lambda i, j: (0, j))]
    args = [jnp.asarray(coefs, F32).reshape(-1), x, g, scale, shift, rope_tab, w]
    out_specs = [pl.BlockSpec((tm, tn), lambda i, j: (i, j))]
    out_shape = [jax.ShapeDtypeStruct((out_rows or n, p), BF16)]
    aliases = {}
    if fill is not None:
        buf, row0 = fill
        blk0 = row0 // tm
        in_specs.append(pl.BlockSpec(memory_space=pl.ANY))
        args.append(buf)
        out_specs.append(pl.BlockSpec((tm, tn), lambda i, j: (blk0 + i, j)))
        out_shape.append(jax.ShapeDtypeStruct(buf.shape, BF16))
        aliases = {len(args) - 1: 1}
    res = pl.pallas_call(
        functools.partial(_proj_kernel, tn=tn, fill=fill is not None),
        grid=(row_blocks, p // tn),
        in_specs=in_specs,
        out_specs=out_specs,
        out_shape=out_shape,
        input_output_aliases=aliases,
        scratch_shapes=[pltpu.VMEM((tm, d), BF16)],
        compiler_params=_params(("parallel", "arbitrary")),
        name="norm_project",
    )(*args)
    return res if fill is not None else res[0]


CONV_A_ROWS = 32


def _conv_a_kernel(main_ref, left_ref, right_ref, w_ref, cb_ref, lg_ref, lb_ref, o_ref, ext_scr, sh_scr, *, t):
    i = pl.program_id(0)
    last = pl.num_programs(0) - 1

    def glu(ref):
        v = ref[...].astype(F32)
        return v[:, :A_WIDTH] * jax.nn.sigmoid(v[:, A_WIDTH:])

    ext_scr[0:HALO, :] = jnp.where(i > 0, glu(left_ref), 0.0)
    ext_scr[HALO:HALO + t, :] = glu(main_ref)
    ext_scr[HALO + t:, :] = jnp.where(i < last, glu(right_ref), 0.0)
    ext = ext_scr[...]
    rows = t + 2 * HALO
    srows = t + 3 * 8
    sh_scr[0] = ext[:srows]
    for b in range(1, 8):
        sh_scr[b] = pltpu.roll(ext, rows - b, 0)[:srows]

    def chunk(c, carry):
        r0 = pl.multiple_of(c * CONV_A_ROWS, CONV_A_ROWS)
        acc = jnp.zeros((CONV_A_ROWS, A_WIDTH), F32)
        for k in range(CONV_A_WIDTH):
            a, b = divmod(k + 1, 8)
            acc = acc + sh_scr[b, pl.ds(r0 + 8 * a, CONV_A_ROWS), :] * w_ref[k:k + 1, :]
        v = acc + cb_ref[...]
        mu = jnp.mean(v, axis=-1, keepdims=True)
        vc = v - mu
        var = jnp.mean(vc * vc, axis=-1, keepdims=True)
        y = vc * lax.rsqrt(var + LN_EPS) * lg_ref[...] + lb_ref[...]
        o_ref[pl.ds(r0, CONV_A_ROWS), :] = (y * jax.nn.sigmoid(y)).astype(BF16)
        return carry

    lax.fori_loop(0, t // CONV_A_ROWS, chunk, 0, unroll=2)


def _conformer_conv(proj, conv_w, conv_b, ln_g, ln_b, *, n, t):
    per = t // HALO
    nh = n // HALO
    vec = pl.BlockSpec((1, A_WIDTH), lambda i: (0, 0))
    return pl.pallas_call(
        functools.partial(_conv_a_kernel, t=t),
        grid=(n // t,),
        in_specs=[pl.BlockSpec((t, A_IN), lambda i: (i, 0)),
                  pl.BlockSpec((HALO, A_IN), lambda i: (jnp.maximum(i * per - 1, 0), 0)),
                  pl.BlockSpec((HALO, A_IN), lambda i: (jnp.minimum((i + 1) * per, nh - 1), 0)),
                  pl.BlockSpec((CONV_A_WIDTH, A_WIDTH), lambda i: (0, 0)),
                  vec, vec, vec],
        out_specs=pl.BlockSpec((t, A_WIDTH), lambda i: (i, 0)),
        out_shape=jax.ShapeDtypeStruct((n, A_WIDTH), BF16),
        scratch_shapes=[pltpu.VMEM((t + 2 * HALO, A_WIDTH), F32),
                        pltpu.VMEM((8, t + 24, A_WIDTH), F32)],
        compiler_params=_params(("parallel",)),
        name="conformer_conv",
    )(proj, proj, proj, conv_w, conv_b.reshape(1, -1), ln_g.reshape(1, -1), ln_b.reshape(1, -1))


def _tile_lanes(x, reps):
    return x if reps == 1 else jnp.concatenate([x] * reps, axis=1)


def _diff_attn_kernel(lam_ref, sub_ref, q_ref, k_ref, v_ref, *rest, lam_init, tk, ncast):
    cast_in, o_ref, cast_out = rest[:ncast], rest[ncast], rest[ncast + 1:2 * ncast + 1]
    qm_scr, m_scr, acc_scr = rest[2 * ncast + 1:]
    for w_ref, wb_ref in zip(cast_in, cast_out):
        wb_ref[...] = w_ref[...].astype(BF16)
    q = q_ref[...]
    lane = lax.broadcasted_iota(jnp.int32, q.shape, 1)
    zero = jnp.zeros_like(q)
    qm_scr[0] = jnp.where(lane < HEAD_DIM, q, zero)
    qm_scr[1] = jnp.where(lane >= HEAD_DIM, q, zero)
    m_scr[...] = jnp.full(m_scr.shape, -jnp.inf, F32)
    acc_scr[...] = jnp.zeros(acc_scr.shape, F32)
    ones = jnp.ones((tk, LANES), BF16)

    def chunk(c, carry):
        r0 = pl.multiple_of(c * tk, tk)
        k = k_ref[pl.ds(r0, tk), :]
        v_ext = jnp.concatenate([v_ref[pl.ds(r0, tk), :], ones], axis=1)
        scores = [_dot_nt(qm_scr[m], k) for m in range(2)]
        for m in range(2):
            s = scores[m]
            m_prev = m_scr[m]
            m_new = jnp.maximum(m_prev, jnp.max(s, axis=1, keepdims=True))
            alpha = jnp.exp2(m_prev - m_new)
            p = jnp.exp2(s - _tile_lanes(m_new, tk // LANES))
            acc_scr[m] = (_tile_lanes(alpha, 2) * acc_scr[m]
                          + jnp.dot(p.astype(BF16), v_ext, preferred_element_type=F32))
            m_scr[m] = m_new
        return carry

    lax.fori_loop(0, k_ref.shape[0] // tk, chunk, 0)

    lp = lam_ref[...]
    lam = (jnp.exp(jnp.sum(lp[0:1] * lp[1:2], axis=1, keepdims=True))
           - jnp.exp(jnp.sum(lp[2:3] * lp[3:4], axis=1, keepdims=True)) + lam_init)
    o = (acc_scr[0, :, :LANES] / acc_scr[0, :, LANES:]
         - lam * (acc_scr[1, :, :LANES] / acc_scr[1, :, LANES:]))
    o = _rms(o) * sub_ref[...] * (1.0 - lam_init)
    o_ref[...] = o.astype(BF16)


def _diff_attention(lam_rows, subln_g, q_src, q_col0, kv_src, k_col0, v_col0, *, n_q, tq, tk, lam_init, cast=()):
    nk = kv_src.shape[0]
    qb, kb, vb = q_col0 // LANES, k_col0 // LANES, v_col0 // LANES
    nq = n_q // tq
    steps = B_HEADS * nq
    slab = lambda w: pl.BlockSpec((w.shape[0] // steps, w.shape[1]), lambda h, i: (h * nq + i, 0))
    res = pl.pallas_call(
        functools.partial(_diff_attn_kernel, lam_init=lam_init, tk=tk, ncast=len(cast)),
        grid=(B_HEADS, nq),
        in_specs=[pl.BlockSpec((8, LANES), lambda h, i: (0, 0)),
                  pl.BlockSpec((1, LANES), lambda h, i: (0, 0)),
                  pl.BlockSpec((tq, LANES), lambda h, i: (i, qb + h)),
                  pl.BlockSpec((nk, LANES), lambda h, i: (0, kb + h)),
                  pl.BlockSpec((nk, LANES), lambda h, i: (0, vb + h))] + [slab(w) for w in cast],
        out_specs=[pl.BlockSpec((tq, LANES), lambda h, i: (i, h))] + [slab(w) for w in cast],
        out_shape=[jax.ShapeDtypeStruct((n_q, B_WIDTH), BF16)]
                  + [jax.ShapeDtypeStruct(w.shape, BF16) for w in cast],
        scratch_shapes=[pltpu.VMEM((2, tq, LANES), BF16),
                        pltpu.VMEM((2, tq, LANES), F32),
                        pltpu.VMEM((2, tq, 2 * LANES), F32)],
        compiler_params=_params(("parallel", "arbitrary")),
        name="diff_attention",
    )(lam_rows, subln_g.reshape(1, LANES), q_src, kv_src, kv_src, *cast)
    return res[0], res[1:]


def _conv_c_kernel(main_ref, left_ref, right_ref, w_ref, o_ref, ext_scr, *, t):
    i = pl.program_id(0)
    last = pl.num_programs(0) - 1

    def gated(ref):
        v = ref[...].astype(F32)
        return v[:, C_WIDTH:2 * C_WIDTH] * v[:, 2 * C_WIDTH:]

    ext_scr[0:HALO, :] = jnp.where(i > 0, gated(left_ref), 0.0)
    ext_scr[HALO:HALO + t, :] = gated(main_ref)
    ext_scr[HALO + t:, :] = jnp.where(i < last, gated(right_ref), 0.0)
    ext = ext_scr[...]
    rows = t + 2 * HALO
    prev = pltpu.roll(ext, 1, 0)[HALO:HALO + t]
    nxt = pltpu.roll(ext, rows - 1, 0)[HALO:HALO + t]
    conv = prev * w_ref[0:1, :] + ext[HALO:HALO + t] * w_ref[1:2, :] + nxt * w_ref[2:3, :]
    o_ref[...] = (main_ref[:, :C_WIDTH].astype(F32) * conv).astype(BF16)


def _short_conv(proj, w, *, t):
    n = proj.shape[0]
    per = t // HALO
    nh = n // HALO
    return pl.pallas_call(
        functools.partial(_conv_c_kernel, t=t),
        grid=(n // t,),
        in_specs=[pl.BlockSpec((t, C_IN), lambda i: (i, 0)),
                  pl.BlockSpec((HALO, C_IN), lambda i: (jnp.maximum(i * per - 1, 0), 0)),
                  pl.BlockSpec((HALO, C_IN), lambda i: (jnp.minimum((i + 1) * per, nh - 1), 0)),
                  pl.BlockSpec((3, C_WIDTH), lambda i: (0, 0))],
        out_specs=pl.BlockSpec((t, C_WIDTH), lambda i: (i, 0)),
        out_shape=jax.ShapeDtypeStruct((n, C_WIDTH), BF16),
        scratch_shapes=[pltpu.VMEM((t + 2 * HALO, C_WIDTH), F32)],
        compiler_params=_params(("parallel",)),
        name="short_conv",
    )(proj, proj, proj, w)


def _win_attn_kernel(sink_ref, q_ref, kp_ref, km_ref, kn_ref, vp_ref, vm_ref, vn_ref, c_ref, o_ref, *, tq, n):
    i = pl.program_id(0)
    lane = lax.broadcasted_iota(jnp.int32, (tq, LANES), 1)
    low = lane < HEAD_DIM

    def with_ones(v, keep_low):
        col = lax.broadcasted_iota(jnp.int32, v.shape, 1)
        return jnp.where((col < HEAD_DIM) if keep_low else (col >= HEAD_DIM), v, jnp.ones_like(v))

    nb = tq // WINDOW
    span = 3 * WINDOW
    row = lax.broadcasted_iota(jnp.int32, (WINDOW, span), 0)
    col = lax.broadcasted_iota(jnp.int32, (WINDOW, span), 1)
    band = (col >= row) & (col <= row + 2 * WINDOW)
    masks = []
    for blk in range(nb):
        pos = i * tq + (blk - 1) * WINDOW + col
        masks.append(band & (pos >= 0) & (pos < n))
    zeros = jnp.zeros((WINDOW, WINDOW), BF16)

    def scores(kv):
        tile = slice(kv * LANES, (kv + 1) * LANES)
        kw = jnp.concatenate([kp_ref[:, tile], km_ref[:, tile], kn_ref[:, tile]], axis=0)
        kc = c_ref[:, tile]
        qms = []
        for g in range(D_GROUP):
            c0 = (kv * D_GROUP + g) // 2 * LANES
            qa = q_ref[:, c0:c0 + LANES]
            qms.append(jnp.where(low if g % 2 == 0 else (lane >= HEAD_DIM), qa, jnp.zeros_like(qa)))
        q_all = jnp.concatenate(qms, axis=0)
        return _dot_nt(q_all, kc), _dot_nt(q_all, kw)

    def softmax(kv, s_c_all, s_w_all):
        probs = []
        for g in range(D_GROUP):
            sk = sink_ref[kv * D_GROUP + g] * LOG2E
            p_cs, p_ws, sinks = [], [], []
            for blk in range(nb):
                rows = slice(g * tq + blk * WINDOW, g * tq + (blk + 1) * WINDOW)
                s_c = s_c_all[rows]
                s_w = jnp.where(masks[blk], s_w_all[rows, blk * WINDOW:blk * WINDOW + span], NEG_INF)
                mx = jnp.maximum(jnp.maximum(jnp.max(s_c, axis=1, keepdims=True),
                                             jnp.max(s_w, axis=1, keepdims=True)), sk)
                p_cs.append(jnp.exp2(s_c - mx).astype(BF16))
                p_ws.append(jnp.concatenate([zeros] * blk + [jnp.exp2(s_w - mx).astype(BF16)]
                                            + [zeros] * (nb - 1 - blk), axis=1))
                sinks.append(jnp.exp2(sk - mx))
            probs.append((jnp.concatenate(p_cs, axis=0), jnp.concatenate(p_ws, axis=0),
                          jnp.concatenate(sinks, axis=0)))
        return probs

    def outputs(kv, probs):
        tile = slice(kv * LANES, (kv + 1) * LANES)
        vw = jnp.concatenate([vp_ref[:, tile], vm_ref[:, tile], vn_ref[:, tile]], axis=0)
        vc = c_ref[:, D_KV_HEADS * LANES + kv * LANES:D_KV_HEADS * LANES + (kv + 1) * LANES]
        outs = [None] * D_GROUP
        for parity in range(2):
            heads = list(range(parity, D_GROUP, 2))
            p_c = jnp.concatenate([probs[g][0] for g in heads], axis=0)
            p_w = jnp.concatenate([probs[g][1] for g in heads], axis=0)
            p_sink = jnp.concatenate([probs[g][2] for g in heads], axis=0)
            o = (jnp.dot(p_c, with_ones(vc, parity == 0), preferred_element_type=F32)
                 + jnp.dot(p_w, with_ones(vw, parity == 0), preferred_element_type=F32))
            o = o / (pltpu.roll(o, HEAD_DIM, 1) + p_sink)
            for idx, g in enumerate(heads):
                outs[g] = o[idx * tq:(idx + 1) * tq]
        for a in range(D_GROUP // 2):
            c0 = (kv * D_GROUP // 2 + a) * LANES
            o_ref[:, c0:c0 + LANES] = jnp.where(low, outs[2 * a], outs[2 * a + 1]).astype(BF16)

    for kv in range(D_KV_HEADS):
        outputs(kv, softmax(kv, *scores(kv)))


def _window_attention(sink, proj, q_col0, k_col0, v_col0, pc, *, tq):
    n = proj.shape[0]
    nc = pc.shape[0]
    kvw = D_KV_HEADS * LANES
    per = tq // WINDOW
    nw = n // WINDOW

    def window(col0):
        cb = col0 // kvw
        return [pl.BlockSpec((WINDOW, kvw), lambda i: (jnp.maximum(i * per - 1, 0), cb)),
                pl.BlockSpec((tq, kvw), lambda i: (i, cb)),
                pl.BlockSpec((WINDOW, kvw), lambda i: (jnp.minimum((i + 1) * per, nw - 1), cb))]

    return pl.pallas_call(
        functools.partial(_win_attn_kernel, tq=tq, n=n),
        grid=(n // tq,),
        in_specs=[pl.BlockSpec(memory_space=pltpu.SMEM),
                  pl.BlockSpec((tq, D_Q_WIDTH), lambda i: (i, q_col0 // D_Q_WIDTH))]
                 + window(k_col0) + window(v_col0)
                 + [pl.BlockSpec((nc, 2 * kvw), lambda i: (0, 0))],
        out_specs=pl.BlockSpec((tq, D_Q_WIDTH), lambda i: (i, 0)),
        out_shape=jax.ShapeDtypeStruct((n, D_Q_WIDTH), BF16),
        compiler_params=_params(("parallel",)),
        name="window_attention",
    )(sink, proj, proj, proj, proj, proj, proj, proj, pc)


def _dup_head_cols(w):
    d = w.shape[0]
    w = w.reshape(d, D_KV_HEADS, 1, HEAD_DIM)
    return jnp.broadcast_to(w, (d, D_KV_HEADS, 2, HEAD_DIM)).reshape(d, D_KV_HEADS * LANES)


OUT_ROWS = 256


def _out_kernel(a_ref, b_ref, x_ref, w_ref, gpost_ref, gate_ref, gpre_ref, sc_ref, sh_ref, x1_ref, hf_ref):
    half = a_ref.shape[1]
    for r in range(x_ref.shape[0] // OUT_ROWS):
        rows = slice(r * OUT_ROWS, (r + 1) * OUT_ROWS)
        y = (jnp.dot(a_ref[rows, :], w_ref[:half, :], preferred_element_type=F32)
             + jnp.dot(b_ref[rows, :], w_ref[half:, :], preferred_element_type=F32))
        x1 = x_ref[rows, :] + gate_ref[...] * (_rms(y) * gpost_ref[...])
        x1_ref[rows, :] = x1
        hf_ref[rows, :] = ((_rms(x1) * gpre_ref[...]) * (1.0 + sc_ref[...]) + sh_ref[...]).astype(BF16)


def _out_project(a, b, x, w, g_post, gate, g_pre, scale, shift, *, tm):
    n, d = x.shape
    half = a.shape[1]
    vec = pl.BlockSpec((1, d), lambda i: (0, 0))
    return pl.pallas_call(
        _out_kernel,
        grid=(n // tm,),
        in_specs=[pl.BlockSpec((tm, half), lambda i: (i, 0)),
                  pl.BlockSpec((tm, half), lambda i: (i, 0)),
                  pl.BlockSpec((tm, d), lambda i: (i, 0)),
                  pl.BlockSpec((2 * half, d), lambda i: (0, 0), pipeline_mode=pl.Buffered(1)),
                  vec, vec, vec, vec, vec],
        out_specs=[pl.BlockSpec((tm, d), lambda i: (i, 0)), pl.BlockSpec((tm, d), lambda i: (i, 0))],
        out_shape=[jax.ShapeDtypeStruct((n, d), F32), jax.ShapeDtypeStruct((n, d), BF16)],
        compiler_params=_params(("parallel",)),
        name="out_project",
    )(a, b, x, w, g_post, gate, g_pre, scale, shift)


def _mlp_kernel(hf_ref, w1_ref, w2_ref, x1_ref, gpost_ref, gate_ref, o_ref, acc_scr):
    f = pl.program_id(1)

    @pl.when(f == 0)
    def _():
        acc_scr[...] = jnp.zeros(acc_scr.shape, F32)

    h = jnp.dot(hf_ref[...], w1_ref[...], preferred_element_type=F32)
    h = jnp.square(jnp.maximum(h, 0.0)).astype(BF16)
    acc_scr[...] += jnp.dot(h, w2_ref[...], preferred_element_type=F32)

    @pl.when(f == pl.num_programs(1) - 1)
    def _():
        o_ref[...] = x1_ref[...] + gate_ref[...] * (_rms(acc_scr[...]) * gpost_ref[...])


def _mlp(hf, w1, w2, x1, g_post, gate, *, tm, tf):
    n, d = x1.shape
    ff = w1.shape[1]
    vec = pl.BlockSpec((1, d), lambda i, f: (0, 0))
    return pl.pallas_call(
        _mlp_kernel,
        grid=(n // tm, ff // tf),
        in_specs=[pl.BlockSpec((tm, d), lambda i, f: (i, 0)),
                  pl.BlockSpec((d, tf), lambda i, f: (0, f)),
                  pl.BlockSpec((tf, d), lambda i, f: (f, 0)),
                  pl.BlockSpec((tm, d), lambda i, f: (i, 0)),
                  vec, vec],
        out_specs=pl.BlockSpec((tm, d), lambda i, f: (i, 0)),
        out_shape=jax.ShapeDtypeStruct((n, d), F32),
        scratch_shapes=[pltpu.VMEM((tm, d), F32)],
        compiler_params=_params(("parallel", "arbitrary")),
        name="mlp",
    )(hf, w1, w2, x1, g_post, gate)


Q0_SCALE = ATTN_SCALE * LOG2E
Q1_SCALE = ATTN_SCALE * LOG2E
PLAIN = (0.0, 1.0)


def _rope_table(n):
    rows = n // GRID_W
    pairs = HEAD_DIM // 4
    inv = jnp.power(ROPE_BASE, -jnp.arange(pairs, dtype=F32) / pairs)
    ang_r = jnp.arange(rows, dtype=F32)[:, None] * inv
    ang_c = jnp.arange(GRID_W, dtype=F32)[:, None] * inv

    def grid(f):
        r = jnp.broadcast_to(f(ang_r)[:, None, :], (rows, GRID_W, pairs))
        c = jnp.broadcast_to(f(ang_c)[None, :, :], (rows, GRID_W, pairs))
        return jnp.concatenate([r, c], axis=-1).reshape(n, 2 * pairs)

    cos, sin = grid(jnp.cos), grid(jnp.sin)
    return jnp.stack([jnp.tile(cos, (1, 4)), jnp.tile(jnp.concatenate([-sin, sin], axis=-1), (1, 2))])


def _row(v):
    return v.reshape(1, -1)


def _split_mod(m):
    return [_row(t) for t in jnp.split(m, N_MOD)]


def kernel(x, c, ctx, c_ctx, l0_mod_w, l0_mod_b, l0_norm_mix_pre, l0_norm_mix_post, l0_norm_mlp_pre, l0_norm_mlp_post, l0_w_in, l0_conv_w, l0_conv_b, l0_ln_g, l0_ln_b, l0_lambda_q1, l0_lambda_k1, l0_lambda_q2, l0_lambda_k2, l0_subln_g, l0_w_out, l0_mlp_w1, l0_mlp_w2, l1_mod_w, l1_mod_b, l1_norm_mix_pre, l1_norm_mix_post, l1_norm_mlp_pre, l1_norm_mlp_post, l1_w_in, l1_sconv_w, l1_sink, l1_w_out, l1_mlp_w1, l1_mlp_w2):
    n = x.shape[1]
    nc = ctx.shape[1]
    xs = x[0]
    cs = ctx[0]
    rope = _rope_table(n)
    cvecs = jnp.zeros((8, D_MODEL), F32).at[0].set(c[0]).at[1].set(c_ctx)

    mod = _modulation(cvecs, l0_mod_w, l0_mod_b)
    sh_m, sc_m, gt_m, sh_f, sc_f, gt_f = _split_mod(mod[0])
    csh_m, csc_m, cgt_m, csh_f, csc_f, cgt_f = _split_mod(mod[1])
    w_in = l0_w_in.astype(BF16)
    g_pre, g_post = _row(l0_norm_mix_pre), _row(l0_norm_mix_post)
    gf_pre, gf_post = _row(l0_norm_mlp_pre), _row(l0_norm_mlp_post)
    q0, k0, v0 = A_IN, A_IN + B_WIDTH, A_IN + 2 * B_WIDTH

    blocks = lambda *widths_coefs: [c for w, c in widths_coefs for _ in range(w // PROJ_SUB)]
    proj = _project(xs, g_pre, sc_m, sh_m, w_in, rope,
                    blocks((A_IN, PLAIN), (B_WIDTH, (Q0_SCALE, 0.0)), (B_WIDTH, (1.0, 0.0)), (B_WIDTH, PLAIN)),
                    tm=512, tn=2560, out_rows=n + nc)
    pc, proj = _project(cs, g_pre, csc_m, csh_m, w_in, rope,
                        blocks((A_IN, PLAIN), (B_WIDTH, (0.0, Q0_SCALE)), (2 * B_WIDTH, PLAIN)),
                        tm=nc, tn=1024, fill=(proj, n))
    lam_init = 0.8 - 0.6 * math.exp(-0.3 * 0)
    lam_rows = jnp.zeros((8, LANES), F32)
    for r, lv in enumerate((l0_lambda_q1, l0_lambda_k1, l0_lambda_q2, l0_lambda_k2)):
        lam_rows = lam_rows.at[r, :HEAD_DIM].set(lv)

    a_lat = _conformer_conv(proj, l0_conv_w, l0_conv_b, l0_ln_g, l0_ln_b, n=n, t=512)
    a_ctx = _conformer_conv(pc, l0_conv_w, l0_conv_b, l0_ln_g, l0_ln_b, n=nc, t=nc)
    b_lat, (w_out, w1, w2, w_in_next, w_out_next, w1_next, w2_next) = _diff_attention(
        lam_rows, l0_subln_g, proj, q0, proj, k0, v0, n_q=n, tq=1024, tk=3328, lam_init=lam_init,
        cast=(l0_w_out, l0_mlp_w1, l0_mlp_w2, l1_w_in, l1_w_out, l1_mlp_w1, l1_mlp_w2))
    b_ctx, _ = _diff_attention(lam_rows, l0_subln_g, pc, q0, pc, k0, v0, n_q=nc, tq=nc, tk=nc, lam_init=lam_init)

    x1, hf = _out_project(a_lat, b_lat, xs, w_out, g_post, gt_m, gf_pre, sc_f, sh_f, tm=512)
    xs = _mlp(hf, w1, w2, x1, gf_post, gt_f, tm=512, tf=1024)
    c1, hcf = _out_project(a_ctx, b_ctx, cs, w_out, g_post, cgt_m, gf_pre, csc_f, csh_f, tm=nc)
    cs = _mlp(hcf, w1, w2, c1, gf_post, cgt_f, tm=nc, tf=1024)

    mod = _modulation(cvecs, l1_mod_w, l1_mod_b)
    sh_m, sc_m, gt_m, sh_f, sc_f, gt_f = _split_mod(mod[0])
    csh_m, csc_m = _split_mod(mod[1])[:2]
    w_out, w1, w2 = w_out_next, w1_next, w2_next
    g_pre, g_post = _row(l1_norm_mix_pre), _row(l1_norm_mix_post)
    gf_pre, gf_post = _row(l1_norm_mlp_pre), _row(l1_norm_mlp_post)
    q0, k0, v0 = C_IN, C_IN + D_Q_WIDTH, C_IN + D_Q_WIDTH + D_KV_WIDTH
    w_in = jnp.concatenate([w_in_next[:, :k0], _dup_head_cols(w_in_next[:, k0:v0]),
                            _dup_head_cols(w_in_next[:, v0:])], axis=1)
    kd0, vd0 = k0, k0 + 2 * D_KV_WIDTH

    proj = _project(xs, g_pre, sc_m, sh_m, w_in, rope,
                    blocks((C_IN, PLAIN), (D_Q_WIDTH, (Q1_SCALE, 0.0)), (2 * D_KV_WIDTH, (1.0, 0.0)),
                           (2 * D_KV_WIDTH, PLAIN)), tm=512, tn=2560)
    pc = _project(cs, g_pre, csc_m, csh_m, w_in[:, kd0:], rope, blocks((4 * D_KV_WIDTH, PLAIN)), tm=nc, tn=1024)
    c_lat = _short_conv(proj, l1_sconv_w, t=512)
    d_lat = _window_attention(l1_sink, proj, q0, kd0, vd0, pc, tq=512)
    x1, hf = _out_project(c_lat, d_lat, xs, w_out, g_post, gt_m, gf_pre, sc_f, sh_f, tm=512)
    xs = _mlp(hf, w1, w2, x1, gf_post, gt_f, tm=512, tf=1024)
    return xs[None]
```

```python
import functools
import math

import jax
import jax.numpy as jnp
from jax import lax
from jax.experimental import pallas as pl
from jax.experimental.pallas import tpu as pltpu

F32 = jnp.float32
BF16 = jnp.bfloat16

D_MODEL = 2048
HEAD_DIM = 64
HALF = HEAD_DIM // 2
GRID_W = 64
ROPE_BASE = 10000.0
NORM_EPS = 1e-6
LN_EPS = 1e-5
ATTN_SCALE = HEAD_DIM ** -0.5
LOG2E = math.log2(math.e)
NEG_INF = -1e30
N_MOD = 6
WINDOW = 128
LANES = 128
HALO = 16

A_WIDTH = 1024
A_IN = 2 * A_WIDTH
CONV_A_WIDTH = 31
B_HEADS = 8
B_WIDTH = 1024
C_WIDTH = 1024
C_IN = 3 * C_WIDTH
D_HEADS = 16
D_KV_HEADS = 4
D_GROUP = 4
D_Q_WIDTH = 1024
D_KV_WIDTH = 256
D_FF = 4 * D_MODEL

VMEM_LIMIT = 56 * 1024 * 1024


def _params(sem):
    return pltpu.CompilerParams(dimension_semantics=sem, vmem_limit_bytes=VMEM_LIMIT)


def _rms(x, eps=NORM_EPS):
    return x * lax.rsqrt(jnp.mean(x * x, axis=-1, keepdims=True) + eps)


def _dot_nt(a, b):
    return lax.dot_general(a, b, (((1,), (1,)), ((), ())), preferred_element_type=F32)


def _mod_kernel(c_ref, w_ref, b_ref, o_ref):
    c = c_ref[...]
    s = c * jax.nn.sigmoid(c)
    o_ref[...] = jnp.dot(s, w_ref[...], preferred_element_type=F32) + b_ref[...]


def _modulation(cvecs, w, b):
    tn = 1024
    n_out = w.shape[1]
    return pl.pallas_call(
        _mod_kernel,
        grid=(n_out // tn,),
        in_specs=[pl.BlockSpec((8, D_MODEL), lambda j: (0, 0)),
                  pl.BlockSpec((D_MODEL, tn), lambda j: (0, j)),
                  pl.BlockSpec((1, tn), lambda j: (0, j))],
        out_specs=pl.BlockSpec((8, tn), lambda j: (0, j)),
        out_shape=jax.ShapeDtypeStruct((8, n_out), F32),
        compiler_params=_params(("arbitrary",)),
        name="modulation",
    )(cvecs, w, b.reshape(1, n_out))


PROJ_SUB = 512
PROJ_ROWS = 256


def _proj_kernel(coef_ref, x_ref, g_ref, sc_ref, sh_ref, tab_ref, w_ref, *rest, tn, fill):
    outs, h_scr = (rest[1:3], rest[3]) if fill else (rest[:1], rest[1])
    nsub = tn // PROJ_SUB
    j = pl.program_id(1)

    @pl.when(j == 0)
    def _():
        y = _rms(x_ref[...]) * g_ref[...]
        h_scr[...] = (y * (1.0 + sc_ref[...]) + sh_ref[...]).astype(BF16)

    lane = lax.broadcasted_iota(jnp.int32, (PROJ_ROWS, LANES), 1)
    first_half = (lane & (HEAD_DIM - 1)) < HALF
    for r in range(x_ref.shape[0] // PROJ_ROWS):
        rows = slice(r * PROJ_ROWS, (r + 1) * PROJ_ROWS)
        acc = jnp.dot(h_scr[rows, :], w_ref[...], preferred_element_type=F32)
        cosf = tab_ref[0, rows, :]
        sinf = tab_ref[1, rows, :]
        for s in range(nsub):
            alpha = coef_ref[2 * (j * nsub + s)]
            beta = coef_ref[2 * (j * nsub + s) + 1]
            a = alpha * cosf + beta
            b = alpha * sinf
            for c in range(s * PROJ_SUB // LANES, (s + 1) * PROJ_SUB // LANES):
                y = acc[:, c * LANES:(c + 1) * LANES]
                swapped = jnp.where(first_half, pltpu.roll(y, LANES - HALF, 1), pltpu.roll(y, HALF, 1))
                res = (y * a + swapped * b).astype(BF16)
                for o_ref in outs:
                    o_ref[rows, c * LANES:(c + 1) * LANES] = res


def _project(x, g, scale, shift, w, rope_tab, coefs, *, tm, tn, out_rows=None, fill=None):
    n, d = x.shape
    p = w.shape[1]
    assert len(coefs) == p // PROJ_SUB
    vec = pl.BlockSpec((1, d), lambda i, j: (0, 0))
    row_blocks = pl.cdiv(out_rows or n, tm)
    src = lambda i: jnp.minimum(i, n // tm - 1)
    in_specs = [pl.BlockSpec(memory_space=pltpu.SMEM),
                pl.BlockSpec((tm, d), lambda i, j: (src(i), 0)), vec, vec, vec,
                pl.BlockSpec((2, tm, LANES), lambda i, j: (0, src(i), 0)),
                pl.BlockSpec((d, tn), lambda i, j: (0, j))]
    args = [jnp.asarray(coefs, F32).reshape(-1), x, g, scale, shift, rope_tab, w]
    out_specs = [pl.BlockSpec((tm, tn), lambda i, j: (i, j))]
    out_shape = [jax.ShapeDtypeStruct((out_rows or n, p), BF16)]
    aliases = {}
    if fill is not None:
        buf, row0 = fill
        blk0 = row0 // tm
        in_specs.append(pl.BlockSpec(memory_space=pl.ANY))
        args.append(buf)
        out_specs.append(pl.BlockSpec((tm, tn), lambda i, j: (blk0 + i, j)))
        out_shape.append(jax.ShapeDtypeStruct(buf.shape, BF16))
        aliases = {len(args) - 1: 1}
    res = pl.pallas_call(
        functools.partial(_proj_kernel, tn=tn, fill=fill is not None),
        grid=(row_blocks, p // tn),
        in_specs=in_specs,
        out_specs=out_specs,
        out_shape=out_shape,
        input_output_aliases=aliases,
        scratch_shapes=[pltpu.VMEM((tm, d), BF16)],
        compiler_params=_params(("parallel", "arbitrary")),
        name="norm_project",
    )(*args)
    return res if fill is not None else res[0]


CONV_A_ROWS = 32


def _conv_a_kernel(main_ref, left_ref, right_ref, w_ref, cb_ref, lg_ref, lb_ref, o_ref, ext_scr, sh_scr, *, t):
    i = pl.program_id(0)
    last = pl.num_programs(0) - 1

    def glu(ref):
        v = ref[...].astype(F32)
        return v[:, :A_WIDTH] * jax.nn.sigmoid(v[:, A_WIDTH:])

    ext_scr[0:HALO, :] = jnp.where(i > 0, glu(left_ref), 0.0)
    ext_scr[HALO:HALO + t, :] = glu(main_ref)
    ext_scr[HALO + t:, :] = jnp.where(i < last, glu(right_ref), 0.0)
    ext = ext_scr[...]
    rows = t + 2 * HALO
    srows = t + 3 * 8
    sh_scr[0] = ext[:srows]
    for b in range(1, 8):
        sh_scr[b] = pltpu.roll(ext, rows - b, 0)[:srows]

    def chunk(c, carry):
        r0 = pl.multiple_of(c * CONV_A_ROWS, CONV_A_ROWS)
        acc = jnp.zeros((CONV_A_ROWS, A_WIDTH), F32)
        for k in range(CONV_A_WIDTH):
            a, b = divmod(k + 1, 8)
            acc = acc + sh_scr[b, pl.ds(r0 + 8 * a, CONV_A_ROWS), :] * w_ref[k:k + 1, :]
        v = acc + cb_ref[...]
        mu = jnp.mean(v, axis=-1, keepdims=True)
        vc = v - mu
        var = jnp.mean(vc * vc, axis=-1, keepdims=True)
        y = vc * lax.rsqrt(var + LN_EPS) * lg_ref[...] + lb_ref[...]
        o_ref[pl.ds(r0, CONV_A_ROWS), :] = (y * jax.nn.sigmoid(y)).astype(BF16)
        return carry

    lax.fori_loop(0, t // CONV_A_ROWS, chunk, 0, unroll=2)


def _conformer_conv(proj, conv_w, conv_b, ln_g, ln_b, *, n, t):
    per = t // HALO
    nh = n // HALO
    vec = pl.BlockSpec((1, A_WIDTH), lambda i: (0, 0))
    return pl.pallas_call(
        functools.partial(_conv_a_kernel, t=t),
        grid=(n // t,),
        in_specs=[pl.BlockSpec((t, A_IN), lambda i: (i, 0)),
                  pl.BlockSpec((HALO, A_IN), lambda i: (jnp.maximum(i * per - 1, 0), 0)),
                  pl.BlockSpec((HALO, A_IN), lambda i: (jnp.minimum((i + 1) * per, nh - 1), 0)),
                  pl.BlockSpec((CONV_A_WIDTH, A_WIDTH), lambda i: (0, 0)),
                  vec, vec, vec],
        out_specs=pl.BlockSpec((t, A_WIDTH), lambda i: (i, 0)),
        out_shape=jax.ShapeDtypeStruct((n, A_WIDTH), BF16),
        scratch_shapes=[pltpu.VMEM((t + 2 * HALO, A_WIDTH), F32),
                        pltpu.VMEM((8, t + 24, A_WIDTH), F32)],
        compiler_params=_params(("parallel",)),
        name="conformer_conv",
    )(proj, proj, proj, conv_w, conv_b.reshape(1, -1), ln_g.reshape(1, -1), ln_b.reshape(1, -1))


def _tile_lanes(x, reps):
    return x if reps == 1 else jnp.concatenate([x] * reps, axis=1)


def _diff_attn_kernel(lam_ref, sub_ref, q_ref, k_ref, v_ref, *rest, lam_init, tk, ncast):
    cast_in, o_ref, cast_out = rest[:ncast], rest[ncast], rest[ncast + 1:2 * ncast + 1]
    qm_scr, m_scr, acc_scr = rest[2 * ncast + 1:]
    for w_ref, wb_ref in zip(cast_in, cast_out):
        wb_ref[...] = w_ref[...].astype(BF16)
    q = q_ref[...]
    lane = lax.broadcasted_iota(jnp.int32, q.shape, 1)
    zero = jnp.zeros_like(q)
    qm_scr[0] = jnp.where(lane < HEAD_DIM, q, zero)
    qm_scr[1] = jnp.where(lane >= HEAD_DIM, q, zero)
    m_scr[...] = jnp.full(m_scr.shape, -jnp.inf, F32)
    acc_scr[...] = jnp.zeros(acc_scr.shape, F32)
    ones = jnp.ones((tk, LANES), BF16)

    def chunk(c, carry):
        r0 = pl.multiple_of(c * tk, tk)
        k = k_ref[pl.ds(r0, tk), :]
        v_ext = jnp.concatenate([v_ref[pl.ds(r0, tk), :], ones], axis=1)
        scores = [_dot_nt(qm_scr[m], k) for m in range(2)]
        for m in range(2):
            s = scores[m]
            m_prev = m_scr[m]
            m_new = jnp.maximum(m_prev, jnp.max(s, axis=1, keepdims=True))
            alpha = jnp.exp2(m_prev - m_new)
            p = jnp.exp2(s - _tile_lanes(m_new, tk // LANES))
            acc_scr[m] = (_tile_lanes(alpha, 2) * acc_scr[m]
                          + jnp.dot(p.astype(BF16), v_ext, preferred_element_type=F32))
            m_scr[m] = m_new
        return carry

    lax.fori_loop(0, k_ref.shape[0] // tk, chunk, 0)

    lp = lam_ref[...]
    lam = (jnp.exp(jnp.sum(lp[0:1] * lp[1:2], axis=1, keepdims=True))
           - jnp.exp(jnp.sum(lp[2:3] * lp[3:4], axis=1, keepdims=True)) + lam_init)
    o = (acc_scr[0, :, :LANES] / acc_scr[0, :, LANES:]
         - lam * (acc_scr[1, :, :LANES] / acc_scr[1, :, LANES:]))
    o = _rms(o) * sub_ref[...] * (1.0 - lam_init)
    o_ref[...] = o.astype(BF16)


def _diff_attention(lam_rows, subln_g, q_src, q_col0, kv_src, k_col0, v_col0, *, n_q, tq, tk, lam_init, cast=()):
    nk = kv_src.shape[0]
    qb, kb, vb = q_col0 // LANES, k_col0 // LANES, v_col0 // LANES
    nq = n_q // tq
    steps = B_HEADS * nq
    slab = lambda w: pl.BlockSpec((w.shape[0] // steps, w.shape[1]), lambda h, i: (h * nq + i, 0))
    res = pl.pallas_call(
        functools.partial(_diff_attn_kernel, lam_init=lam_init, tk=tk, ncast=len(cast)),
        grid=(B_HEADS, nq),
        in_specs=[pl.BlockSpec((8, LANES), lambda h, i: (0, 0)),
                  pl.BlockSpec((1, LANES), lambda h, i: (0, 0)),
                  pl.BlockSpec((tq, LANES), lambda h, i: (i, qb + h)),
                  pl.BlockSpec((nk, LANES), lambda h, i: (0, kb + h)),
                  pl.BlockSpec((nk, LANES), lambda h, i: (0, vb + h))] + [slab(w) for w in cast],
        out_specs=[pl.BlockSpec((tq, LANES), lambda h, i: (i, h))] + [slab(w) for w in cast],
        out_shape=[jax.ShapeDtypeStruct((n_q, B_WIDTH), BF16)]
                  + [jax.ShapeDtypeStruct(w.shape, BF16) for w in cast],
        scratch_shapes=[pltpu.VMEM((2, tq, LANES), BF16),
                        pltpu.VMEM((2, tq, LANES), F32),
                        pltpu.VMEM((2, tq, 2 * LANES), F32)],
        compiler_params=_params(("parallel", "arbitrary")),
        name="diff_attention",
    )(lam_rows, subln_g.reshape(1, LANES), q_src, kv_src, kv_src, *cast)
    return res[0], res[1:]


def _short_conv_tile(main_ref, left_ref, right_ref, w_ref, ext_scr):
    t = main_ref.shape[0]
    i = pl.program_id(0)
    last = pl.num_programs(0) - 1

    def gated(ref):
        v = ref[...].astype(F32)
        return v[:, C_WIDTH:2 * C_WIDTH] * v[:, 2 * C_WIDTH:]

    ext_scr[0:HALO, :] = jnp.where(i > 0, gated(left_ref), 0.0)
    ext_scr[HALO:HALO + t, :] = gated(main_ref)
    ext_scr[HALO + t:, :] = jnp.where(i < last, gated(right_ref), 0.0)
    ext = ext_scr[...]
    rows = t + 2 * HALO
    prev = pltpu.roll(ext, 1, 0)[HALO:HALO + t]
    nxt = pltpu.roll(ext, rows - 1, 0)[HALO:HALO + t]
    conv = prev * w_ref[0:1, :] + ext[HALO:HALO + t] * w_ref[1:2, :] + nxt * w_ref[2:3, :]
    return (main_ref[:, :C_WIDTH].astype(F32) * conv).astype(BF16)


def _win_attn_kernel(sink_ref, q_ref, kp_ref, km_ref, kn_ref, vp_ref, vm_ref, vn_ref, c_ref, o_ref, *, tq, n):
    i = pl.program_id(0)
    lane = lax.broadcasted_iota(jnp.int32, (tq, LANES), 1)
    low = lane < HEAD_DIM

    def with_ones(v, keep_low):
        col = lax.broadcasted_iota(jnp.int32, v.shape, 1)
        return jnp.where((col < HEAD_DIM) if keep_low else (col >= HEAD_DIM), v, jnp.ones_like(v))

    nb = tq // WINDOW
    span = 3 * WINDOW
    row = lax.broadcasted_iota(jnp.int32, (WINDOW, span), 0)
    col = lax.broadcasted_iota(jnp.int32, (WINDOW, span), 1)
    band = (col >= row) & (col <= row + 2 * WINDOW)
    masks = []
    for blk in range(nb):
        pos = i * tq + (blk - 1) * WINDOW + col
        masks.append(band & (pos >= 0) & (pos < n))
    zeros = jnp.zeros((WINDOW, WINDOW), BF16)

    def scores(kv):
        tile = slice(kv * LANES, (kv + 1) * LANES)
        kw = jnp.concatenate([kp_ref[:, tile], km_ref[:, tile], kn_ref[:, tile]], axis=0)
        kc = c_ref[:, tile]
        qms = []
        for g in range(D_GROUP):
            c0 = (kv * D_GROUP + g) // 2 * LANES
            qa = q_ref[:, c0:c0 + LANES]
            qms.append(jnp.where(low if g % 2 == 0 else (lane >= HEAD_DIM), qa, jnp.zeros_like(qa)))
        q_all = jnp.concatenate(qms, axis=0)
        return _dot_nt(q_all, kc), _dot_nt(q_all, kw)

    def softmax(kv, s_c_all, s_w_all):
        probs = []
        for g in range(D_GROUP):
            sk = sink_ref[kv * D_GROUP + g] * LOG2E
            p_cs, p_ws, sinks = [], [], []
            for blk in range(nb):
                rows = slice(g * tq + blk * WINDOW, g * tq + (blk + 1) * WINDOW)
                s_c = s_c_all[rows]
                s_w = jnp.where(masks[blk], s_w_all[rows, blk * WINDOW:blk * WINDOW + span], NEG_INF)
                mx = jnp.maximum(jnp.maximum(jnp.max(s_c, axis=1, keepdims=True),
                                             jnp.max(s_w, axis=1, keepdims=True)), sk)
                p_cs.append(jnp.exp2(s_c - mx).astype(BF16))
                p_ws.append(jnp.concatenate([zeros] * blk + [jnp.exp2(s_w - mx).astype(BF16)]
                                            + [zeros] * (nb - 1 - blk), axis=1))
                sinks.append(jnp.exp2(sk - mx))
            probs.append((jnp.concatenate(p_cs, axis=0), jnp.concatenate(p_ws, axis=0),
                          jnp.concatenate(sinks, axis=0)))
        return probs

    def outputs(kv, probs):
        tile = slice(kv * LANES, (kv + 1) * LANES)
        vw = jnp.concatenate([vp_ref[:, tile], vm_ref[:, tile], vn_ref[:, tile]], axis=0)
        vc = c_ref[:, D_KV_HEADS * LANES + kv * LANES:D_KV_HEADS * LANES + (kv + 1) * LANES]
        outs = [None] * D_GROUP
        for parity in range(2):
            heads = list(range(parity, D_GROUP, 2))
            p_c = jnp.concatenate([probs[g][0] for g in heads], axis=0)
            p_w = jnp.concatenate([probs[g][1] for g in heads], axis=0)
            p_sink = jnp.concatenate([probs[g][2] for g in heads], axis=0)
            o = (jnp.dot(p_c, with_ones(vc, parity == 0), preferred_element_type=F32)
                 + jnp.dot(p_w, with_ones(vw, parity == 0), preferred_element_type=F32))
            o = o / (pltpu.roll(o, HEAD_DIM, 1) + p_sink)
            for idx, g in enumerate(heads):
                outs[g] = o[idx * tq:(idx + 1) * tq]
        for a in range(D_GROUP // 2):
            c0 = (kv * D_GROUP // 2 + a) * LANES
            o_ref[:, c0:c0 + LANES] = jnp.where(low, outs[2 * a], outs[2 * a + 1]).astype(BF16)

    for kv in range(D_KV_HEADS):
        outputs(kv, softmax(kv, *scores(kv)))


def _window_attention(sink, proj, q_col0, k_col0, v_col0, pc, *, tq):
    n = proj.shape[0]
    nc = pc.shape[0]
    kvw = D_KV_HEADS * LANES
    per = tq // WINDOW
    nw = n // WINDOW

    def window(col0):
        cb = col0 // kvw
        return [pl.BlockSpec((WINDOW, kvw), lambda i: (jnp.maximum(i * per - 1, 0), cb)),
                pl.BlockSpec((tq, kvw), lambda i: (i, cb)),
                pl.BlockSpec((WINDOW, kvw), lambda i: (jnp.minimum((i + 1) * per, nw - 1), cb))]

    return pl.pallas_call(
        functools.partial(_win_attn_kernel, tq=tq, n=n),
        grid=(n // tq,),
        in_specs=[pl.BlockSpec(memory_space=pltpu.SMEM),
                  pl.BlockSpec((tq, D_Q_WIDTH), lambda i: (i, q_col0 // D_Q_WIDTH))]
                 + window(k_col0) + window(v_col0)
                 + [pl.BlockSpec((nc, 2 * kvw), lambda i: (0, 0))],
        out_specs=pl.BlockSpec((tq, D_Q_WIDTH), lambda i: (i, 0)),
        out_shape=jax.ShapeDtypeStruct((n, D_Q_WIDTH), BF16),
        compiler_params=_params(("parallel",)),
        name="window_attention",
    )(sink, proj, proj, proj, proj, proj, proj, proj, pc)


def _dup_head_cols(w):
    d = w.shape[0]
    w = w.reshape(d, D_KV_HEADS, 1, HEAD_DIM)
    return jnp.broadcast_to(w, (d, D_KV_HEADS, 2, HEAD_DIM)).reshape(d, D_KV_HEADS * LANES)


OUT_ROWS = 256


def _out_kernel(*refs, fused_conv):
    if fused_conv:
        main_ref, left_ref, right_ref, cw_ref = refs[:4]
        a_val = _short_conv_tile(main_ref, left_ref, right_ref, cw_ref, refs[-1])
        refs = refs[4:-1]
        b_ref, x_ref, w_ref, gpost_ref, gate_ref, gpre_ref, sc_ref, sh_ref, x1_ref, hf_ref = refs
        half = b_ref.shape[1]
    else:
        a_ref, b_ref, x_ref, w_ref, gpost_ref, gate_ref, gpre_ref, sc_ref, sh_ref, x1_ref, hf_ref = refs
        half = a_ref.shape[1]
    for r in range(x_ref.shape[0] // OUT_ROWS):
        rows = slice(r * OUT_ROWS, (r + 1) * OUT_ROWS)
        a = a_val[rows] if fused_conv else a_ref[rows, :]
        y = (jnp.dot(a, w_ref[:half, :], preferred_element_type=F32)
             + jnp.dot(b_ref[rows, :], w_ref[half:, :], preferred_element_type=F32))
        x1 = x_ref[rows, :] + gate_ref[...] * (_rms(y) * gpost_ref[...])
        x1_ref[rows, :] = x1
        hf_ref[rows, :] = ((_rms(x1) * gpre_ref[...]) * (1.0 + sc_ref[...]) + sh_ref[...]).astype(BF16)


def _out_project(a, b, x, w, g_post, gate, g_pre, scale, shift, *, tm, conv_w=None):
    n, d = x.shape
    half = b.shape[1]
    vec = pl.BlockSpec((1, d), lambda i: (0, 0))
    fused = conv_w is not None
    if fused:
        per, nh = tm // HALO, n // HALO
        a_specs = [pl.BlockSpec((tm, C_IN), lambda i: (i, 0)),
                   pl.BlockSpec((HALO, C_IN), lambda i: (jnp.maximum(i * per - 1, 0), 0)),
                   pl.BlockSpec((HALO, C_IN), lambda i: (jnp.minimum((i + 1) * per, nh - 1), 0)),
                   pl.BlockSpec((3, C_WIDTH), lambda i: (0, 0))]
        a_args = [a, a, a, conv_w]
        scratch = [pltpu.VMEM((tm + 2 * HALO, C_WIDTH), F32)]
    else:
        a_specs, a_args, scratch = [pl.BlockSpec((tm, half), lambda i: (i, 0))], [a], []
    return pl.pallas_call(
        functools.partial(_out_kernel, fused_conv=fused),
        grid=(n // tm,),
        in_specs=a_specs + [pl.BlockSpec((tm, half), lambda i: (i, 0)),
                            pl.BlockSpec((tm, d), lambda i: (i, 0)),
                            pl.BlockSpec((2 * half, d), lambda i: (0, 0), pipeline_mode=pl.Buffered(1)),
                            vec, vec, vec, vec, vec],
        out_specs=[pl.BlockSpec((tm, d), lambda i: (i, 0)), pl.BlockSpec((tm, d), lambda i: (i, 0))],
        out_shape=[jax.ShapeDtypeStruct((n, d), F32), jax.ShapeDtypeStruct((n, d), BF16)],
        scratch_shapes=scratch,
        compiler_params=_params(("parallel",)),
        name="out_project",
    )(*a_args, b, x, w, g_post, gate, g_pre, scale, shift)


def _mlp_kernel(hf_ref, w1_ref, w2_ref, x1_ref, gpost_ref, gate_ref, o_ref, acc_scr):
    f = pl.program_id(1)

    @pl.when(f == 0)
    def _():
        acc_scr[...] = jnp.zeros(acc_scr.shape, F32)

    h = jnp.dot(hf_ref[...], w1_ref[...], preferred_element_type=F32)
    h = jnp.square(jnp.maximum(h, 0.0)).astype(BF16)
    acc_scr[...] += jnp.dot(h, w2_ref[...], preferred_element_type=F32)

    @pl.when(f == pl.num_programs(1) - 1)
    def _():
        o_ref[...] = x1_ref[...] + gate_ref[...] * (_rms(acc_scr[...]) * gpost_ref[...])


def _mlp(hf, w1, w2, x1, g_post, gate, *, tm, tf):
    n, d = x1.shape
    ff = w1.shape[1]
    vec = pl.BlockSpec((1, d), lambda i, f: (0, 0))
    return pl.pallas_call(
        _mlp_kernel,
        grid=(n // tm, ff // tf),
        in_specs=[pl.BlockSpec((tm, d), lambda i, f: (i, 0)),
                  pl.BlockSpec((d, tf), lambda i, f: (0, f)),
                  pl.BlockSpec((tf, d), lambda i, f: (f, 0)),
                  pl.BlockSpec((tm, d), lambda i, f: (i, 0)),
                  vec, vec],
        out_specs=pl.BlockSpec((tm, d), lambda i, f: (i, 0)),
        out_shape=jax.ShapeDtypeStruct((n, d), F32),
        scratch_shapes=[pltpu.VMEM((tm, d), F32)],
        compiler_params=_params(("parallel", "arbitrary")),
        name="mlp",
    )(hf, w1, w2, x1, g_post, gate)


Q0_SCALE = ATTN_SCALE * LOG2E
Q1_SCALE = ATTN_SCALE * LOG2E
PLAIN = (0.0, 1.0)


def _rope_table(n):
    rows = n // GRID_W
    pairs = HEAD_DIM // 4
    inv = jnp.power(ROPE_BASE, -jnp.arange(pairs, dtype=F32) / pairs)
    ang_r = jnp.arange(rows, dtype=F32)[:, None] * inv
    ang_c = jnp.arange(GRID_W, dtype=F32)[:, None] * inv

    def grid(f):
        r = jnp.broadcast_to(f(ang_r)[:, None, :], (rows, GRID_W, pairs))
        c = jnp.broadcast_to(f(ang_c)[None, :, :], (rows, GRID_W, pairs))
        return jnp.concatenate([r, c], axis=-1).reshape(n, 2 * pairs)

    cos, sin = grid(jnp.cos), grid(jnp.sin)
    return jnp.stack([jnp.tile(cos, (1, 4)), jnp.tile(jnp.concatenate([-sin, sin], axis=-1), (1, 2))])


def _row(v):
    return v.reshape(1, -1)


def _split_mod(m):
    return [_row(t) for t in jnp.split(m, N_MOD)]


def kernel(x, c, ctx, c_ctx, l0_mod_w, l0_mod_b, l0_norm_mix_pre, l0_norm_mix_post, l0_norm_mlp_pre, l0_norm_mlp_post, l0_w_in, l0_conv_w, l0_conv_b, l0_ln_g, l0_ln_b, l0_lambda_q1, l0_lambda_k1, l0_lambda_q2, l0_lambda_k2, l0_subln_g, l0_w_out, l0_mlp_w1, l0_mlp_w2, l1_mod_w, l1_mod_b, l1_norm_mix_pre, l1_norm_mix_post, l1_norm_mlp_pre, l1_norm_mlp_post, l1_w_in, l1_sconv_w, l1_sink, l1_w_out, l1_mlp_w1, l1_mlp_w2):
    n = x.shape[1]
    nc = ctx.shape[1]
    xs = x[0]
    cs = ctx[0]
    rope = _rope_table(n)
    cvecs = jnp.zeros((8, D_MODEL), F32).at[0].set(c[0]).at[1].set(c_ctx)

    mod = _modulation(cvecs, l0_mod_w, l0_mod_b)
    sh_m, sc_m, gt_m, sh_f, sc_f, gt_f = _split_mod(mod[0])
    csh_m, csc_m, cgt_m, csh_f, csc_f, cgt_f = _split_mod(mod[1])
    w_in = l0_w_in.astype(BF16)
    g_pre, g_post = _row(l0_norm_mix_pre), _row(l0_norm_mix_post)
    gf_pre, gf_post = _row(l0_norm_mlp_pre), _row(l0_norm_mlp_post)
    q0, k0, v0 = A_IN, A_IN + B_WIDTH, A_IN + 2 * B_WIDTH

    blocks = lambda *widths_coefs: [c for w, c in widths_coefs for _ in range(w // PROJ_SUB)]
    proj = _project(xs, g_pre, sc_m, sh_m, w_in, rope,
                    blocks((A_IN, PLAIN), (B_WIDTH, (Q0_SCALE, 0.0)), (B_WIDTH, (1.0, 0.0)), (B_WIDTH, PLAIN)),
                    tm=512, tn=2560, out_rows=n + nc)
    pc, proj = _project(cs, g_pre, csc_m, csh_m, w_in, rope,
                        blocks((A_IN, PLAIN), (B_WIDTH, (0.0, Q0_SCALE)), (2 * B_WIDTH, PLAIN)),
                        tm=nc, tn=1024, fill=(proj, n))
    lam_init = 0.8 - 0.6 * math.exp(-0.3 * 0)
    lam_rows = jnp.zeros((8, LANES), F32)
    for r, lv in enumerate((l0_lambda_q1, l0_lambda_k1, l0_lambda_q2, l0_lambda_k2)):
        lam_rows = lam_rows.at[r, :HEAD_DIM].set(lv)

    a_lat = _conformer_conv(proj, l0_conv_w, l0_conv_b, l0_ln_g, l0_ln_b, n=n, t=512)
    a_ctx = _conformer_conv(pc, l0_conv_w, l0_conv_b, l0_ln_g, l0_ln_b, n=nc, t=nc)
    b_lat, (w_out, w1, w2, w_in_next, w_out_next, w1_next, w2_next) = _diff_attention(
        lam_rows, l0_subln_g, proj, q0, proj, k0, v0, n_q=n, tq=1024, tk=3328, lam_init=lam_init,
        cast=(l0_w_out, l0_mlp_w1, l0_mlp_w2, l1_w_in, l1_w_out, l1_mlp_w1, l1_mlp_w2))
    b_ctx, _ = _diff_attention(lam_rows, l0_subln_g, pc, q0, pc, k0, v0, n_q=nc, tq=nc, tk=nc, lam_init=lam_init)

    x1, hf = _out_project(a_lat, b_lat, xs, w_out, g_post, gt_m, gf_pre, sc_f, sh_f, tm=512)
    xs = _mlp(hf, w1, w2, x1, gf_post, gt_f, tm=512, tf=1024)
    c1, hcf = _out_project(a_ctx, b_ctx, cs, w_out, g_post, cgt_m, gf_pre, csc_f, csh_f, tm=nc)
    cs = _mlp(hcf, w1, w2, c1, gf_post, cgt_f, tm=nc, tf=1024)

    mod = _modulation(cvecs, l1_mod_w, l1_mod_b)
    sh_m, sc_m, gt_m, sh_f, sc_f, gt_f = _split_mod(mod[0])
    csh_m, csc_m = _split_mod(mod[1])[:2]
    w_out, w1, w2 = w_out_next, w1_next, w2_next
    g_pre, g_post = _row(l1_norm_mix_pre), _row(l1_norm_mix_post)
    gf_pre, gf_post = _row(l1_norm_mlp_pre), _row(l1_norm_mlp_post)
    q0, k0, v0 = C_IN, C_IN + D_Q_WIDTH, C_IN + D_Q_WIDTH + D_KV_WIDTH
    w_in = jnp.concatenate([w_in_next[:, :k0], _dup_head_cols(w_in_next[:, k0:v0]),
                            _dup_head_cols(w_in_next[:, v0:])], axis=1)
    kd0, vd0 = k0, k0 + 2 * D_KV_WIDTH

    proj = _project(xs, g_pre, sc_m, sh_m, w_in, rope,
                    blocks((C_IN, PLAIN), (D_Q_WIDTH, (Q1_SCALE, 0.0)), (2 * D_KV_WIDTH, (1.0, 0.0)),
                           (2 * D_KV_WIDTH, PLAIN)), tm=512, tn=2560)
    pc = _project(cs, g_pre, csc_m, csh_m, w_in[:, kd0:], rope, blocks((4 * D_KV_WIDTH, PLAIN)), tm=nc, tn=1024)
    d_lat = _window_attention(l1_sink, proj, q0, kd0, vd0, pc, tq=512)
    x1, hf = _out_project(proj, d_lat, xs, w_out, g_post, gt_m, gf_pre, sc_f, sh_f, tm=512, conv_w=l1_sconv_w)
    xs = _mlp(hf, w1, w2, x1, gf_post, gt_f, tm=512, tf=1024)
    return xs[None]
```
